```python
import jax, jax.numpy as jnp
from jax import lax
import numpy as np

D_MODEL = 1024
BATCH = 8
SEQ = 4096
DEPTH = 1

GRID_W = 64
N_Q_HEADS = 8
N_KV_HEADS = 2
Q_PER_KV = N_Q_HEADS // N_KV_HEADS
HEAD_DIM = 64
AXIAL_DIM = HEAD_DIM // 2
ROPE_THETA = 10000.0
Q_BLOCK = 128
GLA_HEADS = 4
GLA_DK = 64
GLA_DV = 128
GATE_RANK = 16
GATE_TAU = 16.0
GLA_CHUNK = 16
ATTN_WIDTH = N_Q_HEADS * HEAD_DIM
GLA_WIDTH = GLA_HEADS * GLA_DV
MIX_WIDTH = ATTN_WIDTH + GLA_WIDTH
D_FF = -((-8 * D_MODEL) // (3 * 256)) * 256
PROJ_SIZES = (ATTN_WIDTH, N_KV_HEADS * HEAD_DIM, N_KV_HEADS * HEAD_DIM,
              GLA_HEADS * GLA_DK, GLA_HEADS * GLA_DK, GLA_WIDTH, GLA_WIDTH, GATE_RANK, GATE_RANK)
PROJ_WIDTH = sum(PROJ_SIZES)
PROJ_SPLITS = tuple(np.cumsum(PROJ_SIZES)[:-1].tolist())
DEEPNORM_ALPHA = (2 * DEPTH) ** 0.25
DEEPNORM_BETA = (8 * DEPTH) ** -0.25
LN_EPS = 1e-5
RMS_EPS = 1e-6

kernel_name = "hybrid_gqa_gla_deepnorm_encoder_block"


def layer_norm(x, g, b):
    xf = x.astype(jnp.float32)
    mu = jnp.mean(xf, -1, keepdims=True)
    var = jnp.mean(jnp.square(xf - mu), -1, keepdims=True)
    return ((xf - mu) * lax.rsqrt(var + LN_EPS) * g + b).astype(x.dtype)


def rms_norm(x, g):
    xf = x.astype(jnp.float32)
    return (xf * lax.rsqrt(jnp.mean(jnp.square(xf), -1, keepdims=True) + RMS_EPS) * g).astype(x.dtype)


def axial_rope_tables(seq_len):
    rows = seq_len // GRID_W
    row_id = jnp.repeat(jnp.arange(rows, dtype=jnp.float32), GRID_W)
    col_id = jnp.tile(jnp.arange(GRID_W, dtype=jnp.float32), rows)
    inv_freq = ROPE_THETA ** (-jnp.arange(0, AXIAL_DIM, 2, dtype=jnp.float32) / AXIAL_DIM)
    ang = jnp.concatenate([row_id[:, None] * inv_freq, col_id[:, None] * inv_freq], -1)
    return jnp.cos(ang), jnp.sin(ang)


def apply_axial_rope(x, cos, sin):
    xf = x.astype(jnp.float32)
    xr = xf.reshape(xf.shape[:-1] + (2, 2, AXIAL_DIM // 2))
    x1 = xr[..., 0, :]
    x2 = xr[..., 1, :]
    c = cos.reshape(cos.shape[0], 2, AXIAL_DIM // 2)
    s = sin.reshape(sin.shape[0], 2, AXIAL_DIM // 2)
    out = jnp.stack([x1 * c - x2 * s, x2 * c + x1 * s], axis=-2)
    return out.reshape(x.shape).astype(x.dtype)


def blocked_attention(q, k, v):
    B, G, R, S, D = q.shape
    nb = S // Q_BLOCK
    qb = jnp.moveaxis(q.reshape(B, G, R, nb, Q_BLOCK, D), 3, 0)

    def one_block(qblk):
        s = jnp.einsum('bgrqd,bgkd->bgrqk', qblk, k).astype(jnp.float32) * (HEAD_DIM ** -0.5)
        p = jax.nn.softmax(s, axis=-1)
        return jnp.einsum('bgrqk,bgkd->bgrqd', p.astype(v.dtype), v)

    ob = lax.map(one_block, qb)
    o = jnp.moveaxis(ob, 0, 3).reshape(B, G, R, S, D)
    return o.transpose(0, 3, 1, 2, 4).reshape(B, S, G * R * D)


def gla_direction(q, k, v, log_a, strict):
    B, H, S, DK = q.shape
    DV = v.shape[-1]
    C = GLA_CHUNK
    n = S // C
    q = q.reshape(B, H, n, C, DK)
    k = k.reshape(B, H, n, C, DK)
    v = v.reshape(B, H, n, C, DV)
    b = jnp.cumsum(log_a.reshape(B, H, n, C, DK), axis=3)
    b_last = b[:, :, :, -1:, :]
    q_dec = q * jnp.exp(b)
    k_inv = k * jnp.exp(-b)
    k_end = k * jnp.exp(b_last - b)
    mask = jnp.tril(jnp.ones((C, C), dtype=bool), -1 if strict else 0)
    scores = jnp.where(mask, jnp.einsum('bhnid,bhnjd->bhnij', q_dec, k_inv), 0.0)
    o_intra = jnp.einsum('bhnij,bhnjv->bhniv', scores, v)
    kv = jnp.einsum('bhnjd,bhnjv->bhndv', k_end, v)
    chunk_decay = jnp.exp(b_last[:, :, :, 0, :])

    def step(state, inp):
        dec, kv_c = inp
        return dec[..., None] * state + kv_c, state

    _, s_prev = lax.scan(step, jnp.zeros((B, H, DK, DV), jnp.float32),
                         (jnp.moveaxis(chunk_decay, 2, 0), jnp.moveaxis(kv, 2, 0)))
    s_prev = jnp.moveaxis(s_prev, 0, 2)
    o_inter = jnp.einsum('bhnid,bhndv->bhniv', q_dec, s_prev)
    return (o_intra + o_inter).reshape(B, H, S, DV)


def setup_inputs(seed: int = 0) -> dict:
    key = jax.random.key(seed)
    ks = jax.random.split(key, 18)
    f32 = jnp.float32

    def nrm(k, shape, scale):
        return jax.random.normal(k, shape, f32) * scale

    seg_scale = (1.0, 1.0, DEEPNORM_BETA, 1.0, 1.0, DEEPNORM_BETA, 1.0, 1.0, 1.0)
    col_scale = jnp.concatenate([jnp.full((n,), s, f32) for n, s in zip(PROJ_SIZES, seg_scale)])
    return {
        "x": nrm(ks[0], (BATCH, SEQ, D_MODEL), 1.0),
        "w_in": nrm(ks[1], (DEPTH, D_MODEL, PROJ_WIDTH), D_MODEL ** -0.5) * col_scale,
        "q_norm_g": 1.0 + nrm(ks[2], (DEPTH, HEAD_DIM), 0.02),
        "k_norm_g": 1.0 + nrm(ks[3], (DEPTH, HEAD_DIM), 0.02),
        "gate_up_fwd": nrm(ks[4], (DEPTH, GATE_RANK, GLA_HEADS * GLA_DK), GATE_RANK ** -0.5),
        "gate_bias_fwd": nrm(ks[5], (DEPTH, GLA_HEADS * GLA_DK), 0.1),
        "gate_up_bwd": nrm(ks[6], (DEPTH, GATE_RANK, GLA_HEADS * GLA_DK), GATE_RANK ** -0.5),
        "gate_bias_bwd": nrm(ks[7], (DEPTH, GLA_HEADS * GLA_DK), 0.1),
        "gla_norm_g": 1.0 + nrm(ks[8], (DEPTH, GLA_DV), 0.02),
        "w_out": nrm(ks[9], (DEPTH, MIX_WIDTH, D_MODEL), MIX_WIDTH ** -0.5) * DEEPNORM_BETA,
        "ln1_g": 1.0 + nrm(ks[10], (DEPTH, D_MODEL), 0.02),
        "ln1_b": nrm(ks[11], (DEPTH, D_MODEL), 0.02),
        "w_ffn_gate": nrm(ks[12], (DEPTH, D_MODEL, D_FF), D_MODEL ** -0.5) * DEEPNORM_BETA,
        "w_ffn_up": nrm(ks[13], (DEPTH, D_MODEL, D_FF), D_MODEL ** -0.5) * DEEPNORM_BETA,
        "w_ffn_down": nrm(ks[14], (DEPTH, D_FF, D_MODEL), D_FF ** -0.5) * DEEPNORM_BETA,
        "ln2_g": 1.0 + nrm(ks[15], (DEPTH, D_MODEL), 0.02),
        "ln2_b": nrm(ks[16], (DEPTH, D_MODEL), 0.02),
    }


def reference(x, w_in, q_norm_g, k_norm_g, gate_up_fwd, gate_bias_fwd, gate_up_bwd, gate_bias_bwd,
              gla_norm_g, w_out, ln1_g, ln1_b, w_ffn_gate, w_ffn_up, w_ffn_down, ln2_g, ln2_b):
    B, S, _ = x.shape
    cos, sin = axial_rope_tables(S)

    def to_gla_heads(t, d):
        return t.reshape(B, S, GLA_HEADS, d).transpose(0, 2, 1, 3).astype(jnp.float32)

    def flip(t):
        return jnp.flip(t, axis=2)

    for layer in range(DEPTH):
        proj = jnp.einsum('bsd,de->bse', x, w_in[layer])
        a_q, a_k, a_v, g_q, g_k, g_v, g_out, z_fwd, z_bwd = jnp.split(proj, PROJ_SPLITS, axis=-1)

        a_q = rms_norm(a_q.reshape(B, S, N_KV_HEADS, Q_PER_KV, HEAD_DIM), q_norm_g[layer]).transpose(0, 2, 3, 1, 4)
        a_k = rms_norm(a_k.reshape(B, S, N_KV_HEADS, HEAD_DIM), k_norm_g[layer]).transpose(0, 2, 1, 3)
        a_v = a_v.reshape(B, S, N_KV_HEADS, HEAD_DIM).transpose(0, 2, 1, 3)
        a_q = apply_axial_rope(a_q, cos, sin)
        a_k = apply_axial_rope(a_k, cos, sin)
        attn_out = blocked_attention(a_q, a_k, a_v)

        gq = to_gla_heads(g_q, GLA_DK) * (GLA_DK ** -0.5)
        gk = to_gla_heads(g_k, GLA_DK)
        gv = to_gla_heads(g_v, GLA_DV)
        la_f = to_gla_heads(jax.nn.log_sigmoid(
            (jnp.einsum('bsr,re->bse', z_fwd, gate_up_fwd[layer]) + gate_bias_fwd[layer]).astype(jnp.float32)) / GATE_TAU, GLA_DK)
        la_b = to_gla_heads(jax.nn.log_sigmoid(
            (jnp.einsum('bsr,re->bse', z_bwd, gate_up_bwd[layer]) + gate_bias_bwd[layer]).astype(jnp.float32)) / GATE_TAU, GLA_DK)
        o_f = gla_direction(gq, gk, gv, la_f, False)
        o_b = flip(gla_direction(flip(gq), flip(gk), flip(gv), flip(la_b), True))
        gla_o = (o_f + o_b).transpose(0, 2, 1, 3)
        gla_o = rms_norm(gla_o, gla_norm_g[layer]) * jax.nn.silu(
            g_out.reshape(B, S, GLA_HEADS, GLA_DV).astype(jnp.float32))
        gla_out = gla_o.reshape(B, S, GLA_WIDTH).astype(x.dtype)

        mixed = jnp.einsum('bse,ed->bsd', jnp.concatenate([attn_out, gla_out], axis=-1), w_out[layer])
        x = layer_norm(DEEPNORM_ALPHA * x + mixed, ln1_g[layer], ln1_b[layer])

        hidden = jax.nn.silu(jnp.einsum('bsd,df->bsf', x, w_ffn_gate[layer])) * jnp.einsum('bsd,df->bsf', x, w_ffn_up[layer])
        ffn = jnp.einsum('bsf,fd->bsd', hidden, w_ffn_down[layer])
        x = layer_norm(DEEPNORM_ALPHA * x + ffn, ln2_g[layer], ln2_b[layer])
    return x
```

```python
import functools

import jax
import jax.numpy as jnp
import numpy as np
from jax import lax
from jax.experimental import pallas as pl
from jax.experimental.pallas import tpu as pltpu

F32 = jnp.float32
BF16 = jnp.bfloat16

D_MODEL = 1024
GRID_W = 64
N_Q_HEADS = 8
N_KV_HEADS = 2
Q_PER_KV = N_Q_HEADS // N_KV_HEADS
HEAD_DIM = 64
AXIAL_DIM = HEAD_DIM // 2
ROPE_HALF = AXIAL_DIM // 2
ROPE_THETA = 10000.0
GLA_HEADS = 4
GLA_DK = 64
GLA_DV = 128
GATE_RANK = 16
GATE_TAU = 16.0
GLA_REF_CHUNK = 16
ATTN_WIDTH = N_Q_HEADS * HEAD_DIM
KV_WIDTH = N_KV_HEADS * HEAD_DIM
GLA_QK_WIDTH = GLA_HEADS * GLA_DK
GLA_WIDTH = GLA_HEADS * GLA_DV
D_FF = 2816
DEPTH = 1
DEEPNORM_ALPHA = (2 * DEPTH) ** 0.25
LN_EPS = 1e-5
RMS_EPS = 1e-6

LANES = 128
VMEM_LIMIT_BYTES = 56 * 1024 * 1024

PROJ_TM = 512
ATTN_TQ = 256
ATTN_TK = 512
GLA_CHUNK = 128
MIX_TM = 512
FFN_TM = 512
FFN_CHUNK = 256

OFF_AQ = 0
OFF_AK = OFF_AQ + ATTN_WIDTH
OFF_AV = OFF_AK + KV_WIDTH
OFF_GQ = OFF_AV + KV_WIDTH
OFF_GK = OFF_GQ + GLA_QK_WIDTH
OFF_GV = OFF_GK + GLA_QK_WIDTH
OFF_GO = OFF_GV + GLA_WIDTH
OFF_Z = OFF_GO + GLA_WIDTH
Z_PAD = LANES
PROJ_PAD_WIDTH = OFF_Z + Z_PAD


def _cparams(semantics):
    return pltpu.CompilerParams(dimension_semantics=semantics, vmem_limit_bytes=VMEM_LIMIT_BYTES)


def _dot(a, b):
    return jnp.dot(a, b, preferred_element_type=F32)


def _dot_nt(a, b):
    return lax.dot_general(a, b, (((1,), (1,)), ((), ())), preferred_element_type=F32)


def _dot_tn(a, b):
    return lax.dot_general(a, b, (((0,), (0,)), ((), ())), preferred_element_type=F32)


def _rope128(y, cos, sin_hi, sin_lo):
    return y * cos + pltpu.roll(y, ROPE_HALF, 1) * sin_hi + pltpu.roll(y, LANES - ROPE_HALF, 1) * sin_lo


def _head_pair_inv_rms(blk, lane_lo):
    sq = blk * blk
    ss_lo = jnp.sum(jnp.where(lane_lo, sq, 0.0), axis=-1, keepdims=True)
    ss_hi = jnp.sum(jnp.where(lane_lo, 0.0, sq), axis=-1, keepdims=True)
    inv = 1.0 / HEAD_DIM
    return jnp.where(lane_lo, lax.rsqrt(ss_lo * inv + RMS_EPS), lax.rsqrt(ss_hi * inv + RMS_EPS))


def _proj_kernel(x_ref, w_ref, gup_ref, gbias_ref, qg_ref, kg_ref, cos_ref, shi_ref, slo_ref,
                 qa_ref, ka_ref, va_ref, gq_ref, gk_ref, gv_ref, go_ref, laf_ref, lab_ref):
    x = x_ref[0].astype(BF16)
    tm = x.shape[0]
    lane = lax.broadcasted_iota(jnp.int32, (tm, LANES), 1)
    lane_lo = lane < HEAD_DIM
    cos, shi, slo = cos_ref[...], shi_ref[...], slo_ref[...]

    aq = _dot(x, w_ref[:, OFF_AQ:OFF_AQ + ATTN_WIDTH])
    for c in range(ATTN_WIDTH // LANES):
        blk = aq[:, c * LANES:(c + 1) * LANES]
        rot = _rope128(blk * qg_ref[...], cos, shi, slo)
        out = (rot * (_head_pair_inv_rms(blk, lane_lo) * (HEAD_DIM ** -0.5))).astype(BF16)
        qa_ref[0, 2 * c] = out[:, :HEAD_DIM]
        qa_ref[0, 2 * c + 1] = out[:, HEAD_DIM:]

    ak = _dot(x, w_ref[:, OFF_AK:OFF_AK + KV_WIDTH])
    rot = _rope128(ak * kg_ref[...], cos, shi, slo)
    out = (rot * _head_pair_inv_rms(ak, lane_lo)).astype(BF16)
    ka_ref[0, 0] = out[:, :HEAD_DIM]
    ka_ref[0, 1] = out[:, HEAD_DIM:]

    av = _dot(x, w_ref[:, OFF_AV:OFF_AV + KV_WIDTH])
    ones_col = jnp.where(lane == HEAD_DIM, 1.0, 0.0)
    va_ref[0, 0] = jnp.where(lane_lo, av, ones_col).astype(BF16)
    va_ref[0, 1] = jnp.where(lane_lo, pltpu.roll(av, HEAD_DIM, 1), ones_col).astype(BF16)

    gq_ref[0] = _dot(x, w_ref[:, OFF_GQ:OFF_GQ + GLA_QK_WIDTH]) * (GLA_DK ** -0.5)
    gk_ref[0] = _dot(x, w_ref[:, OFF_GK:OFF_GK + GLA_QK_WIDTH])
    gv_ref[0] = _dot(x, w_ref[:, OFF_GV:OFF_GV + GLA_WIDTH])
    go_ref[0] = _dot(x, w_ref[:, OFF_GO:OFF_GO + GLA_WIDTH])

    z = _dot(x, w_ref[:, OFF_Z:OFF_Z + Z_PAD])
    g = _dot(z.astype(BF16), gup_ref[...]) + gbias_ref[...]
    log_a = (jnp.minimum(g, 0.0) - jnp.log(1.0 + jnp.exp(-jnp.abs(g)))) * (1.0 / GATE_TAU)
    laf_ref[0] = log_a[:, :GLA_QK_WIDTH]
    lab_ref[0] = log_a[:, GLA_QK_WIDTH:]


def _proj_call(x, w_pad, gup, gbias, qg, kg, cos, shi, slo):
    B, S, D = x.shape
    tm = PROJ_TM
    grid = (S // tm, B)
    tok = lambda s, b: (b, s, 0)
    head = lambda s, b: (b, 0, s, 0)
    const = lambda s, b: (0, 0)
    tab = lambda s, b: (s, 0)
    in_specs = [
        pl.BlockSpec((1, tm, D), tok),
        pl.BlockSpec((D, PROJ_PAD_WIDTH), const),
        pl.BlockSpec((Z_PAD, 2 * GLA_QK_WIDTH), const),
        pl.BlockSpec((1, 2 * GLA_QK_WIDTH), const),
        pl.BlockSpec((1, LANES), const),
        pl.BlockSpec((1, LANES), const),
        pl.BlockSpec((tm, LANES), tab),
        pl.BlockSpec((tm, LANES), tab),
        pl.BlockSpec((tm, LANES), tab),
    ]
    out_shape = [
        jax.ShapeDtypeStruct((B, N_Q_HEADS, S, HEAD_DIM), BF16),
        jax.ShapeDtypeStruct((B, N_KV_HEADS, S, HEAD_DIM), BF16),
        jax.ShapeDtypeStruct((B, N_KV_HEADS, S, LANES), BF16),
        jax.ShapeDtypeStruct((B, S, GLA_QK_WIDTH), F32),
        jax.ShapeDtypeStruct((B, S, GLA_QK_WIDTH), F32),
        jax.ShapeDtypeStruct((B, S, GLA_WIDTH), F32),
        jax.ShapeDtypeStruct((B, S, GLA_WIDTH), F32),
        jax.ShapeDtypeStruct((B, S, GLA_QK_WIDTH), F32),
        jax.ShapeDtypeStruct((B, S, GLA_QK_WIDTH), F32),
    ]
    out_specs = [
        pl.BlockSpec((1, N_Q_HEADS, tm, HEAD_DIM), head),
        pl.BlockSpec((1, N_KV_HEADS, tm, HEAD_DIM), head),
        pl.BlockSpec((1, N_KV_HEADS, tm, LANES), head),
        pl.BlockSpec((1, tm, GLA_QK_WIDTH), tok),
        pl.BlockSpec((1, tm, GLA_QK_WIDTH), tok),
        pl.BlockSpec((1, tm, GLA_WIDTH), tok),
        pl.BlockSpec((1, tm, GLA_WIDTH), tok),
        pl.BlockSpec((1, tm, GLA_QK_WIDTH), tok),
        pl.BlockSpec((1, tm, GLA_QK_WIDTH), tok),
    ]
    return pl.pallas_call(
        _proj_kernel, grid=grid, in_specs=in_specs, out_specs=out_specs, out_shape=out_shape,
        compiler_params=_cparams(("arbitrary", "arbitrary")),
    )(x, w_pad, gup, gbias, qg, kg, cos, shi, slo)


def _attn_kernel(q_ref, k_ref, v_ref, o_ref, m_sc, acc_sc):
    ki = pl.program_id(3)
    tq = q_ref.shape[2]

    @pl.when(ki == 0)
    def _():
        m_sc[...] = jnp.full(m_sc.shape, -jnp.inf, F32)
        acc_sc[...] = jnp.zeros(acc_sc.shape, F32)

    q = q_ref[0].reshape(Q_PER_KV * tq, HEAD_DIM)
    s = _dot_nt(q, k_ref[0, 0])
    m_prev = m_sc[...]
    m_new = jnp.maximum(m_prev, jnp.max(s, axis=-1, keepdims=True))
    p = jnp.exp(s - m_new)
    acc_sc[...] = jnp.exp(m_prev - m_new) * acc_sc[...] + _dot(p.astype(BF16), v_ref[0, 0])
    m_sc[...] = m_new

    @pl.when(ki == pl.num_programs(3) - 1)
    def _():
        acc = acc_sc[...]
        o = acc[:, :HEAD_DIM] / acc[:, HEAD_DIM:HEAD_DIM + 1]
        for r in range(Q_PER_KV):
            o_ref[0, :, r * HEAD_DIM:(r + 1) * HEAD_DIM] = o[r * tq:(r + 1) * tq].astype(o_ref.dtype)


def _attn_call(qa, ka, va):
    B, _, S, _ = qa.shape
    tq, tk = ATTN_TQ, ATTN_TK
    grid = (B, N_KV_HEADS, S // tq, S // tk)
    return pl.pallas_call(
        _attn_kernel, grid=grid,
        in_specs=[
            pl.BlockSpec((1, Q_PER_KV, tq, HEAD_DIM), lambda b, g, qi, ki: (b, g, qi, 0)),
            pl.BlockSpec((1, 1, tk, HEAD_DIM), lambda b, g, qi, ki: (b, g, ki, 0)),
            pl.BlockSpec((1, 1, tk, LANES), lambda b, g, qi, ki: (b, g, ki, 0)),
        ],
        out_specs=pl.BlockSpec((1, tq, Q_PER_KV * HEAD_DIM), lambda b, g, qi, ki: (b, qi, g)),
        out_shape=jax.ShapeDtypeStruct((B, S, ATTN_WIDTH), BF16),
        scratch_shapes=[pltpu.VMEM((Q_PER_KV * tq, 1), F32), pltpu.VMEM((Q_PER_KV * tq, LANES), F32)],
        compiler_params=_cparams(("arbitrary", "arbitrary", "arbitrary", "arbitrary")),
    )(qa, ka, va)


def _block_row(a, blk, row):
    n, w = a.shape
    a3 = a.reshape(n // blk, blk, w)
    return jnp.broadcast_to(a3[:, row:row + 1, :], a3.shape).reshape(n, w)


def _gla_direction(q_ref, k_ref, v_ref, la_ref, o_ref, state_ref, forward):
    C = q_ref.shape[1]
    row = lax.broadcasted_iota(jnp.int32, (C, C), 0)
    col = lax.broadcasted_iota(jnp.int32, (C, C), 1)
    tri = (col <= row) if forward else (col >= row)
    la = la_ref[0]
    cum_all = jnp.dot(tri.astype(F32), la, preferred_element_type=F32, precision=lax.Precision.HIGHEST)
    excl_all = cum_all - la
    xr = row ^ col
    valid = (col <= row) if forward else (col > row)
    lane = lax.broadcasted_iota(jnp.int32, (C, LANES), 1)
    lane_lo = lane < GLA_DK
    edge = C - 1 if forward else 0

    for pair in range(GLA_HEADS // 2):
        sl = slice(pair * LANES, (pair + 1) * LANES)
        q, k = q_ref[0, :, sl], k_ref[0, :, sl]
        cum, excl = cum_all[:, sl], excl_all[:, sl]
        total = cum[edge:edge + 1, :]

        base_row = 0 if forward else GLA_REF_CHUNK - 1
        loc = cum - _block_row(excl, GLA_REF_CHUNK, base_row)
        q_lv = [q * jnp.exp(loc)]
        k_lv = [(k * jnp.exp(-loc)).astype(BF16)]
        blk = 2 * GLA_REF_CHUNK
        while blk <= C:
            mid = _block_row(excl if forward else cum, blk, blk // 2)
            q_lv.append(q * jnp.exp(jnp.minimum(cum - mid, 0.0)))
            k_lv.append((k * jnp.exp(jnp.minimum(mid - cum, 0.0))).astype(BF16))
            blk *= 2
        q_in = q * jnp.exp(cum)
        k_out = k * jnp.exp(total - cum)
        k_out_t = k_out.T.astype(BF16)
        decay_t = jnp.exp(total).T

        for hh in range(2):
            h = 2 * pair + hh
            head_lanes = lane_lo if hh == 0 else jnp.logical_not(lane_lo)
            pick = lambda a: jnp.where(head_lanes, a, 0.0).astype(BF16)
            scores = _dot_nt(pick(q_lv[-1]), k_lv[-1])
            bound = C // 2
            for lv in range(len(q_lv) - 2, -1, -1):
                scores = jnp.where(xr < bound, _dot_nt(pick(q_lv[lv]), k_lv[lv]), scores)
                bound //= 2
            scores = jnp.where(valid, scores, 0.0)
            v = v_ref[0, :, h * GLA_DV:(h + 1) * GLA_DV].astype(BF16)
            state = state_ref[h]
            o = _dot(scores.astype(BF16), v) + _dot(pick(q_in), state.astype(BF16))
            o_ref[0, :, h * GLA_DV:(h + 1) * GLA_DV] = o
            kv = _dot(k_out_t, v)
            row_lo = lax.broadcasted_iota(jnp.int32, (LANES, GLA_DV), 0) < GLA_DK
            keep = row_lo if hh == 0 else jnp.logical_not(row_lo)
            state_ref[h] = jnp.where(keep, decay_t * state + kv, 0.0)


def _gla_kernel(qf_ref, kf_ref, vf_ref, laf_ref, qb_ref, kb_ref, vb_ref, lab_ref,
                of_ref, ob_ref, sf_ref, sb_ref):
    @pl.when(pl.program_id(1) == 0)
    def _():
        sf_ref[...] = jnp.zeros(sf_ref.shape, F32)
        sb_ref[...] = jnp.zeros(sb_ref.shape, F32)

    _gla_direction(qf_ref, kf_ref, vf_ref, laf_ref, of_ref, sf_ref, True)
    _gla_direction(qb_ref, kb_ref, vb_ref, lab_ref, ob_ref, sb_ref, False)


def _gla_call(gq, gk, gv, laf, lab):
    B, S, _ = gq.shape
    C = GLA_CHUNK
    n = S // C
    fwd = lambda b, c: (b, c, 0)
    bwd = lambda b, c: (b, n - 1 - c, 0)
    qk = lambda im: pl.BlockSpec((1, C, GLA_QK_WIDTH), im)
    vv = lambda im: pl.BlockSpec((1, C, GLA_WIDTH), im)
    state = pltpu.VMEM((GLA_HEADS, LANES, GLA_DV), F32)
    return pl.pallas_call(
        _gla_kernel, grid=(B, n),
        in_specs=[qk(fwd), qk(fwd), vv(fwd), qk(fwd), qk(bwd), qk(bwd), vv(bwd), qk(bwd)],
        out_specs=[vv(fwd), vv(bwd)],
        out_shape=[jax.ShapeDtypeStruct((B, S, GLA_WIDTH), F32)] * 2,
        scratch_shapes=[state, state],
        compiler_params=_cparams(("arbitrary", "arbitrary")),
    )(gq, gk, gv, laf, gq, gk, gv, lab)


def _layer_norm(y, g, b):
    mu = jnp.mean(y, axis=-1, keepdims=True)
    d = y - mu
    var = jnp.mean(d * d, axis=-1, keepdims=True)
    return d * lax.rsqrt(var + LN_EPS) * g + b


def _silu(g):
    return g * (1.0 / (1.0 + jnp.exp(-g)))


def _mix_kernel(attn_ref, of_ref, ob_ref, go_ref, x_ref, wo_ref, gng_ref, lng_ref, lnb_ref, out_ref):
    mixed = _dot(attn_ref[0], wo_ref[0:ATTN_WIDTH, :])
    for h in range(GLA_HEADS):
        sl = slice(h * GLA_DV, (h + 1) * GLA_DV)
        o = of_ref[0, :, sl] + ob_ref[0, :, sl]
        inv = lax.rsqrt(jnp.mean(o * o, axis=-1, keepdims=True) + RMS_EPS)
        gated = (o * inv * gng_ref[...]) * _silu(go_ref[0, :, sl])
        mixed = mixed + _dot(gated.astype(BF16), wo_ref[ATTN_WIDTH + h * GLA_DV:ATTN_WIDTH + (h + 1) * GLA_DV, :])
    out_ref[0] = _layer_norm(DEEPNORM_ALPHA * x_ref[0] + mixed, lng_ref[...], lnb_ref[...])


def _mix_call(attn, o_f, o_b, go, x, w_out, gng, ln_g, ln_b):
    B, S, D = x.shape
    tm = MIX_TM
    tok = lambda b, s: (b, s, 0)
    const = lambda b, s: (0, 0)
    return pl.pallas_call(
        _mix_kernel, grid=(B, S // tm),
        in_specs=[
            pl.BlockSpec((1, tm, ATTN_WIDTH), tok),
            pl.BlockSpec((1, tm, GLA_WIDTH), tok),
            pl.BlockSpec((1, tm, GLA_WIDTH), tok),
            pl.BlockSpec((1, tm, GLA_WIDTH), tok),
            pl.BlockSpec((1, tm, D), tok),
            pl.BlockSpec((ATTN_WIDTH + GLA_WIDTH, D), const),
            pl.BlockSpec((1, GLA_DV), const),
            pl.BlockSpec((1, D), const),
            pl.BlockSpec((1, D), const),
        ],
        out_specs=pl.BlockSpec((1, tm, D), tok),
        out_shape=jax.ShapeDtypeStruct((B, S, D), F32),
        compiler_params=_cparams(("arbitrary", "arbitrary")),
    )(attn, o_f, o_b, go, x, w_out, gng, ln_g, ln_b)


def _ffn_kernel(x_ref, wg_ref, wu_ref, wd_ref, lng_ref, lnb_ref, out_ref):
    x = x_ref[0]
    xb = x.astype(BF16)
    acc = DEEPNORM_ALPHA * x
    for c in range(D_FF // FFN_CHUNK):
        sl = slice(c * FFN_CHUNK, (c + 1) * FFN_CHUNK)
        hidden = _silu(_dot(xb, wg_ref[:, sl])) * _dot(xb, wu_ref[:, sl])
        acc = acc + _dot(hidden.astype(BF16), wd_ref[sl, :])
    out_ref[0] = _layer_norm(acc, lng_ref[...], lnb_ref[...])


def _ffn_call(x, w_gate, w_up, w_down, ln_g, ln_b):
    B, S, D = x.shape
    tm = FFN_TM
    tok = lambda b, s: (b, s, 0)
    const = lambda b, s: (0, 0)
    resident = lambda shape: pl.BlockSpec(shape, const, pipeline_mode=pl.Buffered(1))
    return pl.pallas_call(
        _ffn_kernel, grid=(B, S // tm),
        in_specs=[
            pl.BlockSpec((1, tm, D), tok),
            resident((D, D_FF)), resident((D, D_FF)), resident((D_FF, D)),
            pl.BlockSpec((1, D), const),
            pl.BlockSpec((1, D), const),
        ],
        out_specs=pl.BlockSpec((1, tm, D), tok),
        out_shape=jax.ShapeDtypeStruct((B, S, D), F32),
        compiler_params=_cparams(("arbitrary", "arbitrary")),
    )(x, w_gate, w_up, w_down, ln_g, ln_b)


def _rope_tables(seq_len):
    t = jnp.arange(seq_len, dtype=jnp.int32)
    row_id = (t // GRID_W).astype(F32)
    col_id = (t % GRID_W).astype(F32)
    inv_freq = ROPE_THETA ** (-jnp.arange(0, AXIAL_DIM, 2, dtype=F32) / AXIAL_DIM)
    ang_row = row_id[:, None] * inv_freq
    ang_col = col_id[:, None] * inv_freq
    zeros = jnp.zeros_like(ang_row)
    cos = jnp.concatenate([jnp.cos(ang_row)] * 2 + [jnp.cos(ang_col)] * 2, axis=-1)
    sin_hi = jnp.concatenate([zeros, jnp.sin(ang_row), zeros, jnp.sin(ang_col)], axis=-1)
    sin_lo = jnp.concatenate([-jnp.sin(ang_row), zeros, -jnp.sin(ang_col), zeros], axis=-1)
    two = lambda a: jnp.concatenate([a, a], axis=-1)
    return two(cos), two(sin_hi), two(sin_lo)


def kernel(x, w_in, q_norm_g, k_norm_g, gate_up_fwd, gate_bias_fwd, gate_up_bwd, gate_bias_bwd, gla_norm_g,
           w_out, ln1_g, ln1_b, w_ffn_gate, w_ffn_up, w_ffn_down, ln2_g, ln2_b):
    B, S, D = x.shape
    assert D == D_MODEL and S % max(PROJ_TM, ATTN_TQ, ATTN_TK, GLA_CHUNK, MIX_TM, FFN_TM) == 0
    assert w_in.shape[0] == DEPTH
    cos, sin_hi, sin_lo = _rope_tables(S)
    for layer in range(DEPTH):
        w_pad = jnp.pad(w_in[layer], ((0, 0), (0, PROJ_PAD_WIDTH - w_in.shape[-1]))).astype(BF16)
        gup = jnp.zeros((Z_PAD, 2 * GLA_QK_WIDTH), F32)
        gup = gup.at[:GATE_RANK, :GLA_QK_WIDTH].set(gate_up_fwd[layer])
        gup = gup.at[GATE_RANK:2 * GATE_RANK, GLA_QK_WIDTH:].set(gate_up_bwd[layer]).astype(BF16)
        gbias = jnp.concatenate([gate_bias_fwd[layer], gate_bias_bwd[layer]])[None, :]
        qg = jnp.tile(q_norm_g[layer], LANES // HEAD_DIM)[None, :]
        kg = jnp.tile(k_norm_g[layer], LANES // HEAD_DIM)[None, :]

        qa, ka, va, gq, gk, gv, go, laf, lab = _proj_call(x, w_pad, gup, gbias, qg, kg, cos, sin_hi, sin_lo)
        attn = _attn_call(qa, ka, va)
        o_f, o_b = _gla_call(gq, gk, gv, laf, lab)
        x = _mix_call(attn, o_f, o_b, go, x, w_out[layer].astype(BF16), gla_norm_g[layer][None, :],
                      ln1_g[layer][None, :], ln1_b[layer][None, :])
        x = _ffn_call(x, w_ffn_gate[layer].astype(BF16), w_ffn_up[layer].astype(BF16),
                      w_ffn_down[layer].astype(BF16), ln2_g[layer][None, :], ln2_b[layer][None, :])
    return x
```

```python
import functools

import jax
import jax.numpy as jnp
import numpy as np
from jax import lax
from jax.experimental import pallas as pl
from jax.experimental.pallas import tpu as pltpu

F32 = jnp.float32
BF16 = jnp.bfloat16

D_MODEL = 1024
GRID_W = 64
N_Q_HEADS = 8
N_KV_HEADS = 2
Q_PER_KV = N_Q_HEADS // N_KV_HEADS
HEAD_DIM = 64
AXIAL_DIM = HEAD_DIM // 2
ROPE_HALF = AXIAL_DIM // 2
ROPE_THETA = 10000.0
GLA_HEADS = 4
GLA_DK = 64
GLA_DV = 128
GATE_RANK = 16
GATE_TAU = 16.0
GLA_REF_CHUNK = 16
ATTN_WIDTH = N_Q_HEADS * HEAD_DIM
KV_WIDTH = N_KV_HEADS * HEAD_DIM
GLA_QK_WIDTH = GLA_HEADS * GLA_DK
GLA_WIDTH = GLA_HEADS * GLA_DV
D_FF = 2816
DEPTH = 1
DEEPNORM_ALPHA = (2 * DEPTH) ** 0.25
LN_EPS = 1e-5
RMS_EPS = 1e-6
LOG2_E = 1.4426950408889634

LANES = 128
VMEM_LIMIT_BYTES = 56 * 1024 * 1024

PROJ_TM = 512
ATTN_TQ = 256
ATTN_TK = 512
GLA_CHUNK = 128
MIX_TM = 512
FFN_TM = 512
FFN_CHUNK = 256

OFF_AQ = 0
OFF_AK = OFF_AQ + ATTN_WIDTH
OFF_AV = OFF_AK + KV_WIDTH
OFF_GQ = OFF_AV + KV_WIDTH
OFF_GK = OFF_GQ + GLA_QK_WIDTH
OFF_GV = OFF_GK + GLA_QK_WIDTH
OFF_GO = OFF_GV + GLA_WIDTH
OFF_Z = OFF_GO + GLA_WIDTH
Z_PAD = LANES
PROJ_PAD_WIDTH = OFF_Z + Z_PAD


def _cparams(semantics):
    return pltpu.CompilerParams(dimension_semantics=semantics, vmem_limit_bytes=VMEM_LIMIT_BYTES)


def _dot(a, b):
    return jnp.dot(a, b, preferred_element_type=F32)


def _dot_nt(a, b):
    return lax.dot_general(a, b, (((1,), (1,)), ((), ())), preferred_element_type=F32)


def _dot_tn(a, b):
    return lax.dot_general(a, b, (((0,), (0,)), ((), ())), preferred_element_type=F32)


def _rope128(y, cos, sin_hi, sin_lo):
    return y * cos + pltpu.roll(y, ROPE_HALF, 1) * sin_hi + pltpu.roll(y, LANES - ROPE_HALF, 1) * sin_lo


def _head_pair_inv_rms(blk, lane_lo):
    sq = blk * blk
    ss_lo = jnp.sum(jnp.where(lane_lo, sq, 0.0), axis=-1, keepdims=True)
    ss_hi = jnp.sum(jnp.where(lane_lo, 0.0, sq), axis=-1, keepdims=True)
    inv = 1.0 / HEAD_DIM
    return jnp.where(lane_lo, lax.rsqrt(ss_lo * inv + RMS_EPS), lax.rsqrt(ss_hi * inv + RMS_EPS))


def _proj_kernel(x_ref, w_ref, gup_ref, gbias_ref, qg_ref, kg_ref, cos_ref, shi_ref, slo_ref,
                 qa_ref, ka_ref, va_ref, gq_ref, gk_ref, gv_ref, go_ref, laf_ref, lab_ref):
    x = x_ref[0].astype(BF16)
    tm = x.shape[0]
    lane = lax.broadcasted_iota(jnp.int32, (tm, LANES), 1)
    lane_lo = lane < HEAD_DIM
    cos, shi, slo = cos_ref[...], shi_ref[...], slo_ref[...]

    aq = _dot(x, w_ref[:, OFF_AQ:OFF_AQ + ATTN_WIDTH])
    for c in range(ATTN_WIDTH // LANES):
        blk = aq[:, c * LANES:(c + 1) * LANES]
        rot = _rope128(blk * qg_ref[...], cos, shi, slo)
        out_t = (rot * (_head_pair_inv_rms(blk, lane_lo) * (LOG2_E * HEAD_DIM ** -0.5))).T.astype(BF16)
        qa_ref[0, 2 * c] = out_t[:HEAD_DIM]
        qa_ref[0, 2 * c + 1] = out_t[HEAD_DIM:]

    ak = _dot(x, w_ref[:, OFF_AK:OFF_AK + KV_WIDTH])
    rot = _rope128(ak * kg_ref[...], cos, shi, slo)
    out = (rot * _head_pair_inv_rms(ak, lane_lo)).astype(BF16)
    ka_ref[0, 0] = out[:, :HEAD_DIM]
    ka_ref[0, 1] = out[:, HEAD_DIM:]

    av = _dot(x, w_ref[:, OFF_AV:OFF_AV + KV_WIDTH])
    ones_col = jnp.where(lane == HEAD_DIM, 1.0, 0.0)
    va_ref[0, 0] = jnp.where(lane_lo, av, ones_col).T.astype(BF16)
    va_ref[0, 1] = jnp.where(lane_lo, pltpu.roll(av, HEAD_DIM, 1), ones_col).T.astype(BF16)

    gq_ref[0] = _dot(x, w_ref[:, OFF_GQ:OFF_GQ + GLA_QK_WIDTH]) * (GLA_DK ** -0.5)
    gk_ref[0] = _dot(x, w_ref[:, OFF_GK:OFF_GK + GLA_QK_WIDTH])
    gv_ref[0] = _dot(x, w_ref[:, OFF_GV:OFF_GV + GLA_WIDTH])
    go_ref[0] = _dot(x, w_ref[:, OFF_GO:OFF_GO + GLA_WIDTH])

    z = _dot(x, w_ref[:, OFF_Z:OFF_Z + Z_PAD])
    g = _dot(z.astype(BF16), gup_ref[...]) + gbias_ref[...]
    log_a = (jnp.minimum(g, 0.0) - jnp.log(1.0 + jnp.exp(-jnp.abs(g)))) * (1.0 / GATE_TAU)
    laf_ref[0] = log_a[:, :GLA_QK_WIDTH]
    lab_ref[0] = log_a[:, GLA_QK_WIDTH:]


def _proj_call(x, w_pad, gup, gbias, qg, kg, cos, shi, slo):
    B, S, D = x.shape
    tm = PROJ_TM
    grid = (S // tm, B)
    tok = lambda s, b: (b, s, 0)
    head = lambda s, b: (b, 0, s, 0)
    head_t = lambda s, b: (b, 0, 0, s)
    const = lambda s, b: (0, 0)
    tab = lambda s, b: (s, 0)
    in_specs = [
        pl.BlockSpec((1, tm, D), tok),
        pl.BlockSpec((D, PROJ_PAD_WIDTH), const),
        pl.BlockSpec((Z_PAD, 2 * GLA_QK_WIDTH), const),
        pl.BlockSpec((1, 2 * GLA_QK_WIDTH), const),
        pl.BlockSpec((1, LANES), const),
        pl.BlockSpec((1, LANES), const),
        pl.BlockSpec((tm, LANES), tab),
        pl.BlockSpec((tm, LANES), tab),
        pl.BlockSpec((tm, LANES), tab),
    ]
    out_shape = [
        jax.ShapeDtypeStruct((B, N_Q_HEADS, HEAD_DIM, S), BF16),
        jax.ShapeDtypeStruct((B, N_KV_HEADS, S, HEAD_DIM), BF16),
        jax.ShapeDtypeStruct((B, N_KV_HEADS, LANES, S), BF16),
        jax.ShapeDtypeStruct((B, S, GLA_QK_WIDTH), F32),
        jax.ShapeDtypeStruct((B, S, GLA_QK_WIDTH), F32),
        jax.ShapeDtypeStruct((B, S, GLA_WIDTH), F32),
        jax.ShapeDtypeStruct((B, S, GLA_WIDTH), F32),
        jax.ShapeDtypeStruct((B, S, GLA_QK_WIDTH), F32),
        jax.ShapeDtypeStruct((B, S, GLA_QK_WIDTH), F32),
    ]
    out_specs = [
        pl.BlockSpec((1, N_Q_HEADS, HEAD_DIM, tm), head_t),
        pl.BlockSpec((1, N_KV_HEADS, tm, HEAD_DIM), head),
        pl.BlockSpec((1, N_KV_HEADS, LANES, tm), head_t),
        pl.BlockSpec((1, tm, GLA_QK_WIDTH), tok),
        pl.BlockSpec((1, tm, GLA_QK_WIDTH), tok),
        pl.BlockSpec((1, tm, GLA_WIDTH), tok),
        pl.BlockSpec((1, tm, GLA_WIDTH), tok),
        pl.BlockSpec((1, tm, GLA_QK_WIDTH), tok),
        pl.BlockSpec((1, tm, GLA_QK_WIDTH), tok),
    ]
    return pl.pallas_call(
        _proj_kernel, grid=grid, in_specs=in_specs, out_specs=out_specs, out_shape=out_shape,
        compiler_params=_cparams(("arbitrary", "arbitrary")),
    )(x, w_pad, gup, gbias, qg, kg, cos, shi, slo)


def _attn_kernel(qt_ref, k_ref, vt_ref, o_ref):
    tq = qt_ref.shape[3]
    n_keys = k_ref.shape[2]
    tk = ATTN_TK
    q_t = jnp.concatenate([qt_ref[0, r] for r in range(Q_PER_KV)], axis=1)
    m = None
    acc = None
    for j in range(n_keys // tk):
        s_t = _dot(k_ref[0, 0, j * tk:(j + 1) * tk, :], q_t)
        m_blk = jnp.max(s_t, axis=0, keepdims=True)
        m_new = m_blk if m is None else jnp.maximum(m, m_blk)
        p_t = jnp.exp2(s_t - m_new).astype(BF16)
        pv = _dot(vt_ref[0, 0, :, j * tk:(j + 1) * tk], p_t)
        acc = pv if acc is None else jnp.exp2(m - m_new) * acc + pv
        m = m_new
    o_t = acc[:HEAD_DIM] / acc[HEAD_DIM:HEAD_DIM + 1]
    for r in range(Q_PER_KV):
        o_ref[0, :, r * HEAD_DIM:(r + 1) * HEAD_DIM] = o_t[:, r * tq:(r + 1) * tq].T.astype(o_ref.dtype)


def _attn_call(qa_t, ka, va_t):
    B, _, S, _ = ka.shape
    tq = ATTN_TQ
    grid = (B, N_KV_HEADS, S // tq)
    return pl.pallas_call(
        _attn_kernel, grid=grid,
        in_specs=[
            pl.BlockSpec((1, Q_PER_KV, HEAD_DIM, tq), lambda b, g, qi: (b, g, 0, qi)),
            pl.BlockSpec((1, 1, S, HEAD_DIM), lambda b, g, qi: (b, g, 0, 0)),
            pl.BlockSpec((1, 1, LANES, S), lambda b, g, qi: (b, g, 0, 0)),
        ],
        out_specs=pl.BlockSpec((1, tq, Q_PER_KV * HEAD_DIM), lambda b, g, qi: (b, qi, g)),
        out_shape=jax.ShapeDtypeStruct((B, S, ATTN_WIDTH), BF16),
        compiler_params=_cparams(("arbitrary", "arbitrary", "arbitrary")),
    )(qa_t, ka, va_t)


def _block_row(a, blk, row):
    n, w = a.shape
    a3 = a.reshape(n // blk, blk, w)
    return jnp.broadcast_to(a3[:, row:row + 1, :], a3.shape).reshape(n, w)


def _gla_direction(q_ref, k_ref, v_ref, la_ref, o_ref, state_ref, forward):
    C = q_ref.shape[1]
    row = lax.broadcasted_iota(jnp.int32, (C, C), 0)
    col = lax.broadcasted_iota(jnp.int32, (C, C), 1)
    tri = (col <= row) if forward else (col >= row)
    la = la_ref[0]
    cum_all = jnp.dot(tri.astype(F32), la, preferred_element_type=F32, precision=lax.Precision.HIGHEST)
    excl_all = cum_all - la
    xr = row ^ col
    valid = (col <= row) if forward else (col > row)
    lane = lax.broadcasted_iota(jnp.int32, (C, LANES), 1)
    lane_lo = lane < GLA_DK
    edge = C - 1 if forward else 0

    for pair in range(GLA_HEADS // 2):
        sl = slice(pair * LANES, (pair + 1) * LANES)
        q, k = q_ref[0, :, sl], k_ref[0, :, sl]
        cum, excl = cum_all[:, sl], excl_all[:, sl]
        total = cum[edge:edge + 1, :]

        base_row = 0 if forward else GLA_REF_CHUNK - 1
        loc = cum - _block_row(excl, GLA_REF_CHUNK, base_row)
        q_lv = [q * jnp.exp(loc)]
        k_lv = [(k * jnp.exp(-loc)).astype(BF16)]
        blk = 2 * GLA_REF_CHUNK
        while blk <= C:
            mid = _block_row(excl if forward else cum, blk, blk // 2)
            q_lv.append(q * jnp.exp(jnp.minimum(cum - mid, 0.0)))
            k_lv.append((k * jnp.exp(jnp.minimum(mid - cum, 0.0))).astype(BF16))
            blk *= 2
        q_in = q * jnp.exp(cum)
        k_out = k * jnp.exp(total - cum)
        k_out_t = k_out.T.astype(BF16)
        decay_t = jnp.exp(total).T

        for hh in range(2):
            h = 2 * pair + hh
            head_lanes = lane_lo if hh == 0 else jnp.logical_not(lane_lo)
            pick = lambda a: jnp.where(head_lanes, a, 0.0).astype(BF16)
            scores = _dot_nt(pick(q_lv[-1]), k_lv[-1])
            bound = C // 2
            for lv in range(len(q_lv) - 2, -1, -1):
                scores = jnp.where(xr < bound, _dot_nt(pick(q_lv[lv]), k_lv[lv]), scores)
                bound //= 2
            scores = jnp.where(valid, scores, 0.0)
            v = v_ref[0, :, h * GLA_DV:(h + 1) * GLA_DV].astype(BF16)
            state = state_ref[h]
            o = _dot(scores.astype(BF16), v) + _dot(pick(q_in), state.astype(BF16))
            o_ref[0, :, h * GLA_DV:(h + 1) * GLA_DV] = o
            kv = _dot(k_out_t, v)
            row_lo = lax.broadcasted_iota(jnp.int32, (LANES, GLA_DV), 0) < GLA_DK
            keep = row_lo if hh == 0 else jnp.logical_not(row_lo)
            state_ref[h] = jnp.where(keep, decay_t * state + kv, 0.0)


def _gla_kernel(qf_ref, kf_ref, vf_ref, laf_ref, qb_ref, kb_ref, vb_ref, lab_ref,
                of_ref, ob_ref, sf_ref, sb_ref):
    @pl.when(pl.program_id(1) == 0)
    def _():
        sf_ref[...] = jnp.zeros(sf_ref.shape, F32)
        sb_ref[...] = jnp.zeros(sb_ref.shape, F32)

    _gla_direction(qf_ref, kf_ref, vf_ref, laf_ref, of_ref, sf_ref, True)
    _gla_direction(qb_ref, kb_ref, vb_ref, lab_ref, ob_ref, sb_ref, False)


def _gla_call(gq, gk, gv, laf, lab):
    B, S, _ = gq.shape
    C = GLA_CHUNK
    n = S // C
    fwd = lambda b, c: (b, c, 0)
    bwd = lambda b, c: (b, n - 1 - c, 0)
    qk = lambda im: pl.BlockSpec((1, C, GLA_QK_WIDTH), im)
    vv = lambda im: pl.BlockSpec((1, C, GLA_WIDTH), im)
    state = pltpu.VMEM((GLA_HEADS, LANES, GLA_DV), F32)
    return pl.pallas_call(
        _gla_kernel, grid=(B, n),
        in_specs=[qk(fwd), qk(fwd), vv(fwd), qk(fwd), qk(bwd), qk(bwd), vv(bwd), qk(bwd)],
        out_specs=[vv(fwd), vv(bwd)],
        out_shape=[jax.ShapeDtypeStruct((B, S, GLA_WIDTH), F32)] * 2,
        scratch_shapes=[state, state],
        compiler_params=_cparams(("arbitrary", "arbitrary")),
    )(gq, gk, gv, laf, gq, gk, gv, lab)


def _layer_norm(y, g, b):
    mu = jnp.mean(y, axis=-1, keepdims=True)
    d = y - mu
    var = jnp.mean(d * d, axis=-1, keepdims=True)
    return d * lax.rsqrt(var + LN_EPS) * g + b


def _silu(g):
    return g * (1.0 / (1.0 + jnp.exp(-g)))


def _mix_kernel(attn_ref, of_ref, ob_ref, go_ref, x_ref, wo_ref, gng_ref, lng_ref, lnb_ref, out_ref):
    mixed = _dot(attn_ref[0], wo_ref[0:ATTN_WIDTH, :])
    for h in range(GLA_HEADS):
        sl = slice(h * GLA_DV, (h + 1) * GLA_DV)
        o = of_ref[0, :, sl] + ob_ref[0, :, sl]
        inv = lax.rsqrt(jnp.mean(o * o, axis=-1, keepdims=True) + RMS_EPS)
        gated = (o * inv * gng_ref[...]) * _silu(go_ref[0, :, sl])
        mixed = mixed + _dot(gated.astype(BF16), wo_ref[ATTN_WIDTH + h * GLA_DV:ATTN_WIDTH + (h + 1) * GLA_DV, :])
    out_ref[0] = _layer_norm(DEEPNORM_ALPHA * x_ref[0] + mixed, lng_ref[...], lnb_ref[...])


def _mix_call(attn, o_f, o_b, go, x, w_out, gng, ln_g, ln_b):
    B, S, D = x.shape
    tm = MIX_TM
    tok = lambda b, s: (b, s, 0)
    const = lambda b, s: (0, 0)
    return pl.pallas_call(
        _mix_kernel, grid=(B, S // tm),
        in_specs=[
            pl.BlockSpec((1, tm, ATTN_WIDTH), tok),
            pl.BlockSpec((1, tm, GLA_WIDTH), tok),
            pl.BlockSpec((1, tm, GLA_WIDTH), tok),
            pl.BlockSpec((1, tm, GLA_WIDTH), tok),
            pl.BlockSpec((1, tm, D), tok),
            pl.BlockSpec((ATTN_WIDTH + GLA_WIDTH, D), const),
            pl.BlockSpec((1, GLA_DV), const),
            pl.BlockSpec((1, D), const),
            pl.BlockSpec((1, D), const),
        ],
        out_specs=pl.BlockSpec((1, tm, D), tok),
        out_shape=jax.ShapeDtypeStruct((B, S, D), F32),
        compiler_params=_cparams(("arbitrary", "arbitrary")),
    )(attn, o_f, o_b, go, x, w_out, gng, ln_g, ln_b)


def _ffn_kernel(x_ref, wg_ref, wu_ref, wd_ref, lng_ref, lnb_ref, out_ref):
    x = x_ref[0]
    xb = x.astype(BF16)
    acc = DEEPNORM_ALPHA * x
    for c in range(D_FF // FFN_CHUNK):
        sl = slice(c * FFN_CHUNK, (c + 1) * FFN_CHUNK)
        hidden = _silu(_dot(xb, wg_ref[:, sl])) * _dot(xb, wu_ref[:, sl])
        acc = acc + _dot(hidden.astype(BF16), wd_ref[sl, :])
    out_ref[0] = _layer_norm(acc, lng_ref[...], lnb_ref[...])


def _ffn_call(x, w_gate, w_up, w_down, ln_g, ln_b):
    B, S, D = x.shape
    tm = FFN_TM
    tok = lambda b, s: (b, s, 0)
    const = lambda b, s: (0, 0)
    resident = lambda shape: pl.BlockSpec(shape, const, pipeline_mode=pl.Buffered(1))
    return pl.pallas_call(
        _ffn_kernel, grid=(B, S // tm),
        in_specs=[
            pl.BlockSpec((1, tm, D), tok),
            resident((D, D_FF)), resident((D, D_FF)), resident((D_FF, D)),
            pl.BlockSpec((1, D), const),
            pl.BlockSpec((1, D), const),
        ],
        out_specs=pl.BlockSpec((1, tm, D), tok),
        out_shape=jax.ShapeDtypeStruct((B, S, D), F32),
        compiler_params=_cparams(("arbitrary", "arbitrary")),
    )(x, w_gate, w_up, w_down, ln_g, ln_b)


def _rope_tables(seq_len):
    t = jnp.arange(seq_len, dtype=jnp.int32)
    row_id = (t // GRID_W).astype(F32)
    col_id = (t % GRID_W).astype(F32)
    inv_freq = ROPE_THETA ** (-jnp.arange(0, AXIAL_DIM, 2, dtype=F32) / AXIAL_DIM)
    ang_row = row_id[:, None] * inv_freq
    ang_col = col_id[:, None] * inv_freq
    zeros = jnp.zeros_like(ang_row)
    cos = jnp.concatenate([jnp.cos(ang_row)] * 2 + [jnp.cos(ang_col)] * 2, axis=-1)
    sin_hi = jnp.concatenate([zeros, jnp.sin(ang_row), zeros, jnp.sin(ang_col)], axis=-1)
    sin_lo = jnp.concatenate([-jnp.sin(ang_row), zeros, -jnp.sin(ang_col), zeros], axis=-1)
    two = lambda a: jnp.concatenate([a, a], axis=-1)
    return two(cos), two(sin_hi), two(sin_lo)


def kernel(x, w_in, q_norm_g, k_norm_g, gate_up_fwd, gate_bias_fwd, gate_up_bwd, gate_bias_bwd, gla_norm_g,
           w_out, ln1_g, ln1_b, w_ffn_gate, w_ffn_up, w_ffn_down, ln2_g, ln2_b):
    B, S, D = x.shape
    assert D == D_MODEL and S % max(PROJ_TM, ATTN_TQ, ATTN_TK, GLA_CHUNK, MIX_TM, FFN_TM) == 0
    assert w_in.shape[0] == DEPTH
    cos, sin_hi, sin_lo = _rope_tables(S)
    for layer in range(DEPTH):
        w_pad = jnp.pad(w_in[layer], ((0, 0), (0, PROJ_PAD_WIDTH - w_in.shape[-1]))).astype(BF16)
        gup = jnp.zeros((Z_PAD, 2 * GLA_QK_WIDTH), F32)
        gup = gup.at[:GATE_RANK, :GLA_QK_WIDTH].set(gate_up_fwd[layer])
        gup = gup.at[GATE_RANK:2 * GATE_RANK, GLA_QK_WIDTH:].set(gate_up_bwd[layer]).astype(BF16)
        gbias = jnp.concatenate([gate_bias_fwd[layer], gate_bias_bwd[layer]])[None, :]
        qg = jnp.tile(q_norm_g[layer], LANES // HEAD_DIM)[None, :]
        kg = jnp.tile(k_norm_g[layer], LANES // HEAD_DIM)[None, :]

        qa, ka, va, gq, gk, gv, go, laf, lab = _proj_call(x, w_pad, gup, gbias, qg, kg, cos, sin_hi, sin_lo)
        attn = _attn_call(qa, ka, va)
        o_f, o_b = _gla_call(gq, gk, gv, laf, lab)
        x = _mix_call(attn, o_f, o_b, go, x, w_out[layer].astype(BF16), gla_norm_g[layer][None, :],
                      ln1_g[layer][None, :], ln1_b[layer][None, :])
        x = _ffn_call(x, w_ffn_gate[layer].astype(BF16), w_ffn_up[layer].astype(BF16),
                      w_ffn_down[layer].astype(BF16), ln2_g[layer][None, :], ln2_b[layer][None, :])
    return x
```

```python
import functools

import jax
import jax.numpy as jnp
import numpy as np
from jax import lax
from jax.experimental import pallas as pl
from jax.experimental.pallas import tpu as pltpu

F32 = jnp.float32
BF16 = jnp.bfloat16

D_MODEL = 1024
GRID_W = 64
N_Q_HEADS = 8
N_KV_HEADS = 2
Q_PER_KV = N_Q_HEADS // N_KV_HEADS
HEAD_DIM = 64
AXIAL_DIM = HEAD_DIM // 2
ROPE_HALF = AXIAL_DIM // 2
ROPE_THETA = 10000.0
GLA_HEADS = 4
GLA_DK = 64
GLA_DV = 128
GATE_RANK = 16
GATE_TAU = 16.0
GLA_REF_CHUNK = 16
ATTN_WIDTH = N_Q_HEADS * HEAD_DIM
KV_WIDTH = N_KV_HEADS * HEAD_DIM
GLA_QK_WIDTH = GLA_HEADS * GLA_DK
GLA_WIDTH = GLA_HEADS * GLA_DV
D_FF = 2816
DEPTH = 1
DEEPNORM_ALPHA = (2 * DEPTH) ** 0.25
LN_EPS = 1e-5
RMS_EPS = 1e-6
LOG2_E = 1.4426950408889634

LANES = 128
VMEM_LIMIT_BYTES = 56 * 1024 * 1024

PROJ_TM = 512
ATTN_TQ = 256
ATTN_TK = 512
GLA_CHUNK = 128
GLA_BATCH = 2
MIX_TM = 512
FFN_TM = 512
FFN_CHUNK = 256

OFF_AQ = 0
OFF_AK = OFF_AQ + ATTN_WIDTH
OFF_AV = OFF_AK + KV_WIDTH
OFF_GQ = OFF_AV + KV_WIDTH
OFF_GK = OFF_GQ + GLA_QK_WIDTH
OFF_GV = OFF_GK + GLA_QK_WIDTH
OFF_GO = OFF_GV + GLA_WIDTH
OFF_Z = OFF_GO + GLA_WIDTH
Z_PAD = LANES
PROJ_PAD_WIDTH = OFF_Z + Z_PAD


def _cparams(semantics):
    return pltpu.CompilerParams(dimension_semantics=semantics, vmem_limit_bytes=VMEM_LIMIT_BYTES)


def _dot(a, b):
    return jnp.dot(a, b, preferred_element_type=F32)


def _dot_nt(a, b):
    return lax.dot_general(a, b, (((1,), (1,)), ((), ())), preferred_element_type=F32)


def _dot_tn(a, b):
    return lax.dot_general(a, b, (((0,), (0,)), ((), ())), preferred_element_type=F32)


def _rope128(y, cos, sin_hi, sin_lo):
    return y * cos + pltpu.roll(y, ROPE_HALF, 1) * sin_hi + pltpu.roll(y, LANES - ROPE_HALF, 1) * sin_lo


def _head_pair_inv_rms(blk, lane_lo):
    sq = blk * blk
    ss_lo = jnp.sum(jnp.where(lane_lo, sq, 0.0), axis=-1, keepdims=True)
    ss_hi = jnp.sum(jnp.where(lane_lo, 0.0, sq), axis=-1, keepdims=True)
    inv = 1.0 / HEAD_DIM
    return jnp.where(lane_lo, lax.rsqrt(ss_lo * inv + RMS_EPS), lax.rsqrt(ss_hi * inv + RMS_EPS))


def _proj_kernel(x_ref, w_ref, gup_ref, gbias_ref, qg_ref, kg_ref, cos_ref, shi_ref, slo_ref,
                 qa_ref, ka_ref, va_ref, gq_ref, gk_ref, gv_ref, go_ref, laf_ref, lab_ref):
    x = x_ref[0].astype(BF16)
    tm = x.shape[0]
    lane = lax.broadcasted_iota(jnp.int32, (tm, LANES), 1)
    lane_lo = lane < HEAD_DIM
    cos, shi, slo = cos_ref[...], shi_ref[...], slo_ref[...]

    aq = _dot(x, w_ref[:, OFF_AQ:OFF_AQ + ATTN_WIDTH])
    for c in range(ATTN_WIDTH // LANES):
        blk = aq[:, c * LANES:(c + 1) * LANES]
        rot = _rope128(blk * qg_ref[...], cos, shi, slo)
        out_t = (rot * (_head_pair_inv_rms(blk, lane_lo) * (LOG2_E * HEAD_DIM ** -0.5))).T.astype(BF16)
        qa_ref[0, 2 * c] = out_t[:HEAD_DIM]
        qa_ref[0, 2 * c + 1] = out_t[HEAD_DIM:]

    ak = _dot(x, w_ref[:, OFF_AK:OFF_AK + KV_WIDTH])
    rot = _rope128(ak * kg_ref[...], cos, shi, slo)
    out = (rot * _head_pair_inv_rms(ak, lane_lo)).astype(BF16)
    ka_ref[0, 0] = out[:, :HEAD_DIM]
    ka_ref[0, 1] = out[:, HEAD_DIM:]

    av = _dot(x, w_ref[:, OFF_AV:OFF_AV + KV_WIDTH])
    ones_col = jnp.where(lane == HEAD_DIM, 1.0, 0.0)
    va_ref[0, 0] = jnp.where(lane_lo, av, ones_col).T.astype(BF16)
    va_ref[0, 1] = jnp.where(lane_lo, pltpu.roll(av, HEAD_DIM, 1), ones_col).T.astype(BF16)

    gq_ref[0] = (_dot(x, w_ref[:, OFF_GQ:OFF_GQ + GLA_QK_WIDTH]) * (GLA_DK ** -0.5)).astype(BF16)
    gk_ref[0] = _dot(x, w_ref[:, OFF_GK:OFF_GK + GLA_QK_WIDTH]).astype(BF16)
    gv_ref[0] = _dot(x, w_ref[:, OFF_GV:OFF_GV + GLA_WIDTH]).astype(BF16)
    go_ref[0] = _dot(x, w_ref[:, OFF_GO:OFF_GO + GLA_WIDTH]).astype(BF16)

    z = _dot(x, w_ref[:, OFF_Z:OFF_Z + Z_PAD])
    g = _dot(z.astype(BF16), gup_ref[...]) + gbias_ref[...]
    log_a = (jnp.minimum(g, 0.0) - jnp.log(1.0 + jnp.exp(-jnp.abs(g)))) * (1.0 / GATE_TAU)
    laf_ref[0] = log_a[:, :GLA_QK_WIDTH]
    lab_ref[0] = log_a[:, GLA_QK_WIDTH:]


def _proj_call(x, w_pad, gup, gbias, qg, kg, cos, shi, slo):
    B, S, D = x.shape
    tm = PROJ_TM
    grid = (S // tm, B)
    tok = lambda s, b: (b, s, 0)
    head = lambda s, b: (b, 0, s, 0)
    head_t = lambda s, b: (b, 0, 0, s)
    const = lambda s, b: (0, 0)
    tab = lambda s, b: (s, 0)
    in_specs = [
        pl.BlockSpec((1, tm, D), tok),
        pl.BlockSpec((D, PROJ_PAD_WIDTH), const),
        pl.BlockSpec((Z_PAD, 2 * GLA_QK_WIDTH), const),
        pl.BlockSpec((1, 2 * GLA_QK_WIDTH), const),
        pl.BlockSpec((1, LANES), const),
        pl.BlockSpec((1, LANES), const),
        pl.BlockSpec((tm, LANES), tab),
        pl.BlockSpec((tm, LANES), tab),
        pl.BlockSpec((tm, LANES), tab),
    ]
    out_shape = [
        jax.ShapeDtypeStruct((B, N_Q_HEADS, HEAD_DIM, S), BF16),
        jax.ShapeDtypeStruct((B, N_KV_HEADS, S, HEAD_DIM), BF16),
        jax.ShapeDtypeStruct((B, N_KV_HEADS, LANES, S), BF16),
        jax.ShapeDtypeStruct((B, S, GLA_QK_WIDTH), BF16),
        jax.ShapeDtypeStruct((B, S, GLA_QK_WIDTH), BF16),
        jax.ShapeDtypeStruct((B, S, GLA_WIDTH), BF16),
        jax.ShapeDtypeStruct((B, S, GLA_WIDTH), BF16),
        jax.ShapeDtypeStruct((B, S, GLA_QK_WIDTH), F32),
        jax.ShapeDtypeStruct((B, S, GLA_QK_WIDTH), F32),
    ]
    out_specs = [
        pl.BlockSpec((1, N_Q_HEADS, HEAD_DIM, tm), head_t),
        pl.BlockSpec((1, N_KV_HEADS, tm, HEAD_DIM), head),
        pl.BlockSpec((1, N_KV_HEADS, LANES, tm), head_t),
        pl.BlockSpec((1, tm, GLA_QK_WIDTH), tok),
        pl.BlockSpec((1, tm, GLA_QK_WIDTH), tok),
        pl.BlockSpec((1, tm, GLA_WIDTH), tok),
        pl.BlockSpec((1, tm, GLA_WIDTH), tok),
        pl.BlockSpec((1, tm, GLA_QK_WIDTH), tok),
        pl.BlockSpec((1, tm, GLA_QK_WIDTH), tok),
    ]
    return pl.pallas_call(
        _proj_kernel, grid=grid, in_specs=in_specs, out_specs=out_specs, out_shape=out_shape,
        compiler_params=_cparams(("arbitrary", "arbitrary")),
    )(x, w_pad, gup, gbias, qg, kg, cos, shi, slo)


def _attn_kernel(qt_ref, k_ref, vt_ref, o_ref):
    tq = qt_ref.shape[3]
    n_keys = k_ref.shape[2]
    tk = ATTN_TK
    q_t = jnp.concatenate([qt_ref[0, r] for r in range(Q_PER_KV)], axis=1)
    m = None
    acc = None
    for j in range(n_keys // tk):
        s_t = _dot(k_ref[0, 0, j * tk:(j + 1) * tk, :], q_t)
        m_blk = jnp.max(s_t, axis=0, keepdims=True)
        m_new = m_blk if m is None else jnp.maximum(m, m_blk)
        p_t = jnp.exp2(s_t - m_new).astype(BF16)
        pv = _dot(vt_ref[0, 0, :, j * tk:(j + 1) * tk], p_t)
        acc = pv if acc is None else jnp.exp2(m - m_new) * acc + pv
        m = m_new
    o_t = acc[:HEAD_DIM] / acc[HEAD_DIM:HEAD_DIM + 1]
    for r in range(Q_PER_KV):
        o_ref[0, :, r * HEAD_DIM:(r + 1) * HEAD_DIM] = o_t[:, r * tq:(r + 1) * tq].T.astype(o_ref.dtype)


def _attn_call(qa_t, ka, va_t):
    B, _, S, _ = ka.shape
    tq = ATTN_TQ
    grid = (B, N_KV_HEADS, S // tq)
    return pl.pallas_call(
        _attn_kernel, grid=grid,
        in_specs=[
            pl.BlockSpec((1, Q_PER_KV, HEAD_DIM, tq), lambda b, g, qi: (b, g, 0, qi)),
            pl.BlockSpec((1, 1, S, HEAD_DIM), lambda b, g, qi: (b, g, 0, 0)),
            pl.BlockSpec((1, 1, LANES, S), lambda b, g, qi: (b, g, 0, 0)),
        ],
        out_specs=pl.BlockSpec((1, tq, Q_PER_KV * HEAD_DIM), lambda b, g, qi: (b, qi, g)),
        out_shape=jax.ShapeDtypeStruct((B, S, ATTN_WIDTH), BF16),
        compiler_params=_cparams(("arbitrary", "arbitrary", "arbitrary")),
    )(qa_t, ka, va_t)


def _block_row(a, blk, row):
    n, w = a.shape
    a3 = a.reshape(n // blk, blk, w)
    return jnp.broadcast_to(a3[:, row:row + 1, :], a3.shape).reshape(n, w)


def _split3_bf16(a):
    hi = a.astype(BF16)
    r1 = a - hi.astype(F32)
    mid = r1.astype(BF16)
    lo = (r1 - mid.astype(F32)).astype(BF16)
    return hi, mid, lo


def _gla_direction(bi, q_ref, k_ref, v_ref, la_ref, o_ref, state_ref, forward):
    C = q_ref.shape[1]
    row = lax.broadcasted_iota(jnp.int32, (C, C), 0)
    col = lax.broadcasted_iota(jnp.int32, (C, C), 1)
    tri = ((col <= row) if forward else (col >= row)).astype(BF16)
    la = la_ref[bi]
    cum_all = sum(_dot(tri, part) for part in _split3_bf16(la))
    excl_all = cum_all - la
    xr = row ^ col
    valid = (col <= row) if forward else (col > row)
    lane = lax.broadcasted_iota(jnp.int32, (C, LANES), 1)
    lane_lo = lane < GLA_DK
    edge = C - 1 if forward else 0
    srow = lax.broadcasted_iota(jnp.int32, (LANES, 2 * GLA_DV), 0)
    scol = lax.broadcasted_iota(jnp.int32, (LANES, 2 * GLA_DV), 1)
    on_diag = (srow < GLA_DK) == (scol < GLA_DV)

    for pair in range(GLA_HEADS // 2):
        sl = slice(pair * LANES, (pair + 1) * LANES)
        q, k = q_ref[bi, :, sl].astype(F32), k_ref[bi, :, sl].astype(F32)
        cum, excl = cum_all[:, sl], excl_all[:, sl]
        total = cum[edge:edge + 1, :]

        base_row = 0 if forward else GLA_REF_CHUNK - 1
        loc = cum - _block_row(excl, GLA_REF_CHUNK, base_row)
        q_lv = [q * jnp.exp(loc)]
        k_lv = [(k * jnp.exp(-loc)).astype(BF16)]
        blk = 2 * GLA_REF_CHUNK
        while blk <= C:
            mid = _block_row(excl if forward else cum, blk, blk // 2)
            q_lv.append(q * jnp.exp(jnp.minimum(cum - mid, 0.0)))
            k_lv.append((k * jnp.exp(jnp.minimum(mid - cum, 0.0))).astype(BF16))
            blk *= 2
        q_in = (q * jnp.exp(cum)).astype(BF16)
        k_out = k * jnp.exp(total - cum)
        k_out_t = k_out.T.astype(BF16)
        decay_t = jnp.exp(total).T

        def stack_heads(a):
            return jnp.concatenate([jnp.where(lane_lo, a, 0.0), jnp.where(lane_lo, 0.0, a)], axis=0).astype(BF16)

        scores2 = _dot_nt(stack_heads(q_lv[-1]), k_lv[-1])
        bound = C // 2
        xr2 = jnp.concatenate([xr, xr], axis=0)
        for lv in range(len(q_lv) - 2, -1, -1):
            scores2 = jnp.where(xr2 < bound, _dot_nt(stack_heads(q_lv[lv]), k_lv[lv]), scores2)
            bound //= 2
        scores2 = jnp.where(jnp.concatenate([valid, valid], axis=0), scores2, 0.0).astype(BF16)

        v2 = v_ref[bi, :, 2 * pair * GLA_DV:2 * (pair + 1) * GLA_DV]
        state = state_ref[bi, pair]
        inter = _dot(q_in, state.astype(BF16))
        for hh in range(2):
            h = 2 * pair + hh
            cols = slice(hh * GLA_DV, (hh + 1) * GLA_DV)
            o = _dot(scores2[hh * C:(hh + 1) * C], v2[:, cols]) + inter[:, cols]
            o_ref[bi, :, h * GLA_DV:(h + 1) * GLA_DV] = o.astype(o_ref.dtype)
        state_ref[bi, pair] = jnp.where(on_diag, decay_t * state + _dot(k_out_t, v2), 0.0)


def _gla_kernel(qf_ref, kf_ref, vf_ref, laf_ref, qb_ref, kb_ref, vb_ref, lab_ref,
                of_ref, ob_ref, sf_ref, sb_ref):
    @pl.when(pl.program_id(1) == 0)
    def _():
        sf_ref[...] = jnp.zeros(sf_ref.shape, F32)
        sb_ref[...] = jnp.zeros(sb_ref.shape, F32)

    for bi in range(GLA_BATCH):
        _gla_direction(bi, qf_ref, kf_ref, vf_ref, laf_ref, of_ref, sf_ref, True)
        _gla_direction(bi, qb_ref, kb_ref, vb_ref, lab_ref, ob_ref, sb_ref, False)


def _gla_call(gq, gk, gv, laf, lab):
    B, S, _ = gq.shape
    C, nb = GLA_CHUNK, GLA_BATCH
    n = S // C
    fwd = lambda b, c: (b, c, 0)
    bwd = lambda b, c: (b, n - 1 - c, 0)
    qk = lambda im: pl.BlockSpec((nb, C, GLA_QK_WIDTH), im)
    vv = lambda im: pl.BlockSpec((nb, C, GLA_WIDTH), im)
    state = pltpu.VMEM((nb, GLA_HEADS // 2, LANES, 2 * GLA_DV), F32)
    return pl.pallas_call(
        _gla_kernel, grid=(B // nb, n),
        in_specs=[qk(fwd), qk(fwd), vv(fwd), qk(fwd), qk(bwd), qk(bwd), vv(bwd), qk(bwd)],
        out_specs=[vv(fwd), vv(bwd)],
        out_shape=[jax.ShapeDtypeStruct((B, S, GLA_WIDTH), BF16)] * 2,
        scratch_shapes=[state, state],
        compiler_params=_cparams(("arbitrary", "arbitrary")),
    )(gq, gk, gv, laf, gq, gk, gv, lab)


def _layer_norm(y, g, b):
    mu = jnp.mean(y, axis=-1, keepdims=True)
    d = y - mu
    var = jnp.mean(d * d, axis=-1, keepdims=True)
    return d * lax.rsqrt(var + LN_EPS) * g + b


def _silu(g):
    return g * (1.0 / (1.0 + jnp.exp(-g)))


def _mix_kernel(attn_ref, of_ref, ob_ref, go_ref, x_ref, wo_ref, gng_ref, lng_ref, lnb_ref, out_ref):
    mixed = _dot(attn_ref[0], wo_ref[0:ATTN_WIDTH, :])
    for h in range(GLA_HEADS):
        sl = slice(h * GLA_DV, (h + 1) * GLA_DV)
        o = of_ref[0, :, sl].astype(F32) + ob_ref[0, :, sl].astype(F32)
        inv = lax.rsqrt(jnp.mean(o * o, axis=-1, keepdims=True) + RMS_EPS)
        gated = (o * inv * gng_ref[...]) * _silu(go_ref[0, :, sl].astype(F32))
        mixed = mixed + _dot(gated.astype(BF16), wo_ref[ATTN_WIDTH + h * GLA_DV:ATTN_WIDTH + (h + 1) * GLA_DV, :])
    out_ref[0] = _layer_norm(DEEPNORM_ALPHA * x_ref[0] + mixed, lng_ref[...], lnb_ref[...])


def _mix_call(attn, o_f, o_b, go, x, w_out, gng, ln_g, ln_b):
    B, S, D = x.shape
    tm = MIX_TM
    tok = lambda b, s: (b, s, 0)
    const = lambda b, s: (0, 0)
    return pl.pallas_call(
        _mix_kernel, grid=(B, S // tm),
        in_specs=[
            pl.BlockSpec((1, tm, ATTN_WIDTH), tok),
            pl.BlockSpec((1, tm, GLA_WIDTH), tok),
            pl.BlockSpec((1, tm, GLA_WIDTH), tok),
            pl.BlockSpec((1, tm, GLA_WIDTH), tok),
            pl.BlockSpec((1, tm, D), tok),
            pl.BlockSpec((ATTN_WIDTH + GLA_WIDTH, D), const),
            pl.BlockSpec((1, GLA_DV), const),
            pl.BlockSpec((1, D), const),
            pl.BlockSpec((1, D), const),
        ],
        out_specs=pl.BlockSpec((1, tm, D), tok),
        out_shape=jax.ShapeDtypeStruct((B, S, D), F32),
        compiler_params=_cparams(("arbitrary", "arbitrary")),
    )(attn, o_f, o_b, go, x, w_out, gng, ln_g, ln_b)


def _ffn_kernel(x_ref, wg_ref, wu_ref, wd_ref, lng_ref, lnb_ref, out_ref):
    x = x_ref[0]
    xb = x.astype(BF16)
    acc = DEEPNORM_ALPHA * x
    for c in range(D_FF // FFN_CHUNK):
        sl = slice(c * FFN_CHUNK, (c + 1) * FFN_CHUNK)
        hidden = _silu(_dot(xb, wg_ref[:, sl])) * _dot(xb, wu_ref[:, sl])
        acc = acc + _dot(hidden.astype(BF16), wd_ref[sl, :])
    out_ref[0] = _layer_norm(acc, lng_ref[...], lnb_ref[...])


def _ffn_call(x, w_gate, w_up, w_down, ln_g, ln_b):
    B, S, D = x.shape
    tm = FFN_TM
    tok = lambda b, s: (b, s, 0)
    const = lambda b, s: (0, 0)
    resident = lambda shape: pl.BlockSpec(shape, const, pipeline_mode=pl.Buffered(1))
    return pl.pallas_call(
        _ffn_kernel, grid=(B, S // tm),
        in_specs=[
            pl.BlockSpec((1, tm, D), tok),
            resident((D, D_FF)), resident((D, D_FF)), resident((D_FF, D)),
            pl.BlockSpec((1, D), const),
            pl.BlockSpec((1, D), const),
        ],
        out_specs=pl.BlockSpec((1, tm, D), tok),
        out_shape=jax.ShapeDtypeStruct((B, S, D), F32),
        compiler_params=_cparams(("arbitrary", "arbitrary")),
    )(x, w_gate, w_up, w_down, ln_g, ln_b)


def _rope_tables(seq_len):
    t = jnp.arange(seq_len, dtype=jnp.int32)
    row_id = (t // GRID_W).astype(F32)
    col_id = (t % GRID_W).astype(F32)
    inv_freq = ROPE_THETA ** (-jnp.arange(0, AXIAL_DIM, 2, dtype=F32) / AXIAL_DIM)
    ang_row = row_id[:, None] * inv_freq
    ang_col = col_id[:, None] * inv_freq
    zeros = jnp.zeros_like(ang_row)
    cos = jnp.concatenate([jnp.cos(ang_row)] * 2 + [jnp.cos(ang_col)] * 2, axis=-1)
    sin_hi = jnp.concatenate([zeros, jnp.sin(ang_row), zeros, jnp.sin(ang_col)], axis=-1)
    sin_lo = jnp.concatenate([-jnp.sin(ang_row), zeros, -jnp.sin(ang_col), zeros], axis=-1)
    two = lambda a: jnp.concatenate([a, a], axis=-1)
    return two(cos), two(sin_hi), two(sin_lo)


def kernel(x, w_in, q_norm_g, k_norm_g, gate_up_fwd, gate_bias_fwd, gate_up_bwd, gate_bias_bwd, gla_norm_g,
           w_out, ln1_g, ln1_b, w_ffn_gate, w_ffn_up, w_ffn_down, ln2_g, ln2_b):
    B, S, D = x.shape
    assert D == D_MODEL and S % max(PROJ_TM, ATTN_TQ, ATTN_TK, GLA_CHUNK, MIX_TM, FFN_TM) == 0
    assert w_in.shape[0] == DEPTH
    cos, sin_hi, sin_lo = _rope_tables(S)
    for layer in range(DEPTH):
        w_pad = jnp.pad(w_in[layer], ((0, 0), (0, PROJ_PAD_WIDTH - w_in.shape[-1]))).astype(BF16)
        gup = jnp.zeros((Z_PAD, 2 * GLA_QK_WIDTH), F32)
        gup = gup.at[:GATE_RANK, :GLA_QK_WIDTH].set(gate_up_fwd[layer])
        gup = gup.at[GATE_RANK:2 * GATE_RANK, GLA_QK_WIDTH:].set(gate_up_bwd[layer]).astype(BF16)
        gbias = jnp.concatenate([gate_bias_fwd[layer], gate_bias_bwd[layer]])[None, :]
        qg = jnp.tile(q_norm_g[layer], LANES // HEAD_DIM)[None, :]
        kg = jnp.tile(k_norm_g[layer], LANES // HEAD_DIM)[None, :]

        qa, ka, va, gq, gk, gv, go, laf, lab = _proj_call(x, w_pad, gup, gbias, qg, kg, cos, sin_hi, sin_lo)
        attn = _attn_call(qa, ka, va)
        o_f, o_b = _gla_call(gq, gk, gv, laf, lab)
        x = _mix_call(attn, o_f, o_b, go, x, w_out[layer].astype(BF16), gla_norm_g[layer][None, :],
                      ln1_g[layer][None, :], ln1_b[layer][None, :])
        x = _ffn_call(x, w_ffn_gate[layer].astype(BF16), w_ffn_up[layer].astype(BF16),
                      w_ffn_down[layer].astype(BF16), ln2_g[layer][None, :], ln2_b[layer][None, :])
    return x
```

```python
import functools

import jax
import jax.numpy as jnp
import numpy as np
from jax import lax
from jax.experimental import pallas as pl
from jax.experimental.pallas import tpu as pltpu

F32 = jnp.float32
BF16 = jnp.bfloat16

D_MODEL = 1024
GRID_W = 64
N_Q_HEADS = 8
N_KV_HEADS = 2
Q_PER_KV = N_Q_HEADS // N_KV_HEADS
HEAD_DIM = 64
AXIAL_DIM = HEAD_DIM // 2
ROPE_HALF = AXIAL_DIM // 2
ROPE_THETA = 10000.0
GLA_HEADS = 4
GLA_DK = 64
GLA_DV = 128
GATE_RANK = 16
GATE_TAU = 16.0
GLA_REF_CHUNK = 16
ATTN_WIDTH = N_Q_HEADS * HEAD_DIM
KV_WIDTH = N_KV_HEADS * HEAD_DIM
GLA_QK_WIDTH = GLA_HEADS * GLA_DK
GLA_WIDTH = GLA_HEADS * GLA_DV
D_FF = 2816
DEPTH = 1
DEEPNORM_ALPHA = (2 * DEPTH) ** 0.25
LN_EPS = 1e-5
RMS_EPS = 1e-6
LOG2_E = 1.4426950408889634

LANES = 128
VMEM_LIMIT_BYTES = 56 * 1024 * 1024

PROJ_TM = 512
ATTN_TQ = 256
ATTN_TK = 512
VT_ROWS = HEAD_DIM + 16
SCORE_BOUND_SCALE = LOG2_E * HEAD_DIM ** 0.5 * 1.01
ATTN_UNSHIFTED_MAX_LOG2 = 60.0
GLA_CHUNK = 128
GLA_BATCH = 2
MIX_TM = 512
FFN_TM = 512
FFN_CHUNK = 256

OFF_AQ = 0
OFF_AK = OFF_AQ + ATTN_WIDTH
OFF_AV = OFF_AK + KV_WIDTH
OFF_GQ = OFF_AV + KV_WIDTH
OFF_GK = OFF_GQ + GLA_QK_WIDTH
OFF_GV = OFF_GK + GLA_QK_WIDTH
OFF_GO = OFF_GV + GLA_WIDTH
OFF_Z = OFF_GO + GLA_WIDTH
Z_PAD = LANES
PROJ_PAD_WIDTH = OFF_Z + Z_PAD


def _cparams(semantics):
    return pltpu.CompilerParams(dimension_semantics=semantics, vmem_limit_bytes=VMEM_LIMIT_BYTES)


def _dot(a, b):
    return jnp.dot(a, b, preferred_element_type=F32)


def _dot_nt(a, b):
    return lax.dot_general(a, b, (((1,), (1,)), ((), ())), preferred_element_type=F32)


def _dot_tn(a, b):
    return lax.dot_general(a, b, (((0,), (0,)), ((), ())), preferred_element_type=F32)


def _rope128(y, cos, sin_hi, sin_lo):
    return y * cos + pltpu.roll(y, ROPE_HALF, 1) * sin_hi + pltpu.roll(y, LANES - ROPE_HALF, 1) * sin_lo


def _head_pair_inv_rms(blk, lane_lo):
    sq = blk * blk
    ss_lo = jnp.sum(jnp.where(lane_lo, sq, 0.0), axis=-1, keepdims=True)
    ss_hi = jnp.sum(jnp.where(lane_lo, 0.0, sq), axis=-1, keepdims=True)
    inv = 1.0 / HEAD_DIM
    return jnp.where(lane_lo, lax.rsqrt(ss_lo * inv + RMS_EPS), lax.rsqrt(ss_hi * inv + RMS_EPS))


def _proj_kernel(x_ref, w_ref, gup_ref, gbias_ref, qg_ref, kg_ref, cos_ref, shi_ref, slo_ref,
                 qa_ref, ka_ref, va_ref, gq_ref, gk_ref, gv_ref, go_ref, laf_ref, lab_ref):
    x = x_ref[0].astype(BF16)
    tm = x.shape[0]
    lane = lax.broadcasted_iota(jnp.int32, (tm, LANES), 1)
    lane_lo = lane < HEAD_DIM
    cos, shi, slo = cos_ref[...], shi_ref[...], slo_ref[...]

    aq = _dot(x, w_ref[:, OFF_AQ:OFF_AQ + ATTN_WIDTH])
    for c in range(ATTN_WIDTH // LANES):
        blk = aq[:, c * LANES:(c + 1) * LANES]
        rot = _rope128(blk * qg_ref[...], cos, shi, slo)
        out_t = (rot * (_head_pair_inv_rms(blk, lane_lo) * (LOG2_E * HEAD_DIM ** -0.5))).T.astype(BF16)
        qa_ref[0, 2 * c] = out_t[:HEAD_DIM]
        qa_ref[0, 2 * c + 1] = out_t[HEAD_DIM:]

    ak = _dot(x, w_ref[:, OFF_AK:OFF_AK + KV_WIDTH])
    rot = _rope128(ak * kg_ref[...], cos, shi, slo)
    out = (rot * _head_pair_inv_rms(ak, lane_lo)).astype(BF16)
    ka_ref[0, 0] = out[:, :HEAD_DIM]
    ka_ref[0, 1] = out[:, HEAD_DIM:]

    av = _dot(x, w_ref[:, OFF_AV:OFF_AV + KV_WIDTH])
    ones_col = jnp.where(lane == HEAD_DIM, 1.0, 0.0)
    va_ref[0, 0] = jnp.where(lane_lo, av, ones_col).T.astype(BF16)
    va_ref[0, 1] = jnp.where(lane_lo, pltpu.roll(av, HEAD_DIM, 1), ones_col).T.astype(BF16)

    gq_ref[0] = (_dot(x, w_ref[:, OFF_GQ:OFF_GQ + GLA_QK_WIDTH]) * (GLA_DK ** -0.5)).astype(BF16)
    gk_ref[0] = _dot(x, w_ref[:, OFF_GK:OFF_GK + GLA_QK_WIDTH]).astype(BF16)
    gv_ref[0] = _dot(x, w_ref[:, OFF_GV:OFF_GV + GLA_WIDTH]).astype(BF16)
    go_ref[0] = _dot(x, w_ref[:, OFF_GO:OFF_GO + GLA_WIDTH]).astype(BF16)

    z = _dot(x, w_ref[:, OFF_Z:OFF_Z + Z_PAD])
    g = _dot(z.astype(BF16), gup_ref[...]) + gbias_ref[...]
    log_a = (jnp.minimum(g, 0.0) - jnp.log(1.0 + jnp.exp(-jnp.abs(g)))) * (1.0 / GATE_TAU)
    laf_ref[0] = log_a[:, :GLA_QK_WIDTH]
    lab_ref[0] = log_a[:, GLA_QK_WIDTH:]


def _proj_call(x, w_pad, gup, gbias, qg, kg, cos, shi, slo):
    B, S, D = x.shape
    tm = PROJ_TM
    grid = (S // tm, B)
    tok = lambda s, b: (b, s, 0)
    head = lambda s, b: (b, 0, s, 0)
    head_t = lambda s, b: (b, 0, 0, s)
    const = lambda s, b: (0, 0)
    tab = lambda s, b: (s, 0)
    in_specs = [
        pl.BlockSpec((1, tm, D), tok),
        pl.BlockSpec((D, PROJ_PAD_WIDTH), const),
        pl.BlockSpec((Z_PAD, 2 * GLA_QK_WIDTH), const),
        pl.BlockSpec((1, 2 * GLA_QK_WIDTH), const),
        pl.BlockSpec((1, LANES), const),
        pl.BlockSpec((1, LANES), const),
        pl.BlockSpec((tm, LANES), tab),
        pl.BlockSpec((tm, LANES), tab),
        pl.BlockSpec((tm, LANES), tab),
    ]
    out_shape = [
        jax.ShapeDtypeStruct((B, N_Q_HEADS, HEAD_DIM, S), BF16),
        jax.ShapeDtypeStruct((B, N_KV_HEADS, S, HEAD_DIM), BF16),
        jax.ShapeDtypeStruct((B, N_KV_HEADS, LANES, S), BF16),
        jax.ShapeDtypeStruct((B, S, GLA_QK_WIDTH), BF16),
        jax.ShapeDtypeStruct((B, S, GLA_QK_WIDTH), BF16),
        jax.ShapeDtypeStruct((B, S, GLA_WIDTH), BF16),
        jax.ShapeDtypeStruct((B, S, GLA_WIDTH), BF16),
        jax.ShapeDtypeStruct((B, S, GLA_QK_WIDTH), F32),
        jax.ShapeDtypeStruct((B, S, GLA_QK_WIDTH), F32),
    ]
    out_specs = [
        pl.BlockSpec((1, N_Q_HEADS, HEAD_DIM, tm), head_t),
        pl.BlockSpec((1, N_KV_HEADS, tm, HEAD_DIM), head),
        pl.BlockSpec((1, N_KV_HEADS, LANES, tm), head_t),
        pl.BlockSpec((1, tm, GLA_QK_WIDTH), tok),
        pl.BlockSpec((1, tm, GLA_QK_WIDTH), tok),
        pl.BlockSpec((1, tm, GLA_WIDTH), tok),
        pl.BlockSpec((1, tm, GLA_WIDTH), tok),
        pl.BlockSpec((1, tm, GLA_QK_WIDTH), tok),
        pl.BlockSpec((1, tm, GLA_QK_WIDTH), tok),
    ]
    return pl.pallas_call(
        _proj_kernel, grid=grid, in_specs=in_specs, out_specs=out_specs, out_shape=out_shape,
        compiler_params=_cparams(("arbitrary", "arbitrary")),
    )(x, w_pad, gup, gbias, qg, kg, cos, shi, slo)


def _attn_kernel(bound_ref, qt_ref, k_ref, vt_ref, o_ref):
    tq = qt_ref.shape[3]
    n_keys = k_ref.shape[2]
    tk = ATTN_TK
    q_t = jnp.concatenate([qt_ref[0, r] for r in range(Q_PER_KV)], axis=1)
    unshifted_is_safe = bound_ref[0] < ATTN_UNSHIFTED_MAX_LOG2

    def finish(acc):
        o_t = acc[:HEAD_DIM] / acc[HEAD_DIM:HEAD_DIM + 1]
        for r in range(Q_PER_KV):
            o_ref[0, :, r * HEAD_DIM:(r + 1) * HEAD_DIM] = o_t[:, r * tq:(r + 1) * tq].T.astype(o_ref.dtype)

    @pl.when(unshifted_is_safe)
    def _():
        acc = None
        for j in range(n_keys // tk):
            s_t = _dot(k_ref[0, 0, j * tk:(j + 1) * tk, :], q_t)
            pv = _dot(vt_ref[0, 0, :VT_ROWS, j * tk:(j + 1) * tk], jnp.exp2(s_t).astype(BF16))
            acc = pv if acc is None else acc + pv
        finish(acc)

    @pl.when(jnp.logical_not(unshifted_is_safe))
    def _():
        m = None
        acc = None
        for j in range(n_keys // tk):
            s_t = _dot(k_ref[0, 0, j * tk:(j + 1) * tk, :], q_t)
            m_blk = jnp.max(s_t, axis=0, keepdims=True)
            m_new = m_blk if m is None else jnp.maximum(m, m_blk)
            pv = _dot(vt_ref[0, 0, :VT_ROWS, j * tk:(j + 1) * tk], jnp.exp2(s_t - m_new).astype(BF16))
            acc = pv if acc is None else jnp.exp2(m - m_new) * acc + pv
            m = m_new
        finish(acc)


def _attn_call(score_bound, qa_t, ka, va_t):
    B, _, S, _ = ka.shape
    tq = ATTN_TQ
    grid = (B, N_KV_HEADS, S // tq)
    return pl.pallas_call(
        _attn_kernel, grid=grid,
        in_specs=[
            pl.BlockSpec(memory_space=pltpu.SMEM),
            pl.BlockSpec((1, Q_PER_KV, HEAD_DIM, tq), lambda b, g, qi: (b, g, 0, qi)),
            pl.BlockSpec((1, 1, S, HEAD_DIM), lambda b, g, qi: (b, g, 0, 0)),
            pl.BlockSpec((1, 1, LANES, S), lambda b, g, qi: (b, g, 0, 0)),
        ],
        out_specs=pl.BlockSpec((1, tq, Q_PER_KV * HEAD_DIM), lambda b, g, qi: (b, qi, g)),
        out_shape=jax.ShapeDtypeStruct((B, S, ATTN_WIDTH), BF16),
        compiler_params=_cparams(("arbitrary", "arbitrary", "arbitrary")),
    )(score_bound, qa_t, ka, va_t)


def _block_row(a, blk, row):
    n, w = a.shape
    a3 = a.reshape(n // blk, blk, w)
    return jnp.broadcast_to(a3[:, row:row + 1, :], a3.shape).reshape(n, w)


def _split3_bf16(a):
    hi = a.astype(BF16)
    r1 = a - hi.astype(F32)
    mid = r1.astype(BF16)
    lo = (r1 - mid.astype(F32)).astype(BF16)
    return hi, mid, lo


def _gla_direction(bi, q_ref, k_ref, v_ref, la_ref, o_ref, state_ref, forward):
    C = q_ref.shape[1]
    row = lax.broadcasted_iota(jnp.int32, (C, C), 0)
    col = lax.broadcasted_iota(jnp.int32, (C, C), 1)
    tri = ((col <= row) if forward else (col >= row)).astype(BF16)
    la = la_ref[bi]
    cum_all = sum(_dot(tri, part) for part in _split3_bf16(la))
    excl_all = cum_all - la
    xr = row ^ col
    valid = (col <= row) if forward else (col > row)
    lane = lax.broadcasted_iota(jnp.int32, (C, LANES), 1)
    lane_lo = lane < GLA_DK
    edge = C - 1 if forward else 0
    srow = lax.broadcasted_iota(jnp.int32, (LANES, 2 * GLA_DV), 0)
    scol = lax.broadcasted_iota(jnp.int32, (LANES, 2 * GLA_DV), 1)
    on_diag = (srow < GLA_DK) == (scol < GLA_DV)

    for pair in range(GLA_HEADS // 2):
        sl = slice(pair * LANES, (pair + 1) * LANES)
        q, k = q_ref[bi, :, sl].astype(F32), k_ref[bi, :, sl].astype(F32)
        cum, excl = cum_all[:, sl], excl_all[:, sl]
        total = cum[edge:edge + 1, :]

        base_row = 0 if forward else GLA_REF_CHUNK - 1
        loc = cum - _block_row(excl, GLA_REF_CHUNK, base_row)
        q_lv = [q * jnp.exp(loc)]
        k_lv = [(k * jnp.exp(-loc)).astype(BF16)]
        blk = 2 * GLA_REF_CHUNK
        while blk <= C:
            mid = _block_row(excl if forward else cum, blk, blk // 2)
            q_lv.append(q * jnp.exp(jnp.minimum(cum - mid, 0.0)))
            k_lv.append((k * jnp.exp(jnp.minimum(mid - cum, 0.0))).astype(BF16))
            blk *= 2
        q_in = (q * jnp.exp(cum)).astype(BF16)
        k_out = k * jnp.exp(total - cum)
        k_out_t = k_out.T.astype(BF16)
        decay_t = jnp.exp(total).T

        def stack_heads(a):
            return jnp.concatenate([jnp.where(lane_lo, a, 0.0), jnp.where(lane_lo, 0.0, a)], axis=0).astype(BF16)

        scores2 = _dot_nt(stack_heads(q_lv[-1]), k_lv[-1])
        bound = C // 2
        xr2 = jnp.concatenate([xr, xr], axis=0)
        for lv in range(len(q_lv) - 2, -1, -1):
            scores2 = jnp.where(xr2 < bound, _dot_nt(stack_heads(q_lv[lv]), k_lv[lv]), scores2)
            bound //= 2
        scores2 = jnp.where(jnp.concatenate([valid, valid], axis=0), scores2, 0.0).astype(BF16)

        v2 = v_ref[bi, :, 2 * pair * GLA_DV:2 * (pair + 1) * GLA_DV]
        state = state_ref[bi, pair]
        inter = _dot(q_in, state.astype(BF16))
        for hh in range(2):
            h = 2 * pair + hh
            cols = slice(hh * GLA_DV, (hh + 1) * GLA_DV)
            o = _dot(scores2[hh * C:(hh + 1) * C], v2[:, cols]) + inter[:, cols]
            o_ref[bi, :, h * GLA_DV:(h + 1) * GLA_DV] = o.astype(o_ref.dtype)
        state_ref[bi, pair] = jnp.where(on_diag, decay_t * state + _dot(k_out_t, v2), 0.0)


def _gla_kernel(qf_ref, kf_ref, vf_ref, laf_ref, qb_ref, kb_ref, vb_ref, lab_ref,
                of_ref, ob_ref, sf_ref, sb_ref):
    @pl.when(pl.program_id(1) == 0)
    def _():
        sf_ref[...] = jnp.zeros(sf_ref.shape, F32)
        sb_ref[...] = jnp.zeros(sb_ref.shape, F32)

    for bi in range(GLA_BATCH):
        _gla_direction(bi, qf_ref, kf_ref, vf_ref, laf_ref, of_ref, sf_ref, True)
        _gla_direction(bi, qb_ref, kb_ref, vb_ref, lab_ref, ob_ref, sb_ref, False)


def _gla_call(gq, gk, gv, laf, lab):
    B, S, _ = gq.shape
    C, nb = GLA_CHUNK, GLA_BATCH
    n = S // C
    fwd = lambda b, c: (b, c, 0)
    bwd = lambda b, c: (b, n - 1 - c, 0)
    qk = lambda im: pl.BlockSpec((nb, C, GLA_QK_WIDTH), im)
    vv = lambda im: pl.BlockSpec((nb, C, GLA_WIDTH), im)
    state = pltpu.VMEM((nb, GLA_HEADS // 2, LANES, 2 * GLA_DV), F32)
    return pl.pallas_call(
        _gla_kernel, grid=(B // nb, n),
        in_specs=[qk(fwd), qk(fwd), vv(fwd), qk(fwd), qk(bwd), qk(bwd), vv(bwd), qk(bwd)],
        out_specs=[vv(fwd), vv(bwd)],
        out_shape=[jax.ShapeDtypeStruct((B, S, GLA_WIDTH), BF16)] * 2,
        scratch_shapes=[state, state],
        compiler_params=_cparams(("arbitrary", "arbitrary")),
    )(gq, gk, gv, laf, gq, gk, gv, lab)


def _layer_norm(y, g, b):
    mu = jnp.mean(y, axis=-1, keepdims=True)
    d = y - mu
    var = jnp.mean(d * d, axis=-1, keepdims=True)
    return d * lax.rsqrt(var + LN_EPS) * g + b


def _silu(g):
    return g * (1.0 / (1.0 + jnp.exp(-g)))


def _mix_kernel(attn_ref, of_ref, ob_ref, go_ref, x_ref, wo_ref, gng_ref, lng_ref, lnb_ref, out_ref):
    mixed = _dot(attn_ref[0], wo_ref[0:ATTN_WIDTH, :])
    for h in range(GLA_HEADS):
        sl = slice(h * GLA_DV, (h + 1) * GLA_DV)
        o = of_ref[0, :, sl].astype(F32) + ob_ref[0, :, sl].astype(F32)
        inv = lax.rsqrt(jnp.mean(o * o, axis=-1, keepdims=True) + RMS_EPS)
        gated = (o * inv * gng_ref[...]) * _silu(go_ref[0, :, sl].astype(F32))
        mixed = mixed + _dot(gated.astype(BF16), wo_ref[ATTN_WIDTH + h * GLA_DV:ATTN_WIDTH + (h + 1) * GLA_DV, :])
    out_ref[0] = _layer_norm(DEEPNORM_ALPHA * x_ref[0] + mixed, lng_ref[...], lnb_ref[...])


def _mix_call(attn, o_f, o_b, go, x, w_out, gng, ln_g, ln_b):
    B, S, D = x.shape
    tm = MIX_TM
    tok = lambda b, s: (b, s, 0)
    const = lambda b, s: (0, 0)
    return pl.pallas_call(
        _mix_kernel, grid=(B, S // tm),
        in_specs=[
            pl.BlockSpec((1, tm, ATTN_WIDTH), tok),
            pl.BlockSpec((1, tm, GLA_WIDTH), tok),
            pl.BlockSpec((1, tm, GLA_WIDTH), tok),
            pl.BlockSpec((1, tm, GLA_WIDTH), tok),
            pl.BlockSpec((1, tm, D), tok),
            pl.BlockSpec((ATTN_WIDTH + GLA_WIDTH, D), const),
            pl.BlockSpec((1, GLA_DV), const),
            pl.BlockSpec((1, D), const),
            pl.BlockSpec((1, D), const),
        ],
        out_specs=pl.BlockSpec((1, tm, D), tok),
        out_shape=jax.ShapeDtypeStruct((B, S, D), F32),
        compiler_params=_cparams(("arbitrary", "arbitrary")),
    )(attn, o_f, o_b, go, x, w_out, gng, ln_g, ln_b)


def _ffn_kernel(x_ref, wg_ref, wu_ref, wd_ref, lng_ref, lnb_ref, out_ref):
    x = x_ref[0]
    xb = x.astype(BF16)
    acc = DEEPNORM_ALPHA * x
    for c in range(D_FF // FFN_CHUNK):
        sl = slice(c * FFN_CHUNK, (c + 1) * FFN_CHUNK)
        hidden = _silu(_dot(xb, wg_ref[:, sl])) * _dot(xb, wu_ref[:, sl])
        acc = acc + _dot(hidden.astype(BF16), wd_ref[sl, :])
    out_ref[0] = _layer_norm(acc, lng_ref[...], lnb_ref[...])


def _ffn_call(x, w_gate, w_up, w_down, ln_g, ln_b):
    B, S, D = x.shape
    tm = FFN_TM
    tok = lambda b, s: (b, s, 0)
    const = lambda b, s: (0, 0)
    resident = lambda shape: pl.BlockSpec(shape, const, pipeline_mode=pl.Buffered(1))
    return pl.pallas_call(
        _ffn_kernel, grid=(B, S // tm),
        in_specs=[
            pl.BlockSpec((1, tm, D), tok),
            resident((D, D_FF)), resident((D, D_FF)), resident((D_FF, D)),
            pl.BlockSpec((1, D), const),
            pl.BlockSpec((1, D), const),
        ],
        out_specs=pl.BlockSpec((1, tm, D), tok),
        out_shape=jax.ShapeDtypeStruct((B, S, D), F32),
        compiler_params=_cparams(("arbitrary", "arbitrary")),
    )(x, w_gate, w_up, w_down, ln_g, ln_b)


def _rope_tables(seq_len):
    t = jnp.arange(seq_len, dtype=jnp.int32)
    row_id = (t // GRID_W).astype(F32)
    col_id = (t % GRID_W).astype(F32)
    inv_freq = ROPE_THETA ** (-jnp.arange(0, AXIAL_DIM, 2, dtype=F32) / AXIAL_DIM)
    ang_row = row_id[:, None] * inv_freq
    ang_col = col_id[:, None] * inv_freq
    zeros = jnp.zeros_like(ang_row)
    cos = jnp.concatenate([jnp.cos(ang_row)] * 2 + [jnp.cos(ang_col)] * 2, axis=-1)
    sin_hi = jnp.concatenate([zeros, jnp.sin(ang_row), zeros, jnp.sin(ang_col)], axis=-1)
    sin_lo = jnp.concatenate([-jnp.sin(ang_row), zeros, -jnp.sin(ang_col), zeros], axis=-1)
    two = lambda a: jnp.concatenate([a, a], axis=-1)
    return two(cos), two(sin_hi), two(sin_lo)


def kernel(x, w_in, q_norm_g, k_norm_g, gate_up_fwd, gate_bias_fwd, gate_up_bwd, gate_bias_bwd, gla_norm_g,
           w_out, ln1_g, ln1_b, w_ffn_gate, w_ffn_up, w_ffn_down, ln2_g, ln2_b):
    B, S, D = x.shape
    assert D == D_MODEL and S % max(PROJ_TM, ATTN_TQ, ATTN_TK, GLA_CHUNK, MIX_TM, FFN_TM) == 0
    assert w_in.shape[0] == DEPTH
    cos, sin_hi, sin_lo = _rope_tables(S)
    for layer in range(DEPTH):
        w_pad = jnp.pad(w_in[layer], ((0, 0), (0, PROJ_PAD_WIDTH - w_in.shape[-1]))).astype(BF16)
        gup = jnp.zeros((Z_PAD, 2 * GLA_QK_WIDTH), F32)
        gup = gup.at[:GATE_RANK, :GLA_QK_WIDTH].set(gate_up_fwd[layer])
        gup = gup.at[GATE_RANK:2 * GATE_RANK, GLA_QK_WIDTH:].set(gate_up_bwd[layer]).astype(BF16)
        gbias = jnp.concatenate([gate_bias_fwd[layer], gate_bias_bwd[layer]])[None, :]
        qg = jnp.tile(q_norm_g[layer], LANES // HEAD_DIM)[None, :]
        kg = jnp.tile(k_norm_g[layer], LANES // HEAD_DIM)[None, :]

        qa, ka, va, gq, gk, gv, go, laf, lab = _proj_call(x, w_pad, gup, gbias, qg, kg, cos, sin_hi, sin_lo)
        score_bound = (jnp.max(jnp.abs(q_norm_g[layer])) * jnp.max(jnp.abs(k_norm_g[layer]))
                       * SCORE_BOUND_SCALE).reshape(1).astype(F32)
        attn = _attn_call(score_bound, qa, ka, va)
        o_f, o_b = _gla_call(gq, gk, gv, laf, lab)
        x = _mix_call(attn, o_f, o_b, go, x, w_out[layer].astype(BF16), gla_norm_g[layer][None, :],
                      ln1_g[layer][None, :], ln1_b[layer][None, :])
        x = _ffn_call(x, w_ffn_gate[layer].astype(BF16), w_ffn_up[layer].astype(BF16),
                      w_ffn_down[layer].astype(BF16), ln2_g[layer][None, :], ln2_b[layer][None, :])
    return x
```

```python
import functools

import jax
import jax.numpy as jnp
import numpy as np
from jax import lax
from jax.experimental import pallas as pl
from jax.experimental.pallas import tpu as pltpu

F32 = jnp.float32
BF16 = jnp.bfloat16

D_MODEL = 1024
GRID_W = 64
N_Q_HEADS = 8
N_KV_HEADS = 2
Q_PER_KV = N_Q_HEADS // N_KV_HEADS
HEAD_DIM = 64
AXIAL_DIM = HEAD_DIM // 2
ROPE_HALF = AXIAL_DIM // 2
ROPE_THETA = 10000.0
GLA_HEADS = 4
GLA_DK = 64
GLA_DV = 128
GATE_RANK = 16
GATE_TAU = 16.0
GLA_REF_CHUNK = 16
ATTN_WIDTH = N_Q_HEADS * HEAD_DIM
KV_WIDTH = N_KV_HEADS * HEAD_DIM
GLA_QK_WIDTH = GLA_HEADS * GLA_DK
GLA_WIDTH = GLA_HEADS * GLA_DV
D_FF = 2816
DEPTH = 1
DEEPNORM_ALPHA = (2 * DEPTH) ** 0.25
LN_EPS = 1e-5
RMS_EPS = 1e-6
LOG2_E = 1.4426950408889634

LANES = 128
VMEM_LIMIT_BYTES = 56 * 1024 * 1024

PROJ_TM = 512
ATTN_TQ = 256
ATTN_TK = 512
VT_ROWS = HEAD_DIM + 16
SCORE_BOUND_SCALE = LOG2_E * HEAD_DIM ** 0.5 * 1.01
ATTN_UNSHIFTED_MAX_LOG2 = 60.0
GLA_CHUNK = 128
GLA_BATCH = 4
GLA_PHASES = 4
MIX_TM = 512
FFN_TM = 512
FFN_CHUNK = 256

OFF_AQ = 0
OFF_AK = OFF_AQ + ATTN_WIDTH
OFF_AV = OFF_AK + KV_WIDTH
OFF_GQ = OFF_AV + KV_WIDTH
OFF_GK = OFF_GQ + GLA_QK_WIDTH
OFF_GV = OFF_GK + GLA_QK_WIDTH
OFF_GO = OFF_GV + GLA_WIDTH
OFF_Z = OFF_GO + GLA_WIDTH
Z_PAD = LANES
PROJ_PAD_WIDTH = OFF_Z + Z_PAD


def _cparams(semantics):
    return pltpu.CompilerParams(dimension_semantics=semantics, vmem_limit_bytes=VMEM_LIMIT_BYTES)


def _dot(a, b):
    return jnp.dot(a, b, preferred_element_type=F32)


def _dot_nt(a, b):
    return lax.dot_general(a, b, (((1,), (1,)), ((), ())), preferred_element_type=F32)


def _dot_tn(a, b):
    return lax.dot_general(a, b, (((0,), (0,)), ((), ())), preferred_element_type=F32)


def _rope128(y, cos, sin_hi, sin_lo):
    return y * cos + pltpu.roll(y, ROPE_HALF, 1) * sin_hi + pltpu.roll(y, LANES - ROPE_HALF, 1) * sin_lo


def _head_pair_inv_rms(blk, lane_lo):
    sq = blk * blk
    ss_lo = jnp.sum(jnp.where(lane_lo, sq, 0.0), axis=-1, keepdims=True)
    ss_hi = jnp.sum(jnp.where(lane_lo, 0.0, sq), axis=-1, keepdims=True)
    inv = 1.0 / HEAD_DIM
    return jnp.where(lane_lo, lax.rsqrt(ss_lo * inv + RMS_EPS), lax.rsqrt(ss_hi * inv + RMS_EPS))


def _proj_kernel(x_ref, w_ref, gup_ref, gbias_ref, qg_ref, kg_ref, cos_ref, shi_ref, slo_ref,
                 qa_ref, ka_ref, va_ref, gq_ref, gk_ref, gv_ref, go_ref, laf_ref, lab_ref):
    x = x_ref[0].astype(BF16)
    tm = x.shape[0]
    lane = lax.broadcasted_iota(jnp.int32, (tm, LANES), 1)
    lane_lo = lane < HEAD_DIM
    cos, shi, slo = cos_ref[...], shi_ref[...], slo_ref[...]

    def proj(off, width):
        return _dot(x, w_ref[:, off:off + width])

    aq = proj(OFF_AQ, ATTN_WIDTH)
    akv = proj(OFF_AK, 2 * KV_WIDTH)
    z = proj(OFF_Z, Z_PAD)

    def q_epilogue(c):
        blk = aq[:, c * LANES:(c + 1) * LANES]
        rot = _rope128(blk * qg_ref[...], cos, shi, slo)
        out_t = (rot * (_head_pair_inv_rms(blk, lane_lo) * (LOG2_E * HEAD_DIM ** -0.5))).T.astype(BF16)
        qa_ref[0, 2 * c] = out_t[:HEAD_DIM]
        qa_ref[0, 2 * c + 1] = out_t[HEAD_DIM:]

    def k_epilogue():
        ak = akv[:, :KV_WIDTH]
        rot = _rope128(ak * kg_ref[...], cos, shi, slo)
        out = (rot * _head_pair_inv_rms(ak, lane_lo)).astype(BF16)
        ka_ref[0, 0] = out[:, :HEAD_DIM]
        ka_ref[0, 1] = out[:, HEAD_DIM:]

    def v_epilogue():
        av = akv[:, KV_WIDTH:]
        ones_col = jnp.where(lane == HEAD_DIM, 1.0, 0.0)
        va_ref[0, 0] = jnp.where(lane_lo, av, ones_col).T.astype(BF16)
        va_ref[0, 1] = jnp.where(lane_lo, pltpu.roll(av, HEAD_DIM, 1), ones_col).T.astype(BF16)

    def gate_epilogue():
        g = _dot(z.astype(BF16), gup_ref[...]) + gbias_ref[...]
        log_a = (jnp.minimum(g, 0.0) - jnp.log(1.0 + jnp.exp(-jnp.abs(g)))) * (1.0 / GATE_TAU)
        laf_ref[0] = log_a[:, :GLA_QK_WIDTH]
        lab_ref[0] = log_a[:, GLA_QK_WIDTH:]

    half = GLA_WIDTH // 2
    gq_ref[0] = (proj(OFF_GQ, GLA_QK_WIDTH) * (GLA_DK ** -0.5)).astype(BF16)
    q_epilogue(0)
    gk_ref[0] = proj(OFF_GK, GLA_QK_WIDTH).astype(BF16)
    q_epilogue(1)
    gv_ref[0, :, :half] = proj(OFF_GV, half).astype(BF16)
    q_epilogue(2)
    gv_ref[0, :, half:] = proj(OFF_GV + half, half).astype(BF16)
    q_epilogue(3)
    go_ref[0, :, :half] = proj(OFF_GO, half).astype(BF16)
    k_epilogue()
    v_epilogue()
    go_ref[0, :, half:] = proj(OFF_GO + half, half).astype(BF16)
    gate_epilogue()


def _proj_call(x, w_pad, gup, gbias, qg, kg, cos, shi, slo):
    B, S, D = x.shape
    tm = PROJ_TM
    grid = (S // tm, B)
    tok = lambda s, b: (b, s, 0)
    head = lambda s, b: (b, 0, s, 0)
    head_t = lambda s, b: (b, 0, 0, s)
    const = lambda s, b: (0, 0)
    tab = lambda s, b: (s, 0)
    in_specs = [
        pl.BlockSpec((1, tm, D), tok),
        pl.BlockSpec((D, PROJ_PAD_WIDTH), const),
        pl.BlockSpec((Z_PAD, 2 * GLA_QK_WIDTH), const),
        pl.BlockSpec((1, 2 * GLA_QK_WIDTH), const),
        pl.BlockSpec((1, LANES), const),
        pl.BlockSpec((1, LANES), const),
        pl.BlockSpec((tm, LANES), tab),
        pl.BlockSpec((tm, LANES), tab),
        pl.BlockSpec((tm, LANES), tab),
    ]
    out_shape = [
        jax.ShapeDtypeStruct((B, N_Q_HEADS, HEAD_DIM, S), BF16),
        jax.ShapeDtypeStruct((B, N_KV_HEADS, S, HEAD_DIM), BF16),
        jax.ShapeDtypeStruct((B, N_KV_HEADS, LANES, S), BF16),
        jax.ShapeDtypeStruct((B, S, GLA_QK_WIDTH), BF16),
        jax.ShapeDtypeStruct((B, S, GLA_QK_WIDTH), BF16),
        jax.ShapeDtypeStruct((B, S, GLA_WIDTH), BF16),
        jax.ShapeDtypeStruct((B, S, GLA_WIDTH), BF16),
        jax.ShapeDtypeStruct((B, S, GLA_QK_WIDTH), F32),
        jax.ShapeDtypeStruct((B, S, GLA_QK_WIDTH), F32),
    ]
    out_specs = [
        pl.BlockSpec((1, N_Q_HEADS, HEAD_DIM, tm), head_t),
        pl.BlockSpec((1, N_KV_HEADS, tm, HEAD_DIM), head),
        pl.BlockSpec((1, N_KV_HEADS, LANES, tm), head_t),
        pl.BlockSpec((1, tm, GLA_QK_WIDTH), tok),
        pl.BlockSpec((1, tm, GLA_QK_WIDTH), tok),
        pl.BlockSpec((1, tm, GLA_WIDTH), tok),
        pl.BlockSpec((1, tm, GLA_WIDTH), tok),
        pl.BlockSpec((1, tm, GLA_QK_WIDTH), tok),
        pl.BlockSpec((1, tm, GLA_QK_WIDTH), tok),
    ]
    return pl.pallas_call(
        _proj_kernel, grid=grid, in_specs=in_specs, out_specs=out_specs, out_shape=out_shape,
        compiler_params=_cparams(("arbitrary", "arbitrary")),
    )(x, w_pad, gup, gbias, qg, kg, cos, shi, slo)


def _attn_kernel(bound_ref, qt_ref, k_ref, vt_ref, o_ref):
    tq = qt_ref.shape[3]
    n_keys = k_ref.shape[2]
    tk = ATTN_TK
    q_t = jnp.concatenate([qt_ref[0, r] for r in range(Q_PER_KV)], axis=1)
    unshifted_is_safe = bound_ref[0] < ATTN_UNSHIFTED_MAX_LOG2

    def finish(acc):
        o_t = acc[:HEAD_DIM] / acc[HEAD_DIM:HEAD_DIM + 1]
        for r in range(Q_PER_KV):
            o_ref[0, :, r * HEAD_DIM:(r + 1) * HEAD_DIM] = o_t[:, r * tq:(r + 1) * tq].T.astype(o_ref.dtype)

    @pl.when(unshifted_is_safe)
    def _():
        n = n_keys // tk
        scores = lambda j: _dot(k_ref[0, 0, j * tk:(j + 1) * tk, :], q_t)
        acc = None
        s_next = scores(0)
        for j in range(n):
            s_t, s_next = s_next, (scores(j + 1) if j + 1 < n else None)
            pv = _dot(vt_ref[0, 0, :VT_ROWS, j * tk:(j + 1) * tk], jnp.exp2(s_t).astype(BF16))
            acc = pv if acc is None else acc + pv
        finish(acc)

    @pl.when(jnp.logical_not(unshifted_is_safe))
    def _():
        m = None
        acc = None
        for j in range(n_keys // tk):
            s_t = _dot(k_ref[0, 0, j * tk:(j + 1) * tk, :], q_t)
            m_blk = jnp.max(s_t, axis=0, keepdims=True)
            m_new = m_blk if m is None else jnp.maximum(m, m_blk)
            pv = _dot(vt_ref[0, 0, :VT_ROWS, j * tk:(j + 1) * tk], jnp.exp2(s_t - m_new).astype(BF16))
            acc = pv if acc is None else jnp.exp2(m - m_new) * acc + pv
            m = m_new
        finish(acc)


def _attn_call(score_bound, qa_t, ka, va_t):
    B, _, S, _ = ka.shape
    tq = ATTN_TQ
    grid = (B, N_KV_HEADS, S // tq)
    return pl.pallas_call(
        _attn_kernel, grid=grid,
        in_specs=[
            pl.BlockSpec(memory_space=pltpu.SMEM),
            pl.BlockSpec((1, Q_PER_KV, HEAD_DIM, tq), lambda b, g, qi: (b, g, 0, qi)),
            pl.BlockSpec((1, 1, S, HEAD_DIM), lambda b, g, qi: (b, g, 0, 0)),
            pl.BlockSpec((1, 1, LANES, S), lambda b, g, qi: (b, g, 0, 0)),
        ],
        out_specs=pl.BlockSpec((1, tq, Q_PER_KV * HEAD_DIM), lambda b, g, qi: (b, qi, g)),
        out_shape=jax.ShapeDtypeStruct((B, S, ATTN_WIDTH), BF16),
        compiler_params=_cparams(("arbitrary", "arbitrary", "arbitrary")),
    )(score_bound, qa_t, ka, va_t)


def _block_row(a, blk, row):
    n, w = a.shape
    a3 = a.reshape(n // blk, blk, w)
    return jnp.broadcast_to(a3[:, row:row + 1, :], a3.shape).reshape(n, w)


def _split3_bf16(a):
    hi = a.astype(BF16)
    r1 = a - hi.astype(F32)
    mid = r1.astype(BF16)
    lo = (r1 - mid.astype(F32)).astype(BF16)
    return hi, mid, lo


def _gla_direction(bi, q_ref, k_ref, v_ref, la_ref, o_ref, state_ref, forward):
    C = q_ref.shape[1]
    row = lax.broadcasted_iota(jnp.int32, (C, C), 0)
    col = lax.broadcasted_iota(jnp.int32, (C, C), 1)
    tri = ((col <= row) if forward else (col >= row)).astype(BF16)
    la = la_ref[bi]
    cum_all = sum(_dot(tri, part) for part in _split3_bf16(la))
    excl_all = cum_all - la
    xr = row ^ col
    valid = (col <= row) if forward else (col > row)
    lane = lax.broadcasted_iota(jnp.int32, (C, LANES), 1)
    lane_lo = lane < GLA_DK
    edge = C - 1 if forward else 0
    srow = lax.broadcasted_iota(jnp.int32, (LANES, 2 * GLA_DV), 0)
    scol = lax.broadcasted_iota(jnp.int32, (LANES, 2 * GLA_DV), 1)
    on_diag = (srow < GLA_DK) == (scol < GLA_DV)
    pairs = range(GLA_HEADS // 2)
    yield

    factors = []
    for pair in pairs:
        sl = slice(pair * LANES, (pair + 1) * LANES)
        q, k = q_ref[bi, :, sl].astype(F32), k_ref[bi, :, sl].astype(F32)
        cum, excl = cum_all[:, sl], excl_all[:, sl]
        total = cum[edge:edge + 1, :]

        base_row = 0 if forward else GLA_REF_CHUNK - 1
        loc = cum - _block_row(excl, GLA_REF_CHUNK, base_row)
        q_lv = [q * jnp.exp(loc)]
        k_lv = [(k * jnp.exp(-loc)).astype(BF16)]
        blk = 2 * GLA_REF_CHUNK
        while blk <= C:
            mid = _block_row(excl if forward else cum, blk, blk // 2)
            e = jnp.exp(-jnp.abs(cum - mid))
            q_lv.append(q * e)
            k_lv.append((k * e).astype(BF16))
            blk *= 2
        q_in = (q * jnp.exp(cum)).astype(BF16)
        k_out = k * jnp.exp(total - cum)
        k_out_t = k_out.T.astype(BF16)
        decay_t = jnp.exp(total).T
        factors.append((q_lv, k_lv, q_in, k_out_t, decay_t))
    yield

    def stack_heads(a):
        return jnp.concatenate([jnp.where(lane_lo, a, 0.0), jnp.where(lane_lo, 0.0, a)], axis=0).astype(BF16)

    xr2 = jnp.concatenate([xr, xr], axis=0)
    valid2 = jnp.concatenate([valid, valid], axis=0)
    scores = []
    for pair in pairs:
        q_lv, k_lv = factors[pair][:2]
        scores2 = _dot_nt(stack_heads(q_lv[-1]), k_lv[-1])
        bound = C // 2
        for lv in range(len(q_lv) - 2, -1, -1):
            scores2 = jnp.where(xr2 < bound, _dot_nt(stack_heads(q_lv[lv]), k_lv[lv]), scores2)
            bound //= 2
        scores.append(jnp.where(valid2, scores2, 0.0).astype(BF16))
    yield

    for pair in pairs:
        q_in, k_out_t, decay_t = factors[pair][2:]
        v2 = v_ref[bi, :, 2 * pair * GLA_DV:2 * (pair + 1) * GLA_DV]
        state = state_ref[bi, pair]
        inter = _dot(q_in, state.astype(BF16))
        for hh in range(2):
            h = 2 * pair + hh
            cols = slice(hh * GLA_DV, (hh + 1) * GLA_DV)
            o = _dot(scores[pair][hh * C:(hh + 1) * C], v2[:, cols]) + inter[:, cols]
            o_ref[bi, :, h * GLA_DV:(h + 1) * GLA_DV] = o.astype(o_ref.dtype)
        state_ref[bi, pair] = jnp.where(on_diag, decay_t * state + _dot(k_out_t, v2), 0.0)
    yield


def _gla_kernel(qf_ref, kf_ref, vf_ref, laf_ref, qb_ref, kb_ref, vb_ref, lab_ref,
                of_ref, ob_ref, sf_ref, sb_ref):
    @pl.when(pl.program_id(1) == 0)
    def _():
        sf_ref[...] = jnp.zeros(sf_ref.shape, F32)
        sb_ref[...] = jnp.zeros(sb_ref.shape, F32)

    chains = []
    for bi in range(GLA_BATCH):
        chains.append(_gla_direction(bi, qf_ref, kf_ref, vf_ref, laf_ref, of_ref, sf_ref, True))
        chains.append(_gla_direction(bi, qb_ref, kb_ref, vb_ref, lab_ref, ob_ref, sb_ref, False))
    for _ in range(GLA_PHASES):
        for chain in chains:
            next(chain)


def _gla_call(gq, gk, gv, laf, lab):
    B, S, _ = gq.shape
    C, nb = GLA_CHUNK, GLA_BATCH
    n = S // C
    fwd = lambda b, c: (b, c, 0)
    bwd = lambda b, c: (b, n - 1 - c, 0)
    qk = lambda im: pl.BlockSpec((nb, C, GLA_QK_WIDTH), im)
    vv = lambda im: pl.BlockSpec((nb, C, GLA_WIDTH), im)
    state = pltpu.VMEM((nb, GLA_HEADS // 2, LANES, 2 * GLA_DV), F32)
    return pl.pallas_call(
        _gla_kernel, grid=(B // nb, n),
        in_specs=[qk(fwd), qk(fwd), vv(fwd), qk(fwd), qk(bwd), qk(bwd), vv(bwd), qk(bwd)],
        out_specs=[vv(fwd), vv(bwd)],
        out_shape=[jax.ShapeDtypeStruct((B, S, GLA_WIDTH), BF16)] * 2,
        scratch_shapes=[state, state],
        compiler_params=_cparams(("arbitrary", "arbitrary")),
    )(gq, gk, gv, laf, gq, gk, gv, lab)


def _layer_norm(y, g, b):
    mu = jnp.mean(y, axis=-1, keepdims=True)
    d = y - mu
    var = jnp.mean(d * d, axis=-1, keepdims=True)
    return d * lax.rsqrt(var + LN_EPS) * g + b


def _silu(g):
    return g * (1.0 / (1.0 + jnp.exp(-g)))


def _mix_kernel(attn_ref, of_ref, ob_ref, go_ref, x_ref, wo_ref, gng_ref, lng_ref, lnb_ref, out_ref):
    mixed = _dot(attn_ref[0], wo_ref[0:ATTN_WIDTH, :])
    for h in range(GLA_HEADS):
        sl = slice(h * GLA_DV, (h + 1) * GLA_DV)
        o = of_ref[0, :, sl].astype(F32) + ob_ref[0, :, sl].astype(F32)
        inv = lax.rsqrt(jnp.mean(o * o, axis=-1, keepdims=True) + RMS_EPS)
        gated = (o * inv * gng_ref[...]) * _silu(go_ref[0, :, sl].astype(F32))
        mixed = mixed + _dot(gated.astype(BF16), wo_ref[ATTN_WIDTH + h * GLA_DV:ATTN_WIDTH + (h + 1) * GLA_DV, :])
    out_ref[0] = _layer_norm(DEEPNORM_ALPHA * x_ref[0] + mixed, lng_ref[...], lnb_ref[...])


def _mix_call(attn, o_f, o_b, go, x, w_out, gng, ln_g, ln_b):
    B, S, D = x.shape
    tm = MIX_TM
    tok = lambda b, s: (b, s, 0)
    const = lambda b, s: (0, 0)
    return pl.pallas_call(
        _mix_kernel, grid=(B, S // tm),
        in_specs=[
            pl.BlockSpec((1, tm, ATTN_WIDTH), tok),
            pl.BlockSpec((1, tm, GLA_WIDTH), tok),
            pl.BlockSpec((1, tm, GLA_WIDTH), tok),
            pl.BlockSpec((1, tm, GLA_WIDTH), tok),
            pl.BlockSpec((1, tm, D), tok),
            pl.BlockSpec((ATTN_WIDTH + GLA_WIDTH, D), const),
            pl.BlockSpec((1, GLA_DV), const),
            pl.BlockSpec((1, D), const),
            pl.BlockSpec((1, D), const),
        ],
        out_specs=pl.BlockSpec((1, tm, D), tok),
        out_shape=jax.ShapeDtypeStruct((B, S, D), F32),
        compiler_params=_cparams(("arbitrary", "arbitrary")),
    )(attn, o_f, o_b, go, x, w_out, gng, ln_g, ln_b)


def _ffn_kernel(x_ref, wg_ref, wu_ref, wd_ref, lng_ref, lnb_ref, out_ref):
    x = x_ref[0]
    xb = x.astype(BF16)
    acc = DEEPNORM_ALPHA * x
    for c in range(D_FF // FFN_CHUNK):
        sl = slice(c * FFN_CHUNK, (c + 1) * FFN_CHUNK)
        hidden = _silu(_dot(xb, wg_ref[:, sl])) * _dot(xb, wu_ref[:, sl])
        acc = acc + _dot(hidden.astype(BF16), wd_ref[sl, :])
    out_ref[0] = _layer_norm(acc, lng_ref[...], lnb_ref[...])


def _ffn_call(x, w_gate, w_up, w_down, ln_g, ln_b):
    B, S, D = x.shape
    tm = FFN_TM
    tok = lambda b, s: (b, s, 0)
    const = lambda b, s: (0, 0)
    resident = lambda shape: pl.BlockSpec(shape, const, pipeline_mode=pl.Buffered(1))
    return pl.pallas_call(
        _ffn_kernel, grid=(B, S // tm),
        in_specs=[
            pl.BlockSpec((1, tm, D), tok),
            resident((D, D_FF)), resident((D, D_FF)), resident((D_FF, D)),
            pl.BlockSpec((1, D), const),
            pl.BlockSpec((1, D), const),
        ],
        out_specs=pl.BlockSpec((1, tm, D), tok),
        out_shape=jax.ShapeDtypeStruct((B, S, D), F32),
        compiler_params=_cparams(("arbitrary", "arbitrary")),
    )(x, w_gate, w_up, w_down, ln_g, ln_b)


def _rope_tables(seq_len):
    t = jnp.arange(seq_len, dtype=jnp.int32)
    row_id = (t // GRID_W).astype(F32)
    col_id = (t % GRID_W).astype(F32)
    inv_freq = ROPE_THETA ** (-jnp.arange(0, AXIAL_DIM, 2, dtype=F32) / AXIAL_DIM)
    ang_row = row_id[:, None] * inv_freq
    ang_col = col_id[:, None] * inv_freq
    zeros = jnp.zeros_like(ang_row)
    cos = jnp.concatenate([jnp.cos(ang_row)] * 2 + [jnp.cos(ang_col)] * 2, axis=-1)
    sin_hi = jnp.concatenate([zeros, jnp.sin(ang_row), zeros, jnp.sin(ang_col)], axis=-1)
    sin_lo = jnp.concatenate([-jnp.sin(ang_row), zeros, -jnp.sin(ang_col), zeros], axis=-1)
    two = lambda a: jnp.concatenate([a, a], axis=-1)
    return two(cos), two(sin_hi), two(sin_lo)


def kernel(x, w_in, q_norm_g, k_norm_g, gate_up_fwd, gate_bias_fwd, gate_up_bwd, gate_bias_bwd, gla_norm_g,
           w_out, ln1_g, ln1_b, w_ffn_gate, w_ffn_up, w_ffn_down, ln2_g, ln2_b):
    B, S, D = x.shape
    assert D == D_MODEL and S % max(PROJ_TM, ATTN_TQ, ATTN_TK, GLA_CHUNK, MIX_TM, FFN_TM) == 0
    assert w_in.shape[0] == DEPTH
    cos, sin_hi, sin_lo = _rope_tables(S)
    for layer in range(DEPTH):
        w_pad = jnp.pad(w_in[layer], ((0, 0), (0, PROJ_PAD_WIDTH - w_in.shape[-1]))).astype(BF16)
        gup = jnp.zeros((Z_PAD, 2 * GLA_QK_WIDTH), F32)
        gup = gup.at[:GATE_RANK, :GLA_QK_WIDTH].set(gate_up_fwd[layer])
        gup = gup.at[GATE_RANK:2 * GATE_RANK, GLA_QK_WIDTH:].set(gate_up_bwd[layer]).astype(BF16)
        gbias = jnp.concatenate([gate_bias_fwd[layer], gate_bias_bwd[layer]])[None, :]
        qg = jnp.tile(q_norm_g[layer], LANES // HEAD_DIM)[None, :]
        kg = jnp.tile(k_norm_g[layer], LANES // HEAD_DIM)[None, :]

        qa, ka, va, gq, gk, gv, go, laf, lab = _proj_call(x, w_pad, gup, gbias, qg, kg, cos, sin_hi, sin_lo)
        score_bound = (jnp.max(jnp.abs(q_norm_g[layer])) * jnp.max(jnp.abs(k_norm_g[layer]))
                       * SCORE_BOUND_SCALE).reshape(1).astype(F32)
        attn = _attn_call(score_bound, qa, ka, va)
        o_f, o_b = _gla_call(gq, gk, gv, laf, lab)
        x = _mix_call(attn, o_f, o_b, go, x, w_out[layer].astype(BF16), gla_norm_g[layer][None, :],
                      ln1_g[layer][None, :], ln1_b[layer][None, :])
        x = _ffn_call(x, w_ffn_gate[layer].astype(BF16), w_ffn_up[layer].astype(BF16),
                      w_ffn_down[layer].astype(BF16), ln2_g[layer][None, :], ln2_b[layer][None, :])
    return x
```

```python
import jax
import jax.numpy as jnp
from jax import lax
from jax.experimental import pallas as pl
from jax.experimental.pallas import tpu as pltpu

F32 = jnp.float32
BF16 = jnp.bfloat16

D_MODEL = 1024
GRID_W = 64
N_Q_HEADS = 8
N_KV_HEADS = 2
Q_PER_KV = N_Q_HEADS // N_KV_HEADS
HEAD_DIM = 64
AXIAL_DIM = HEAD_DIM // 2
ROPE_HALF = AXIAL_DIM // 2
ROPE_THETA = 10000.0
GLA_HEADS = 4
GLA_DK = 64
GLA_DV = 128
GATE_RANK = 16
GATE_TAU = 16.0
GLA_REF_CHUNK = 16
ATTN_WIDTH = N_Q_HEADS * HEAD_DIM
KV_WIDTH = N_KV_HEADS * HEAD_DIM
GLA_QK_WIDTH = GLA_HEADS * GLA_DK
GLA_WIDTH = GLA_HEADS * GLA_DV
D_FF = 2816
DEPTH = 1
DEEPNORM_ALPHA = (2 * DEPTH) ** 0.25
LN_EPS = 1e-5
RMS_EPS = 1e-6
LOG2_E = 1.4426950408889634

LANES = 128
VMEM_LIMIT_BYTES = 56 * 1024 * 1024

PROJ_TM = 512
ATTN_TQ = 512
ATTN_TK = 256
VT_ROWS = HEAD_DIM + 16
SCORE_BOUND_SCALE = LOG2_E * HEAD_DIM ** 0.5 * 1.01
ATTN_UNSHIFTED_MAX_LOG2 = 60.0
GLA_CHUNK = 128
GLA_BATCH = 4
GLA_PHASES = 4
MIX_FFN_TM = 512
FFN_CHUNK = 256

OFF_AQ = 0
OFF_AK = OFF_AQ + ATTN_WIDTH
OFF_AV = OFF_AK + KV_WIDTH
OFF_GQ = OFF_AV + KV_WIDTH
OFF_GK = OFF_GQ + GLA_QK_WIDTH
OFF_GV = OFF_GK + GLA_QK_WIDTH
OFF_GO = OFF_GV + GLA_WIDTH
OFF_Z = OFF_GO + GLA_WIDTH
Z_PAD = LANES
PROJ_PAD_WIDTH = OFF_Z + Z_PAD


def _cparams(semantics):
    return pltpu.CompilerParams(dimension_semantics=semantics, vmem_limit_bytes=VMEM_LIMIT_BYTES)


def _dot(a, b):
    return jnp.dot(a, b, preferred_element_type=F32)


def _dot_nt(a, b):
    return lax.dot_general(a, b, (((1,), (1,)), ((), ())), preferred_element_type=F32)


def _dot_tn(a, b):
    return lax.dot_general(a, b, (((0,), (0,)), ((), ())), preferred_element_type=F32)


def _rope128(y, cos, sin_hi, sin_lo):
    return y * cos + pltpu.roll(y, ROPE_HALF, 1) * sin_hi + pltpu.roll(y, LANES - ROPE_HALF, 1) * sin_lo


def _head_pair_inv_rms(blk, lane_lo):
    sq = blk * blk
    ss_lo = jnp.sum(jnp.where(lane_lo, sq, 0.0), axis=-1, keepdims=True)
    ss_hi = jnp.sum(jnp.where(lane_lo, 0.0, sq), axis=-1, keepdims=True)
    inv = 1.0 / HEAD_DIM
    return jnp.where(lane_lo, lax.rsqrt(ss_lo * inv + RMS_EPS), lax.rsqrt(ss_hi * inv + RMS_EPS))


def _proj_kernel(x_ref, w_ref, gup_ref, gbias_ref, qg_ref, kg_ref, cos_ref, shi_ref, slo_ref,
                 qa_ref, ka_ref, va_ref, gq_ref, gk_ref, gv_ref, go_ref, laf_ref, lab_ref):
    x = x_ref[0].astype(BF16)
    tm = x.shape[0]
    lane = lax.broadcasted_iota(jnp.int32, (tm, LANES), 1)
    lane_lo = lane < HEAD_DIM
    cos, shi, slo = cos_ref[...], shi_ref[...], slo_ref[...]

    def proj(off, width):
        return _dot(x, w_ref[:, off:off + width])

    aq = proj(OFF_AQ, ATTN_WIDTH)
    akv = proj(OFF_AK, 2 * KV_WIDTH)
    z = proj(OFF_Z, Z_PAD)

    def q_epilogue(c):
        blk = aq[:, c * LANES:(c + 1) * LANES]
        rot = _rope128(blk * qg_ref[...], cos, shi, slo)
        out_t = (rot * (_head_pair_inv_rms(blk, lane_lo) * (LOG2_E * HEAD_DIM ** -0.5))).T.astype(BF16)
        qa_ref[0, 2 * c] = out_t[:HEAD_DIM]
        qa_ref[0, 2 * c + 1] = out_t[HEAD_DIM:]

    def k_epilogue():
        ak = akv[:, :KV_WIDTH]
        rot = _rope128(ak * kg_ref[...], cos, shi, slo)
        out = (rot * _head_pair_inv_rms(ak, lane_lo)).astype(BF16)
        ka_ref[0, 0] = out[:, :HEAD_DIM]
        ka_ref[0, 1] = out[:, HEAD_DIM:]

    def v_epilogue():
        av = akv[:, KV_WIDTH:]
        ones_col = jnp.where(lane == HEAD_DIM, 1.0, 0.0)
        va_ref[0, 0] = jnp.where(lane_lo, av, ones_col).T.astype(BF16)
        va_ref[0, 1] = jnp.where(lane_lo, pltpu.roll(av, HEAD_DIM, 1), ones_col).T.astype(BF16)

    def gate_epilogue():
        g = _dot(z.astype(BF16), gup_ref[...]) + gbias_ref[...]
        log_a = (jnp.minimum(g, 0.0) - jnp.log(1.0 + jnp.exp(-jnp.abs(g)))) * (1.0 / GATE_TAU)
        laf_ref[0] = log_a[:, :GLA_QK_WIDTH]
        lab_ref[0] = log_a[:, GLA_QK_WIDTH:]

    half = GLA_WIDTH // 2
    gq_ref[0] = (proj(OFF_GQ, GLA_QK_WIDTH) * (GLA_DK ** -0.5)).astype(BF16)
    q_epilogue(0)
    gk_ref[0] = proj(OFF_GK, GLA_QK_WIDTH).astype(BF16)
    q_epilogue(1)
    gv_ref[0, :, :half] = proj(OFF_GV, half).astype(BF16)
    q_epilogue(2)
    gv_ref[0, :, half:] = proj(OFF_GV + half, half).astype(BF16)
    q_epilogue(3)
    go_ref[0, :, :half] = proj(OFF_GO, half).astype(BF16)
    k_epilogue()
    v_epilogue()
    go_ref[0, :, half:] = proj(OFF_GO + half, half).astype(BF16)
    gate_epilogue()


def _proj_call(x, w_pad, gup, gbias, qg, kg, cos, shi, slo):
    B, S, D = x.shape
    tm = PROJ_TM
    grid = (S // tm, B)
    tok = lambda s, b: (b, s, 0)
    head = lambda s, b: (b, 0, s, 0)
    head_t = lambda s, b: (b, 0, 0, s)
    const = lambda s, b: (0, 0)
    tab = lambda s, b: (s, 0)
    in_specs = [
        pl.BlockSpec((1, tm, D), tok),
        pl.BlockSpec((D, PROJ_PAD_WIDTH), const),
        pl.BlockSpec((Z_PAD, 2 * GLA_QK_WIDTH), const),
        pl.BlockSpec((1, 2 * GLA_QK_WIDTH), const),
        pl.BlockSpec((1, LANES), const),
        pl.BlockSpec((1, LANES), const),
        pl.BlockSpec((tm, LANES), tab),
        pl.BlockSpec((tm, LANES), tab),
        pl.BlockSpec((tm, LANES), tab),
    ]
    out_shape = [
        jax.ShapeDtypeStruct((B, N_Q_HEADS, HEAD_DIM, S), BF16),
        jax.ShapeDtypeStruct((B, N_KV_HEADS, S, HEAD_DIM), BF16),
        jax.ShapeDtypeStruct((B, N_KV_HEADS, LANES, S), BF16),
        jax.ShapeDtypeStruct((B, S, GLA_QK_WIDTH), BF16),
        jax.ShapeDtypeStruct((B, S, GLA_QK_WIDTH), BF16),
        jax.ShapeDtypeStruct((B, S, GLA_WIDTH), BF16),
        jax.ShapeDtypeStruct((B, S, GLA_WIDTH), BF16),
        jax.ShapeDtypeStruct((B, S, GLA_QK_WIDTH), F32),
        jax.ShapeDtypeStruct((B, S, GLA_QK_WIDTH), F32),
    ]
    out_specs = [
        pl.BlockSpec((1, N_Q_HEADS, HEAD_DIM, tm), head_t),
        pl.BlockSpec((1, N_KV_HEADS, tm, HEAD_DIM), head),
        pl.BlockSpec((1, N_KV_HEADS, LANES, tm), head_t),
        pl.BlockSpec((1, tm, GLA_QK_WIDTH), tok),
        pl.BlockSpec((1, tm, GLA_QK_WIDTH), tok),
        pl.BlockSpec((1, tm, GLA_WIDTH), tok),
        pl.BlockSpec((1, tm, GLA_WIDTH), tok),
        pl.BlockSpec((1, tm, GLA_QK_WIDTH), tok),
        pl.BlockSpec((1, tm, GLA_QK_WIDTH), tok),
    ]
    return pl.pallas_call(
        _proj_kernel, grid=grid, in_specs=in_specs, out_specs=out_specs, out_shape=out_shape,
        compiler_params=_cparams(("arbitrary", "arbitrary")),
    )(x, w_pad, gup, gbias, qg, kg, cos, shi, slo)


def _attn_kernel(bound_ref, qt_ref, k_ref, vt_ref, o_ref):
    tq = qt_ref.shape[3]
    n_keys = k_ref.shape[2]
    tk = ATTN_TK
    q_t = jnp.concatenate([qt_ref[0, r] for r in range(Q_PER_KV)], axis=1)
    unshifted_is_safe = bound_ref[0] < ATTN_UNSHIFTED_MAX_LOG2

    def finish(acc):
        o_t = acc[:HEAD_DIM] / acc[HEAD_DIM:HEAD_DIM + 1]
        for r in range(Q_PER_KV):
            o_ref[0, :, r * HEAD_DIM:(r + 1) * HEAD_DIM] = o_t[:, r * tq:(r + 1) * tq].T.astype(o_ref.dtype)

    @pl.when(unshifted_is_safe)
    def _():
        n = n_keys // tk
        scores = lambda j: _dot(k_ref[0, 0, j * tk:(j + 1) * tk, :], q_t)
        acc = None
        s_next = scores(0)
        for j in range(n):
            s_t, s_next = s_next, (scores(j + 1) if j + 1 < n else None)
            pv = _dot(vt_ref[0, 0, :VT_ROWS, j * tk:(j + 1) * tk], jnp.exp2(s_t).astype(BF16))
            acc = pv if acc is None else acc + pv
        finish(acc)

    @pl.when(jnp.logical_not(unshifted_is_safe))
    def _():
        m = None
        acc = None
        for j in range(n_keys // tk):
            s_t = _dot(k_ref[0, 0, j * tk:(j + 1) * tk, :], q_t)
            m_blk = jnp.max(s_t, axis=0, keepdims=True)
            m_new = m_blk if m is None else jnp.maximum(m, m_blk)
            pv = _dot(vt_ref[0, 0, :VT_ROWS, j * tk:(j + 1) * tk], jnp.exp2(s_t - m_new).astype(BF16))
            acc = pv if acc is None else jnp.exp2(m - m_new) * acc + pv
            m = m_new
        finish(acc)


def _attn_call(score_bound, qa_t, ka, va_t):
    B, _, S, _ = ka.shape
    tq = ATTN_TQ
    grid = (B, N_KV_HEADS, S // tq)
    return pl.pallas_call(
        _attn_kernel, grid=grid,
        in_specs=[
            pl.BlockSpec(memory_space=pltpu.SMEM),
            pl.BlockSpec((1, Q_PER_KV, HEAD_DIM, tq), lambda b, g, qi: (b, g, 0, qi)),
            pl.BlockSpec((1, 1, S, HEAD_DIM), lambda b, g, qi: (b, g, 0, 0)),
            pl.BlockSpec((1, 1, LANES, S), lambda b, g, qi: (b, g, 0, 0)),
        ],
        out_specs=pl.BlockSpec((1, tq, Q_PER_KV * HEAD_DIM), lambda b, g, qi: (b, qi, g)),
        out_shape=jax.ShapeDtypeStruct((B, S, ATTN_WIDTH), BF16),
        compiler_params=_cparams(("arbitrary", "arbitrary", "arbitrary")),
    )(score_bound, qa_t, ka, va_t)


def _block_row(a, blk, row):
    n, w = a.shape
    a3 = a.reshape(n // blk, blk, w)
    return jnp.broadcast_to(a3[:, row:row + 1, :], a3.shape).reshape(n, w)


def _split3_bf16(a):
    hi = a.astype(BF16)
    r1 = a - hi.astype(F32)
    mid = r1.astype(BF16)
    lo = (r1 - mid.astype(F32)).astype(BF16)
    return hi, mid, lo


def _gla_direction(bi, q_ref, k_ref, v_ref, la_ref, o_ref, state_ref, forward):
    C = q_ref.shape[1]
    row = lax.broadcasted_iota(jnp.int32, (C, C), 0)
    col = lax.broadcasted_iota(jnp.int32, (C, C), 1)
    tri = ((col <= row) if forward else (col >= row)).astype(BF16)
    la = la_ref[bi]
    cum_all = sum(_dot(tri, part) for part in _split3_bf16(la))
    excl_all = cum_all - la
    xr = row ^ col
    valid = (col <= row) if forward else (col > row)
    lane = lax.broadcasted_iota(jnp.int32, (C, LANES), 1)
    lane_lo = lane < GLA_DK
    edge = C - 1 if forward else 0
    srow = lax.broadcasted_iota(jnp.int32, (LANES, 2 * GLA_DV), 0)
    scol = lax.broadcasted_iota(jnp.int32, (LANES, 2 * GLA_DV), 1)
    on_diag = (srow < GLA_DK) == (scol < GLA_DV)
    pairs = range(GLA_HEADS // 2)
    yield

    factors = []
    for pair in pairs:
        sl = slice(pair * LANES, (pair + 1) * LANES)
        q, k = q_ref[bi, :, sl].astype(F32), k_ref[bi, :, sl].astype(F32)
        cum, excl = cum_all[:, sl], excl_all[:, sl]
        total = cum[edge:edge + 1, :]

        base_row = 0 if forward else GLA_REF_CHUNK - 1
        loc = cum - _block_row(excl, GLA_REF_CHUNK, base_row)
        q_lv = [q * jnp.exp(loc)]
        k_lv = [(k * jnp.exp(-loc)).astype(BF16)]
        blk = 2 * GLA_REF_CHUNK
        while blk <= C:
            mid = _block_row(excl if forward else cum, blk, blk // 2)
            e = jnp.exp(-jnp.abs(cum - mid))
            q_lv.append(q * e)
            k_lv.append((k * e).astype(BF16))
            blk *= 2
        q_in = (q * jnp.exp(cum)).astype(BF16)
        k_out = k * jnp.exp(total - cum)
        k_out_t = k_out.T.astype(BF16)
        decay_t = jnp.exp(total).T
        factors.append((q_lv, k_lv, q_in, k_out_t, decay_t))
    yield

    def stack_heads(a):
        return jnp.concatenate([jnp.where(lane_lo, a, 0.0), jnp.where(lane_lo, 0.0, a)], axis=0).astype(BF16)

    xr2 = jnp.concatenate([xr, xr], axis=0)
    valid2 = jnp.concatenate([valid, valid], axis=0)
    scores = []
    for pair in pairs:
        q_lv, k_lv = factors[pair][:2]
        scores2 = _dot_nt(stack_heads(q_lv[-1]), k_lv[-1])
        bound = C // 2
        for lv in range(len(q_lv) - 2, -1, -1):
            scores2 = jnp.where(xr2 < bound, _dot_nt(stack_heads(q_lv[lv]), k_lv[lv]), scores2)
            bound //= 2
        scores.append(jnp.where(valid2, scores2, 0.0).astype(BF16))
    yield

    for pair in pairs:
        q_in, k_out_t, decay_t = factors[pair][2:]
        v2 = v_ref[bi, :, 2 * pair * GLA_DV:2 * (pair + 1) * GLA_DV]
        state = state_ref[bi, pair]
        inter = _dot(q_in, state.astype(BF16))
        for hh in range(2):
            h = 2 * pair + hh
            cols = slice(hh * GLA_DV, (hh + 1) * GLA_DV)
            o = _dot(scores[pair][hh * C:(hh + 1) * C], v2[:, cols]) + inter[:, cols]
            o_ref[bi, :, h * GLA_DV:(h + 1) * GLA_DV] = o.astype(o_ref.dtype)
        state_ref[bi, pair] = jnp.where(on_diag, decay_t * state + _dot(k_out_t, v2), 0.0)
    yield


def _gla_kernel(qf_ref, kf_ref, vf_ref, laf_ref, qb_ref, kb_ref, vb_ref, lab_ref,
                of_ref, ob_ref, sf_ref, sb_ref):
    @pl.when(pl.program_id(1) == 0)
    def _():
        sf_ref[...] = jnp.zeros(sf_ref.shape, F32)
        sb_ref[...] = jnp.zeros(sb_ref.shape, F32)

    chains = []
    for bi in range(GLA_BATCH):
        chains.append(_gla_direction(bi, qf_ref, kf_ref, vf_ref, laf_ref, of_ref, sf_ref, True))
        chains.append(_gla_direction(bi, qb_ref, kb_ref, vb_ref, lab_ref, ob_ref, sb_ref, False))
    for _ in range(GLA_PHASES):
        for chain in chains:
            next(chain)


def _gla_call(gq, gk, gv, laf, lab):
    B, S, _ = gq.shape
    C, nb = GLA_CHUNK, GLA_BATCH
    n = S // C
    fwd = lambda b, c: (b, c, 0)
    bwd = lambda b, c: (b, n - 1 - c, 0)
    qk = lambda im: pl.BlockSpec((nb, C, GLA_QK_WIDTH), im)
    vv = lambda im: pl.BlockSpec((nb, C, GLA_WIDTH), im)
    state = pltpu.VMEM((nb, GLA_HEADS // 2, LANES, 2 * GLA_DV), F32)
    return pl.pallas_call(
        _gla_kernel, grid=(B // nb, n),
        in_specs=[qk(fwd), qk(fwd), vv(fwd), qk(fwd), qk(bwd), qk(bwd), vv(bwd), qk(bwd)],
        out_specs=[vv(fwd), vv(bwd)],
        out_shape=[jax.ShapeDtypeStruct((B, S, GLA_WIDTH), BF16)] * 2,
        scratch_shapes=[state, state],
        compiler_params=_cparams(("arbitrary", "arbitrary")),
    )(gq, gk, gv, laf, gq, gk, gv, lab)


def _layer_norm(y, g, b):
    mu = jnp.mean(y, axis=-1, keepdims=True)
    d = y - mu
    var = jnp.mean(d * d, axis=-1, keepdims=True)
    return d * lax.rsqrt(var + LN_EPS) * g + b


def _silu(g):
    return g * (1.0 / (1.0 + jnp.exp(-g)))


def _mix_ffn_kernel(attn_ref, of_ref, ob_ref, go_ref, x_ref, wo_ref, gng_ref, ln1g_ref, ln1b_ref,
                    wg_ref, wu_ref, wd_ref, ln2g_ref, ln2b_ref, out_ref, x1_sc, x1b_sc):
    t = pl.program_id(0)
    cur, prev = t % 2, (t + 1) % 2

    @pl.when(t == 0)
    def _():
        x1_sc[1] = jnp.zeros(x1_sc.shape[1:], F32)
        x1b_sc[1] = jnp.zeros(x1b_sc.shape[1:], BF16)

    hidden = []

    def ffn_chunk(c):
        sl = slice(c * FFN_CHUNK, (c + 1) * FFN_CHUNK)
        xb = x1b_sc[prev]
        hidden.append((_silu(_dot(xb, wg_ref[:, sl])) * _dot(xb, wu_ref[:, sl])).astype(BF16))

    def gate_head(h):
        sl = slice(h * GLA_DV, (h + 1) * GLA_DV)
        o = of_ref[:, sl].astype(F32) + ob_ref[:, sl].astype(F32)
        inv = lax.rsqrt(jnp.mean(o * o, axis=-1, keepdims=True) + RMS_EPS)
        return ((o * inv * gng_ref[...]) * _silu(go_ref[:, sl].astype(F32))).astype(BF16)

    n_chunks = D_FF // FFN_CHUNK
    gated = []
    for c in range(GLA_HEADS):
        ffn_chunk(c)
        gated.append(gate_head(c))
    ffn_chunk(GLA_HEADS)
    merged = jnp.concatenate([attn_ref[...]] + gated, axis=-1)
    mixed = _dot(merged, wo_ref[...])
    ffn_chunk(GLA_HEADS + 1)
    ffn_chunk(GLA_HEADS + 2)
    x1 = _layer_norm(DEEPNORM_ALPHA * x_ref[...] + mixed, ln1g_ref[...], ln1b_ref[...])
    x1_sc[cur] = x1
    x1b_sc[cur] = x1.astype(BF16)
    for c in range(GLA_HEADS + 3, n_chunks):
        ffn_chunk(c)
    ffn = _dot(jnp.concatenate(hidden, axis=-1), wd_ref[...])
    out_ref[...] = _layer_norm(DEEPNORM_ALPHA * x1_sc[prev] + ffn, ln2g_ref[...], ln2b_ref[...])


def _mix_ffn_call(attn, o_f, o_b, go, x, w_out, gng, ln1_g, ln1_b, w_gate, w_up, w_down, ln2_g, ln2_b):
    B, S, D = x.shape
    tm = MIX_FFN_TM
    tokens = B * S
    n = tokens // tm
    flat = lambda a: a.reshape(tokens, a.shape[-1])
    merge_tile = lambda t: (jnp.minimum(t, n - 1), 0)
    ffn_tile = lambda t: (jnp.maximum(t - 1, 0), 0)
    const = lambda t: (0, 0)
    resident = lambda shape: pl.BlockSpec(shape, const, pipeline_mode=pl.Buffered(1))
    out = pl.pallas_call(
        _mix_ffn_kernel, grid=(n + 1,),
        in_specs=[
            pl.BlockSpec((tm, ATTN_WIDTH), merge_tile),
            pl.BlockSpec((tm, GLA_WIDTH), merge_tile),
            pl.BlockSpec((tm, GLA_WIDTH), merge_tile),
            pl.BlockSpec((tm, GLA_WIDTH), merge_tile),
            pl.BlockSpec((tm, D), merge_tile),
            resident((ATTN_WIDTH + GLA_WIDTH, D)),
            pl.BlockSpec((1, GLA_DV), const),
            pl.BlockSpec((1, D), const),
            pl.BlockSpec((1, D), const),
            resident((D, D_FF)), resident((D, D_FF)), resident((D_FF, D)),
            pl.BlockSpec((1, D), const),
            pl.BlockSpec((1, D), const),
        ],
        out_specs=pl.BlockSpec((tm, D), ffn_tile),
        out_shape=jax.ShapeDtypeStruct((tokens, D), F32),
        scratch_shapes=[pltpu.VMEM((2, tm, D), F32), pltpu.VMEM((2, tm, D), BF16)],
        compiler_params=_cparams(("arbitrary",)),
    )(flat(attn), flat(o_f), flat(o_b), flat(go), flat(x), w_out, gng, ln1_g, ln1_b,
      w_gate, w_up, w_down, ln2_g, ln2_b)
    return out.reshape(B, S, D)


def _rope_tables(seq_len):
    t = jnp.arange(seq_len, dtype=jnp.int32)
    row_id = (t // GRID_W).astype(F32)
    col_id = (t % GRID_W).astype(F32)
    inv_freq = ROPE_THETA ** (-jnp.arange(0, AXIAL_DIM, 2, dtype=F32) / AXIAL_DIM)
    ang_row = row_id[:, None] * inv_freq
    ang_col = col_id[:, None] * inv_freq
    zeros = jnp.zeros_like(ang_row)
    cos = jnp.concatenate([jnp.cos(ang_row)] * 2 + [jnp.cos(ang_col)] * 2, axis=-1)
    sin_hi = jnp.concatenate([zeros, jnp.sin(ang_row), zeros, jnp.sin(ang_col)], axis=-1)
    sin_lo = jnp.concatenate([-jnp.sin(ang_row), zeros, -jnp.sin(ang_col), zeros], axis=-1)
    two = lambda a: jnp.concatenate([a, a], axis=-1)
    return two(cos), two(sin_hi), two(sin_lo)


def kernel(x, w_in, q_norm_g, k_norm_g, gate_up_fwd, gate_bias_fwd, gate_up_bwd, gate_bias_bwd, gla_norm_g,
           w_out, ln1_g, ln1_b, w_ffn_gate, w_ffn_up, w_ffn_down, ln2_g, ln2_b):
    B, S, D = x.shape
    assert D == D_MODEL and S % max(PROJ_TM, ATTN_TQ, ATTN_TK, GLA_CHUNK, MIX_FFN_TM) == 0 and B % GLA_BATCH == 0
    assert w_in.shape[0] == DEPTH
    cos, sin_hi, sin_lo = _rope_tables(S)
    for layer in range(DEPTH):
        w_pad = jnp.pad(w_in[layer], ((0, 0), (0, PROJ_PAD_WIDTH - w_in.shape[-1]))).astype(BF16)
        gup = jnp.zeros((Z_PAD, 2 * GLA_QK_WIDTH), F32)
        gup = gup.at[:GATE_RANK, :GLA_QK_WIDTH].set(gate_up_fwd[layer])
        gup = gup.at[GATE_RANK:2 * GATE_RANK, GLA_QK_WIDTH:].set(gate_up_bwd[layer]).astype(BF16)
        gbias = jnp.concatenate([gate_bias_fwd[layer], gate_bias_bwd[layer]])[None, :]
        qg = jnp.tile(q_norm_g[layer], LANES // HEAD_DIM)[None, :]
        kg = jnp.tile(k_norm_g[layer], LANES // HEAD_DIM)[None, :]

        qa, ka, va, gq, gk, gv, go, laf, lab = _proj_call(x, w_pad, gup, gbias, qg, kg, cos, sin_hi, sin_lo)
        score_bound = (jnp.max(jnp.abs(q_norm_g[layer])) * jnp.max(jnp.abs(k_norm_g[layer]))
                       * SCORE_BOUND_SCALE).reshape(1).astype(F32)
        attn = _attn_call(score_bound, qa, ka, va)
        o_f, o_b = _gla_call(gq, gk, gv, laf, lab)
        x = _mix_ffn_call(attn, o_f, o_b, go, x, w_out[layer].astype(BF16), gla_norm_g[layer][None, :],
                          ln1_g[layer][None, :], ln1_b[layer][None, :],
                          w_ffn_gate[layer].astype(BF16), w_ffn_up[layer].astype(BF16),
                          w_ffn_down[layer].astype(BF16), ln2_g[layer][None, :], ln2_b[layer][None, :])
    return x
```

```python
import jax
import jax.numpy as jnp
from jax import lax
from jax.experimental import pallas as pl
from jax.experimental.pallas import tpu as pltpu

F32 = jnp.float32
BF16 = jnp.bfloat16

D_MODEL = 1024
GRID_W = 64
N_Q_HEADS = 8
N_KV_HEADS = 2
Q_PER_KV = N_Q_HEADS // N_KV_HEADS
HEAD_DIM = 64
AXIAL_DIM = HEAD_DIM // 2
ROPE_HALF = AXIAL_DIM // 2
ROPE_THETA = 10000.0
GLA_HEADS = 4
GLA_DK = 64
GLA_DV = 128
GATE_RANK = 16
GATE_TAU = 16.0
GLA_REF_CHUNK = 16
ATTN_WIDTH = N_Q_HEADS * HEAD_DIM
KV_WIDTH = N_KV_HEADS * HEAD_DIM
GLA_QK_WIDTH = GLA_HEADS * GLA_DK
GLA_WIDTH = GLA_HEADS * GLA_DV
D_FF = 2816
DEPTH = 1
DEEPNORM_ALPHA = (2 * DEPTH) ** 0.25
LN_EPS = 1e-5
RMS_EPS = 1e-6
LOG2_E = 1.4426950408889634

LANES = 128
VMEM_LIMIT_BYTES = 56 * 1024 * 1024

PROJ_TM = 512
ATTN_TQ = 512
ATTN_TK = 256
ATTN_COL = 512
VT_ROWS = HEAD_DIM + 16
SCORE_BOUND_SCALE = LOG2_E * HEAD_DIM ** 0.5 * 1.01
ATTN_UNSHIFTED_MAX_LOG2 = 60.0
GLA_CHUNK = 128
GLA_BATCH = 4
GLA_PHASES = 4
MIX_FFN_TM = 512
FFN_CHUNK = 256

OFF_AQ = 0
OFF_AK = OFF_AQ + ATTN_WIDTH
OFF_AV = OFF_AK + KV_WIDTH
OFF_GQ = OFF_AV + KV_WIDTH
OFF_GK = OFF_GQ + GLA_QK_WIDTH
OFF_GV = OFF_GK + GLA_QK_WIDTH
OFF_GO = OFF_GV + GLA_WIDTH
OFF_Z = OFF_GO + GLA_WIDTH
Z_PAD = LANES
PROJ_PAD_WIDTH = OFF_Z + Z_PAD


def _cparams(semantics):
    return pltpu.CompilerParams(dimension_semantics=semantics, vmem_limit_bytes=VMEM_LIMIT_BYTES)


def _dot(a, b):
    return jnp.dot(a, b, preferred_element_type=F32)


def _dot_nt(a, b):
    return lax.dot_general(a, b, (((1,), (1,)), ((), ())), preferred_element_type=F32)


def _rope128(y, cos, sin_hi, sin_lo):
    return y * cos + pltpu.roll(y, ROPE_HALF, 1) * sin_hi + pltpu.roll(y, LANES - ROPE_HALF, 1) * sin_lo


def _head_pair_inv_rms(blk, lane_lo):
    sq = blk * blk
    ss_lo = jnp.sum(jnp.where(lane_lo, sq, 0.0), axis=-1, keepdims=True)
    ss_hi = jnp.sum(jnp.where(lane_lo, 0.0, sq), axis=-1, keepdims=True)
    inv = 1.0 / HEAD_DIM
    return jnp.where(lane_lo, lax.rsqrt(ss_lo * inv + RMS_EPS), lax.rsqrt(ss_hi * inv + RMS_EPS))


def _proj_kernel(x_ref, w_ref, gup_ref, gbias_ref, qg_ref, kg_ref, cos_ref, shi_ref, slo_ref,
                 qa_ref, ka_ref, va_ref, gq_ref, gk_ref, gv_ref, go_ref, laf_ref, lab_ref):
    x = x_ref[0].astype(BF16)
    tm = x.shape[0]
    lane = lax.broadcasted_iota(jnp.int32, (tm, LANES), 1)
    lane_lo = lane < HEAD_DIM
    cos, shi, slo = cos_ref[...], shi_ref[...], slo_ref[...]

    def proj(off, width):
        return _dot(x, w_ref[:, off:off + width])

    aq = proj(OFF_AQ, ATTN_WIDTH)
    akv = proj(OFF_AK, 2 * KV_WIDTH)
    z = proj(OFF_Z, Z_PAD)

    def q_epilogue(c):
        blk = aq[:, c * LANES:(c + 1) * LANES]
        rot = _rope128(blk * qg_ref[...], cos, shi, slo)
        out_t = (rot * (_head_pair_inv_rms(blk, lane_lo) * (LOG2_E * HEAD_DIM ** -0.5))).T.astype(BF16)
        qa_ref[0, 2 * c] = out_t[:HEAD_DIM]
        qa_ref[0, 2 * c + 1] = out_t[HEAD_DIM:]

    def k_epilogue():
        ak = akv[:, :KV_WIDTH]
        rot = _rope128(ak * kg_ref[...], cos, shi, slo)
        out = (rot * _head_pair_inv_rms(ak, lane_lo)).astype(BF16)
        ka_ref[0, 0] = out[:, :HEAD_DIM]
        ka_ref[0, 1] = out[:, HEAD_DIM:]

    def v_epilogue():
        av = akv[:, KV_WIDTH:]
        ones_col = jnp.where(lane == HEAD_DIM, 1.0, 0.0)
        va_ref[0, 0] = jnp.where(lane_lo, av, ones_col).T.astype(BF16)
        va_ref[0, 1] = jnp.where(lane_lo, pltpu.roll(av, HEAD_DIM, 1), ones_col).T.astype(BF16)

    def gate_epilogue():
        g = _dot(z.astype(BF16), gup_ref[...]) + gbias_ref[...]
        log2_a = (jnp.minimum(g, 0.0) - jnp.log(1.0 + jnp.exp(-jnp.abs(g)))) * (LOG2_E / GATE_TAU)
        laf_ref[0] = log2_a[:, :GLA_QK_WIDTH]
        lab_ref[0] = log2_a[:, GLA_QK_WIDTH:]

    half = GLA_WIDTH // 2
    gq_ref[0] = (proj(OFF_GQ, GLA_QK_WIDTH) * (GLA_DK ** -0.5)).astype(BF16)
    q_epilogue(0)
    gk_ref[0] = proj(OFF_GK, GLA_QK_WIDTH).astype(BF16)
    q_epilogue(1)
    gv_ref[0, :, :half] = proj(OFF_GV, half).astype(BF16)
    q_epilogue(2)
    gv_ref[0, :, half:] = proj(OFF_GV + half, half).astype(BF16)
    q_epilogue(3)
    go_ref[0, :, :half] = proj(OFF_GO, half).astype(BF16)
    k_epilogue()
    v_epilogue()
    go_ref[0, :, half:] = proj(OFF_GO + half, half).astype(BF16)
    gate_epilogue()


def _proj_call(x, w_pad, gup, gbias, qg, kg, cos, shi, slo):
    B, S, D = x.shape
    tm = PROJ_TM
    grid = (S // tm, B)
    tok = lambda s, b: (b, s, 0)
    head = lambda s, b: (b, 0, s, 0)
    head_t = lambda s, b: (b, 0, 0, s)
    const = lambda s, b: (0, 0)
    tab = lambda s, b: (s, 0)
    in_specs = [
        pl.BlockSpec((1, tm, D), tok),
        pl.BlockSpec((D, PROJ_PAD_WIDTH), const),
        pl.BlockSpec((Z_PAD, 2 * GLA_QK_WIDTH), const),
        pl.BlockSpec((1, 2 * GLA_QK_WIDTH), const),
        pl.BlockSpec((1, LANES), const),
        pl.BlockSpec((1, LANES), const),
        pl.BlockSpec((tm, LANES), tab),
        pl.BlockSpec((tm, LANES), tab),
        pl.BlockSpec((tm, LANES), tab),
    ]
    out_shape = [
        jax.ShapeDtypeStruct((B, N_Q_HEADS, HEAD_DIM, S), BF16),
        jax.ShapeDtypeStruct((B, N_KV_HEADS, S, HEAD_DIM), BF16),
        jax.ShapeDtypeStruct((B, N_KV_HEADS, LANES, S), BF16),
        jax.ShapeDtypeStruct((B, S, GLA_QK_WIDTH), BF16),
        jax.ShapeDtypeStruct((B, S, GLA_QK_WIDTH), BF16),
        jax.ShapeDtypeStruct((B, S, GLA_WIDTH), BF16),
        jax.ShapeDtypeStruct((B, S, GLA_WIDTH), BF16),
        jax.ShapeDtypeStruct((B, S, GLA_QK_WIDTH), F32),
        jax.ShapeDtypeStruct((B, S, GLA_QK_WIDTH), F32),
    ]
    out_specs = [
        pl.BlockSpec((1, N_Q_HEADS, HEAD_DIM, tm), head_t),
        pl.BlockSpec((1, N_KV_HEADS, tm, HEAD_DIM), head),
        pl.BlockSpec((1, N_KV_HEADS, LANES, tm), head_t),
        pl.BlockSpec((1, tm, GLA_QK_WIDTH), tok),
        pl.BlockSpec((1, tm, GLA_QK_WIDTH), tok),
        pl.BlockSpec((1, tm, GLA_WIDTH), tok),
        pl.BlockSpec((1, tm, GLA_WIDTH), tok),
        pl.BlockSpec((1, tm, GLA_QK_WIDTH), tok),
        pl.BlockSpec((1, tm, GLA_QK_WIDTH), tok),
    ]
    return pl.pallas_call(
        _proj_kernel, grid=grid, in_specs=in_specs, out_specs=out_specs, out_shape=out_shape,
        compiler_params=_cparams(("arbitrary", "arbitrary")),
    )(x, w_pad, gup, gbias, qg, kg, cos, shi, slo)


def _attn_kernel(bound_ref, qt_ref, k_ref, vt_ref, o_ref):
    tq = qt_ref.shape[3]
    n_keys = k_ref.shape[2]
    tk = ATTN_TK
    q_t = jnp.concatenate([qt_ref[0, r] for r in range(Q_PER_KV)], axis=1)
    unshifted_is_safe = bound_ref[0] < ATTN_UNSHIFTED_MAX_LOG2

    def finish(acc):
        o_t = acc[:HEAD_DIM] / acc[HEAD_DIM:HEAD_DIM + 1]
        for r in range(Q_PER_KV):
            o_ref[0, :, r * HEAD_DIM:(r + 1) * HEAD_DIM] = o_t[:, r * tq:(r + 1) * tq].T.astype(o_ref.dtype)

    @pl.when(unshifted_is_safe)
    def _():
        n = n_keys // tk
        n_col = q_t.shape[1] // ATTN_COL
        col = lambda c: slice(c * ATTN_COL, (c + 1) * ATTN_COL)
        scores = lambda j, c: _dot(k_ref[0, 0, j * tk:(j + 1) * tk, :], q_t[:, col(c)])
        acc = [None] * n_col
        s_cur = [scores(0, c) for c in range(n_col)]
        for j in range(n):
            s_nxt = []
            for c in range(n_col):
                pv = _dot(vt_ref[0, 0, :VT_ROWS, j * tk:(j + 1) * tk], jnp.exp2(s_cur[c]).astype(BF16))
                acc[c] = pv if acc[c] is None else acc[c] + pv
                if j + 1 < n:
                    s_nxt.append(scores(j + 1, c))
            s_cur = s_nxt
        finish(jnp.concatenate(acc, axis=1))

    @pl.when(jnp.logical_not(unshifted_is_safe))
    def _():
        m = None
        acc = None
        for j in range(n_keys // tk):
            s_t = _dot(k_ref[0, 0, j * tk:(j + 1) * tk, :], q_t)
            m_blk = jnp.max(s_t, axis=0, keepdims=True)
            m_new = m_blk if m is None else jnp.maximum(m, m_blk)
            pv = _dot(vt_ref[0, 0, :VT_ROWS, j * tk:(j + 1) * tk], jnp.exp2(s_t - m_new).astype(BF16))
            acc = pv if acc is None else jnp.exp2(m - m_new) * acc + pv
            m = m_new
        finish(acc)


def _attn_call(score_bound, qa_t, ka, va_t):
    B, _, S, _ = ka.shape
    tq = ATTN_TQ
    grid = (B, N_KV_HEADS, S // tq)
    return pl.pallas_call(
        _attn_kernel, grid=grid,
        in_specs=[
            pl.BlockSpec(memory_space=pltpu.SMEM),
            pl.BlockSpec((1, Q_PER_KV, HEAD_DIM, tq), lambda b, g, qi: (b, g, 0, qi)),
            pl.BlockSpec((1, 1, S, HEAD_DIM), lambda b, g, qi: (b, g, 0, 0)),
            pl.BlockSpec((1, 1, LANES, S), lambda b, g, qi: (b, g, 0, 0)),
        ],
        out_specs=pl.BlockSpec((1, tq, Q_PER_KV * HEAD_DIM), lambda b, g, qi: (b, qi, g)),
        out_shape=jax.ShapeDtypeStruct((B, S, ATTN_WIDTH), BF16),
        compiler_params=_cparams(("arbitrary", "arbitrary", "arbitrary")),
    )(score_bound, qa_t, ka, va_t)


def _block_row(a, blk, row):
    n, w = a.shape
    a3 = a.reshape(n // blk, blk, w)
    return jnp.broadcast_to(a3[:, row:row + 1, :], a3.shape).reshape(n, w)


def _split3_bf16(a):
    hi = a.astype(BF16)
    r1 = a - hi.astype(F32)
    mid = r1.astype(BF16)
    lo = (r1 - mid.astype(F32)).astype(BF16)
    return hi, mid, lo


def _gla_direction(bi, q_ref, k_ref, v_ref, la_ref, o_ref, state_ref, forward):
    C = q_ref.shape[1]
    row = lax.broadcasted_iota(jnp.int32, (C, C), 0)
    col = lax.broadcasted_iota(jnp.int32, (C, C), 1)
    tri = ((col <= row) if forward else (col >= row)).astype(BF16)
    la = la_ref[bi]
    cum_all = sum(_dot(tri, part) for part in _split3_bf16(la))
    excl_all = cum_all - la
    xr = row ^ col
    valid = (col <= row) if forward else (col > row)
    lane = lax.broadcasted_iota(jnp.int32, (C, LANES), 1)
    lane_lo = lane < GLA_DK
    edge = C - 1 if forward else 0
    srow = lax.broadcasted_iota(jnp.int32, (LANES, 2 * GLA_DV), 0)
    scol = lax.broadcasted_iota(jnp.int32, (LANES, 2 * GLA_DV), 1)
    on_diag = (srow < GLA_DK) == (scol < GLA_DV)
    pairs = range(GLA_HEADS // 2)
    yield

    factors = []
    for pair in pairs:
        sl = slice(pair * LANES, (pair + 1) * LANES)
        q, k = q_ref[bi, :, sl].astype(F32), k_ref[bi, :, sl].astype(F32)
        cum, excl = cum_all[:, sl], excl_all[:, sl]
        total = cum[edge:edge + 1, :]

        base_row = 0 if forward else GLA_REF_CHUNK - 1
        loc = cum - _block_row(excl, GLA_REF_CHUNK, base_row)
        q_lv = [q * jnp.exp2(loc)]
        k_lv = [(k * jnp.exp2(-loc)).astype(BF16)]
        blk = 2 * GLA_REF_CHUNK
        while blk <= C:
            mid = _block_row(excl if forward else cum, blk, blk // 2)
            e = jnp.exp2(-jnp.abs(cum - mid))
            q_lv.append(q * e)
            k_lv.append((k * e).astype(BF16))
            blk *= 2
        q_in = (q * jnp.exp2(cum)).astype(BF16)
        k_out = k * jnp.exp2(total - cum)
        k_out_t = k_out.T.astype(BF16)
        decay_t = jnp.exp2(total).T
        factors.append((q_lv, k_lv, q_in, k_out_t, decay_t))
    yield

    def stack_heads(a):
        return jnp.concatenate([jnp.where(lane_lo, a, 0.0), jnp.where(lane_lo, 0.0, a)], axis=0).astype(BF16)

    xr2 = jnp.concatenate([xr, xr], axis=0)
    valid2 = jnp.concatenate([valid, valid], axis=0)
    scores = []
    for pair in pairs:
        q_lv, k_lv = factors[pair][:2]
        scores2 = _dot_nt(stack_heads(q_lv[-1]), k_lv[-1])
        bound = C // 2
        for lv in range(len(q_lv) - 2, -1, -1):
            scores2 = jnp.where(xr2 < bound, _dot_nt(stack_heads(q_lv[lv]), k_lv[lv]), scores2)
            bound //= 2
        scores.append(jnp.where(valid2, scores2, 0.0).astype(BF16))
    yield

    for pair in pairs:
        q_in, k_out_t, decay_t = factors[pair][2:]
        v2 = v_ref[bi, :, 2 * pair * GLA_DV:2 * (pair + 1) * GLA_DV]
        state = state_ref[bi, pair]
        inter = _dot(q_in, state.astype(BF16))
        for hh in range(2):
            h = 2 * pair + hh
            cols = slice(hh * GLA_DV, (hh + 1) * GLA_DV)
            o = _dot(scores[pair][hh * C:(hh + 1) * C], v2[:, cols]) + inter[:, cols]
            o_ref[bi, :, h * GLA_DV:(h + 1) * GLA_DV] = o.astype(o_ref.dtype)
        state_ref[bi, pair] = jnp.where(on_diag, decay_t * state + _dot(k_out_t, v2), 0.0)
    yield


def _gla_kernel(qf_ref, kf_ref, vf_ref, laf_ref, qb_ref, kb_ref, vb_ref, lab_ref,
                of_ref, ob_ref, sf_ref, sb_ref):
    @pl.when(pl.program_id(1) == 0)
    def _():
        sf_ref[...] = jnp.zeros(sf_ref.shape, F32)
        sb_ref[...] = jnp.zeros(sb_ref.shape, F32)

    chains = []
    for bi in range(GLA_BATCH):
        chains.append(_gla_direction(bi, qf_ref, kf_ref, vf_ref, laf_ref, of_ref, sf_ref, True))
        chains.append(_gla_direction(bi, qb_ref, kb_ref, vb_ref, lab_ref, ob_ref, sb_ref, False))
    for _ in range(GLA_PHASES):
        for chain in chains:
            next(chain)


def _gla_call(gq, gk, gv, laf, lab):
    B, S, _ = gq.shape
    C, nb = GLA_CHUNK, GLA_BATCH
    n = S // C
    fwd = lambda b, c: (b, c, 0)
    bwd = lambda b, c: (b, n - 1 - c, 0)
    qk = lambda im: pl.BlockSpec((nb, C, GLA_QK_WIDTH), im)
    vv = lambda im: pl.BlockSpec((nb, C, GLA_WIDTH), im)
    state = pltpu.VMEM((nb, GLA_HEADS // 2, LANES, 2 * GLA_DV), F32)
    return pl.pallas_call(
        _gla_kernel, grid=(B // nb, n),
        in_specs=[qk(fwd), qk(fwd), vv(fwd), qk(fwd), qk(bwd), qk(bwd), vv(bwd), qk(bwd)],
        out_specs=[vv(fwd), vv(bwd)],
        out_shape=[jax.ShapeDtypeStruct((B, S, GLA_WIDTH), BF16)] * 2,
        scratch_shapes=[state, state],
        compiler_params=_cparams(("arbitrary", "arbitrary")),
    )(gq, gk, gv, laf, gq, gk, gv, lab)


def _layer_norm(y, g, b):
    mu = jnp.mean(y, axis=-1, keepdims=True)
    d = y - mu
    var = jnp.mean(d * d, axis=-1, keepdims=True)
    return d * lax.rsqrt(var + LN_EPS) * g + b


def _silu(g):
    return g * (1.0 / (1.0 + jnp.exp(-g)))


def _mix_ffn_kernel(attn_ref, of_ref, ob_ref, go_ref, x_ref, wo_ref, gng_ref, ln1g_ref, ln1b_ref,
                    wg_ref, wu_ref, wd_ref, ln2g_ref, ln2b_ref, out_ref, x1_sc, x1b_sc):
    t = pl.program_id(0)
    cur, prev = t % 2, (t + 1) % 2

    @pl.when(t == 0)
    def _():
        x1_sc[1] = jnp.zeros(x1_sc.shape[1:], F32)
        x1b_sc[1] = jnp.zeros(x1b_sc.shape[1:], BF16)

    hidden = []

    def ffn_chunk(c):
        sl = slice(c * FFN_CHUNK, (c + 1) * FFN_CHUNK)
        xb = x1b_sc[prev]
        hidden.append((_silu(_dot(xb, wg_ref[:, sl])) * _dot(xb, wu_ref[:, sl])).astype(BF16))

    def gate_head(h):
        sl = slice(h * GLA_DV, (h + 1) * GLA_DV)
        o = of_ref[:, sl].astype(F32) + ob_ref[:, sl].astype(F32)
        inv = lax.rsqrt(jnp.mean(o * o, axis=-1, keepdims=True) + RMS_EPS)
        return ((o * inv * gng_ref[...]) * _silu(go_ref[:, sl].astype(F32))).astype(BF16)

    n_chunks = D_FF // FFN_CHUNK
    gated = []
    for c in range(GLA_HEADS):
        ffn_chunk(c)
        gated.append(gate_head(c))
    ffn_chunk(GLA_HEADS)
    merged = jnp.concatenate([attn_ref[...]] + gated, axis=-1)
    mixed = _dot(merged, wo_ref[...])
    ffn_chunk(GLA_HEADS + 1)
    ffn_chunk(GLA_HEADS + 2)
    x1 = _layer_norm(DEEPNORM_ALPHA * x_ref[...] + mixed, ln1g_ref[...], ln1b_ref[...])
    x1_sc[cur] = x1
    x1b_sc[cur] = x1.astype(BF16)
    for c in range(GLA_HEADS + 3, n_chunks):
        ffn_chunk(c)
    ffn = _dot(jnp.concatenate(hidden, axis=-1), wd_ref[...])
    out_ref[...] = _layer_norm(DEEPNORM_ALPHA * x1_sc[prev] + ffn, ln2g_ref[...], ln2b_ref[...])


def _mix_ffn_call(attn, o_f, o_b, go, x, w_out, gng, ln1_g, ln1_b, w_gate, w_up, w_down, ln2_g, ln2_b):
    B, S, D = x.shape
    tm = MIX_FFN_TM
    tokens = B * S
    n = tokens // tm
    flat = lambda a: a.reshape(tokens, a.shape[-1])
    merge_tile = lambda t: (jnp.minimum(t, n - 1), 0)
    ffn_tile = lambda t: (jnp.maximum(t - 1, 0), 0)
    const = lambda t: (0, 0)
    resident = lambda shape: pl.BlockSpec(shape, const, pipeline_mode=pl.Buffered(1))
    out = pl.pallas_call(
        _mix_ffn_kernel, grid=(n + 1,),
        in_specs=[
            pl.BlockSpec((tm, ATTN_WIDTH), merge_tile),
            pl.BlockSpec((tm, GLA_WIDTH), merge_tile),
            pl.BlockSpec((tm, GLA_WIDTH), merge_tile),
            pl.BlockSpec((tm, GLA_WIDTH), merge_tile),
            pl.BlockSpec((tm, D), merge_tile),
            resident((ATTN_WIDTH + GLA_WIDTH, D)),
            pl.BlockSpec((1, GLA_DV), const),
            pl.BlockSpec((1, D), const),
            pl.BlockSpec((1, D), const),
            resident((D, D_FF)), resident((D, D_FF)), resident((D_FF, D)),
            pl.BlockSpec((1, D), const),
            pl.BlockSpec((1, D), const),
        ],
        out_specs=pl.BlockSpec((tm, D), ffn_tile),
        out_shape=jax.ShapeDtypeStruct((tokens, D), F32),
        scratch_shapes=[pltpu.VMEM((2, tm, D), F32), pltpu.VMEM((2, tm, D), BF16)],
        compiler_params=_cparams(("arbitrary",)),
    )(flat(attn), flat(o_f), flat(o_b), flat(go), flat(x), w_out, gng, ln1_g, ln1_b,
      w_gate, w_up, w_down, ln2_g, ln2_b)
    return out.reshape(B, S, D)


def _rope_tables(seq_len):
    t = jnp.arange(seq_len, dtype=jnp.int32)
    row_id = (t // GRID_W).astype(F32)
    col_id = (t % GRID_W).astype(F32)
    inv_freq = ROPE_THETA ** (-jnp.arange(0, AXIAL_DIM, 2, dtype=F32) / AXIAL_DIM)
    lane = jnp.arange(LANES, dtype=jnp.int32)
    d = lane % HEAD_DIM
    is_col = (d // AXIAL_DIM) == 1
    upper = ((d % AXIAL_DIM) // ROPE_HALF) == 1
    ang = jnp.where(is_col, col_id[:, None], row_id[:, None]) * jnp.tile(inv_freq, LANES // ROPE_HALF)
    sin = jnp.sin(ang)
    return jnp.cos(ang), jnp.where(upper, sin, 0.0), jnp.where(upper, 0.0, -sin)


def kernel(x, w_in, q_norm_g, k_norm_g, gate_up_fwd, gate_bias_fwd, gate_up_bwd, gate_bias_bwd, gla_norm_g,
           w_out, ln1_g, ln1_b, w_ffn_gate, w_ffn_up, w_ffn_down, ln2_g, ln2_b):
    B, S, D = x.shape
    assert D == D_MODEL and S % max(PROJ_TM, ATTN_TQ, ATTN_TK, GLA_CHUNK, MIX_FFN_TM) == 0 and B % GLA_BATCH == 0
    assert w_in.shape[0] == DEPTH
    cos, sin_hi, sin_lo = _rope_tables(S)
    for layer in range(DEPTH):
        w_pad = jnp.pad(w_in[layer], ((0, 0), (0, PROJ_PAD_WIDTH - w_in.shape[-1]))).astype(BF16)
        gup = jnp.zeros((Z_PAD, 2 * GLA_QK_WIDTH), F32)
        gup = gup.at[:GATE_RANK, :GLA_QK_WIDTH].set(gate_up_fwd[layer])
        gup = gup.at[GATE_RANK:2 * GATE_RANK, GLA_QK_WIDTH:].set(gate_up_bwd[layer]).astype(BF16)
        gbias = jnp.concatenate([gate_bias_fwd[layer], gate_bias_bwd[layer]])[None, :]
        qg = jnp.tile(q_norm_g[layer], LANES // HEAD_DIM)[None, :]
        kg = jnp.tile(k_norm_g[layer], LANES // HEAD_DIM)[None, :]

        qa, ka, va, gq, gk, gv, go, laf, lab = _proj_call(x, w_pad, gup, gbias, qg, kg, cos, sin_hi, sin_lo)
        score_bound = (jnp.max(jnp.abs(q_norm_g[layer])) * jnp.max(jnp.abs(k_norm_g[layer]))
                       * SCORE_BOUND_SCALE).reshape(1).astype(F32)
        attn = _attn_call(score_bound, qa, ka, va)
        o_f, o_b = _gla_call(gq, gk, gv, laf, lab)
        x = _mix_ffn_call(attn, o_f, o_b, go, x, w_out[layer].astype(BF16), gla_norm_g[layer][None, :],
                          ln1_g[layer][None, :], ln1_b[layer][None, :],
                          w_ffn_gate[layer].astype(BF16), w_ffn_up[layer].astype(BF16),
                          w_ffn_down[layer].astype(BF16), ln2_g[layer][None, :], ln2_b[layer][None, :])
    return x
```

```python
import jax
import jax.numpy as jnp
from jax import lax
from jax.experimental import pallas as pl
from jax.experimental.pallas import tpu as pltpu

F32 = jnp.float32
BF16 = jnp.bfloat16

D_MODEL = 1024
GRID_W = 64
N_Q_HEADS = 8
N_KV_HEADS = 2
Q_PER_KV = N_Q_HEADS // N_KV_HEADS
HEAD_DIM = 64
AXIAL_DIM = HEAD_DIM // 2
ROPE_HALF = AXIAL_DIM // 2
ROPE_THETA = 10000.0
GLA_HEADS = 4
GLA_DK = 64
GLA_DV = 128
GATE_RANK = 16
GATE_TAU = 16.0
GLA_REF_CHUNK = 16
ATTN_WIDTH = N_Q_HEADS * HEAD_DIM
KV_WIDTH = N_KV_HEADS * HEAD_DIM
GLA_QK_WIDTH = GLA_HEADS * GLA_DK
GLA_WIDTH = GLA_HEADS * GLA_DV
D_FF = 2816
DEPTH = 1
DEEPNORM_ALPHA = (2 * DEPTH) ** 0.25
LN_EPS = 1e-5
RMS_EPS = 1e-6
LOG2_E = 1.4426950408889634

LANES = 128
SUBLANES = 8
VMEM_LIMIT_BYTES = 56 * 1024 * 1024

PROJ_TM = 512
ATTN_TQ = 512
ATTN_TK = 256
ATTN_COL = 512
VT_ROWS = HEAD_DIM + 16
SCORE_BOUND_SCALE = LOG2_E * HEAD_DIM ** 0.5 * 1.01
ATTN_UNSHIFTED_MAX_LOG2 = 60.0
GLA_CHUNK = 128
GLA_BATCH = 4
GLA_PHASES = 4
MIX_FFN_TM = 512
FFN_CHUNK = 256

OFF_AQ = 0
OFF_AK = OFF_AQ + ATTN_WIDTH
OFF_AV = OFF_AK + KV_WIDTH
OFF_GQ = OFF_AV + KV_WIDTH
OFF_GK = OFF_GQ + GLA_QK_WIDTH
OFF_GV = OFF_GK + GLA_QK_WIDTH
OFF_GO = OFF_GV + GLA_WIDTH
OFF_Z = OFF_GO + GLA_WIDTH
Z_PAD = LANES
PROJ_PAD_WIDTH = OFF_Z + Z_PAD


def _cparams(semantics):
    return pltpu.CompilerParams(dimension_semantics=semantics, vmem_limit_bytes=VMEM_LIMIT_BYTES)


def _dot(a, b):
    return jnp.dot(a, b, preferred_element_type=F32)


def _dot_nt(a, b):
    return lax.dot_general(a, b, (((1,), (1,)), ((), ())), preferred_element_type=F32)


def _rope128(y, cos, sin_hi, sin_lo):
    return y * cos + pltpu.roll(y, ROPE_HALF, 1) * sin_hi + pltpu.roll(y, LANES - ROPE_HALF, 1) * sin_lo


def _head_pair_inv_rms(blk, lane_lo):
    sq = blk * blk
    ss_lo = jnp.sum(jnp.where(lane_lo, sq, 0.0), axis=-1, keepdims=True)
    ss_hi = jnp.sum(jnp.where(lane_lo, 0.0, sq), axis=-1, keepdims=True)
    inv = 1.0 / HEAD_DIM
    return jnp.where(lane_lo, lax.rsqrt(ss_lo * inv + RMS_EPS), lax.rsqrt(ss_hi * inv + RMS_EPS))


def _proj_kernel(x_ref, w_ref, gup_ref, gbias_ref, qg_ref, kg_ref, cos_ref, shi_ref, slo_ref,
                 qa_ref, ka_ref, va_ref, gq_ref, gk_ref, gv_ref, go_ref, laf_ref, lab_ref,
                 aq_sc, akv_sc, z_sc):
    t = pl.program_id(0)
    cur, prev = t % 2, (t + 1) % 2

    @pl.when(t == 0)
    def _():
        aq_sc[1] = jnp.zeros(aq_sc.shape[1:], F32)
        akv_sc[1] = jnp.zeros(akv_sc.shape[1:], F32)
        z_sc[1] = jnp.zeros(z_sc.shape[1:], F32)

    x = x_ref[0].astype(BF16)
    tm = x.shape[0]
    lane = lax.broadcasted_iota(jnp.int32, (tm, LANES), 1)
    lane_lo = lane < HEAD_DIM
    cos, shi, slo = cos_ref[...], shi_ref[...], slo_ref[...]

    def proj(off, width):
        return _dot(x, w_ref[:, off:off + width])

    def q_epilogue(c):
        blk = aq_sc[prev, :, c * LANES:(c + 1) * LANES]
        rot = _rope128(blk * qg_ref[...], cos, shi, slo)
        out_t = (rot * (_head_pair_inv_rms(blk, lane_lo) * (LOG2_E * HEAD_DIM ** -0.5))).T.astype(BF16)
        qa_ref[0, 2 * c] = out_t[:HEAD_DIM]
        qa_ref[0, 2 * c + 1] = out_t[HEAD_DIM:]

    def k_epilogue():
        ak = akv_sc[prev, :, :KV_WIDTH]
        rot = _rope128(ak * kg_ref[...], cos, shi, slo)
        out = (rot * _head_pair_inv_rms(ak, lane_lo)).astype(BF16)
        ka_ref[0, 0] = out[:, :HEAD_DIM]
        ka_ref[0, 1] = out[:, HEAD_DIM:]

    def v_epilogue():
        av = akv_sc[prev, :, KV_WIDTH:]
        ones_col = jnp.where(lane == HEAD_DIM, 1.0, 0.0)
        va_ref[0, 0] = jnp.where(lane_lo, av, ones_col).T.astype(BF16)
        va_ref[0, 1] = jnp.where(lane_lo, pltpu.roll(av, HEAD_DIM, 1), ones_col).T.astype(BF16)

    def gate_epilogue():
        g = _dot(z_sc[prev].astype(BF16), gup_ref[...]) + gbias_ref[...]
        log2_a = (jnp.minimum(g, 0.0) - jnp.log(1.0 + jnp.exp(-jnp.abs(g)))) * (LOG2_E / GATE_TAU)
        laf_ref[0] = log2_a[:, :GLA_QK_WIDTH]
        lab_ref[0] = log2_a[:, GLA_QK_WIDTH:]

    half = GLA_WIDTH // 2
    aq = proj(OFF_AQ, ATTN_WIDTH)
    q_epilogue(0)
    akv = proj(OFF_AK, 2 * KV_WIDTH)
    q_epilogue(1)
    gq_ref[0] = (proj(OFF_GQ, GLA_QK_WIDTH) * (GLA_DK ** -0.5)).astype(BF16)
    q_epilogue(2)
    gk_ref[0] = proj(OFF_GK, GLA_QK_WIDTH).astype(BF16)
    q_epilogue(3)
    gv_ref[0, :, :half] = proj(OFF_GV, half).astype(BF16)
    k_epilogue()
    gv_ref[0, :, half:] = proj(OFF_GV + half, half).astype(BF16)
    v_epilogue()
    go_ref[0, :, :half] = proj(OFF_GO, half).astype(BF16)
    z = proj(OFF_Z, Z_PAD)
    gate_epilogue()
    go_ref[0, :, half:] = proj(OFF_GO + half, half).astype(BF16)
    aq_sc[cur] = aq
    akv_sc[cur] = akv
    z_sc[cur] = z


def _proj_call(x, w_pad, gup, gbias, qg, kg, cos, shi, slo):
    B, S, D = x.shape
    tm = PROJ_TM
    n_s = S // tm
    n = n_s * B
    mm = lambda t: jnp.minimum(t, n - 1)
    ep = lambda t: jnp.maximum(t - 1, 0)
    tok = lambda t: (mm(t) % B, mm(t) // B, 0)
    tok_ep = lambda t: (ep(t) % B, ep(t) // B, 0)
    head = lambda t: (ep(t) % B, 0, ep(t) // B, 0)
    head_t = lambda t: (ep(t) % B, 0, 0, ep(t) // B)
    const = lambda t: (0, 0)
    tab = lambda t: (ep(t) // B, 0)
    in_specs = [
        pl.BlockSpec((1, tm, D), tok),
        pl.BlockSpec((D, PROJ_PAD_WIDTH), const),
        pl.BlockSpec((Z_PAD, 2 * GLA_QK_WIDTH), const),
        pl.BlockSpec((1, 2 * GLA_QK_WIDTH), const),
        pl.BlockSpec((1, LANES), const),
        pl.BlockSpec((1, LANES), const),
        pl.BlockSpec((tm, LANES), tab),
        pl.BlockSpec((tm, LANES), tab),
        pl.BlockSpec((tm, LANES), tab),
    ]
    out_shape = [
        jax.ShapeDtypeStruct((B, N_Q_HEADS, HEAD_DIM, S), BF16),
        jax.ShapeDtypeStruct((B, N_KV_HEADS, S, HEAD_DIM), BF16),
        jax.ShapeDtypeStruct((B, N_KV_HEADS, LANES, S), BF16),
        jax.ShapeDtypeStruct((B, S, GLA_QK_WIDTH), BF16),
        jax.ShapeDtypeStruct((B, S, GLA_QK_WIDTH), BF16),
        jax.ShapeDtypeStruct((B, S, GLA_WIDTH), BF16),
        jax.ShapeDtypeStruct((B, S, GLA_WIDTH), BF16),
        jax.ShapeDtypeStruct((B, S, GLA_QK_WIDTH), F32),
        jax.ShapeDtypeStruct((B, S, GLA_QK_WIDTH), F32),
    ]
    out_specs = [
        pl.BlockSpec((1, N_Q_HEADS, HEAD_DIM, tm), head_t),
        pl.BlockSpec((1, N_KV_HEADS, tm, HEAD_DIM), head),
        pl.BlockSpec((1, N_KV_HEADS, LANES, tm), head_t),
        pl.BlockSpec((1, tm, GLA_QK_WIDTH), tok),
        pl.BlockSpec((1, tm, GLA_QK_WIDTH), tok),
        pl.BlockSpec((1, tm, GLA_WIDTH), tok),
        pl.BlockSpec((1, tm, GLA_WIDTH), tok),
        pl.BlockSpec((1, tm, GLA_QK_WIDTH), tok_ep),
        pl.BlockSpec((1, tm, GLA_QK_WIDTH), tok_ep),
    ]
    scratch = [pltpu.VMEM((2, tm, ATTN_WIDTH), F32), pltpu.VMEM((2, tm, 2 * KV_WIDTH), F32),
               pltpu.VMEM((2, tm, Z_PAD), F32)]
    return pl.pallas_call(
        _proj_kernel, grid=(n + 1,), in_specs=in_specs, out_specs=out_specs, out_shape=out_shape,
        scratch_shapes=scratch, compiler_params=_cparams(("arbitrary",)),
    )(x, w_pad, gup, gbias, qg, kg, cos, shi, slo)


def _attn_kernel(bound_ref, qt_ref, k_ref, vt_ref, o_ref):
    tq = qt_ref.shape[3]
    n_keys = k_ref.shape[2]
    tk = ATTN_TK
    q_t = jnp.concatenate([qt_ref[0, r] for r in range(Q_PER_KV)], axis=1)
    unshifted_is_safe = bound_ref[0] < ATTN_UNSHIFTED_MAX_LOG2

    def finish(acc, den_row):
        o_t = acc[:HEAD_DIM] / den_row
        for r in range(Q_PER_KV):
            o_ref[0, :, r * HEAD_DIM:(r + 1) * HEAD_DIM] = o_t[:, r * tq:(r + 1) * tq].T.astype(o_ref.dtype)

    @pl.when(unshifted_is_safe)
    def _():
        n = n_keys // tk
        n_col = q_t.shape[1] // ATTN_COL
        col = lambda c: slice(c * ATTN_COL, (c + 1) * ATTN_COL)
        scores = lambda j, c: _dot(k_ref[0, 0, j * tk:(j + 1) * tk, :], q_t[:, col(c)])
        acc = [None] * n_col
        den = [None] * n_col
        s_cur = [scores(0, c) for c in range(n_col)]
        for j in range(n):
            s_nxt = []
            for c in range(n_col):
                p = jnp.exp2(s_cur[c])
                pv = _dot(vt_ref[0, 0, :HEAD_DIM, j * tk:(j + 1) * tk], p.astype(BF16))
                acc[c] = pv if acc[c] is None else acc[c] + pv
                part = jnp.sum(p.reshape(tk // SUBLANES, SUBLANES, ATTN_COL), axis=0)
                den[c] = part if den[c] is None else den[c] + part
                if j + 1 < n:
                    s_nxt.append(scores(j + 1, c))
            s_cur = s_nxt
        den_row = jnp.sum(jnp.concatenate(den, axis=1), axis=0, keepdims=True)
        finish(jnp.concatenate(acc, axis=1), den_row)

    @pl.when(jnp.logical_not(unshifted_is_safe))
    def _():
        m = None
        acc = None
        for j in range(n_keys // tk):
            s_t = _dot(k_ref[0, 0, j * tk:(j + 1) * tk, :], q_t)
            m_blk = jnp.max(s_t, axis=0, keepdims=True)
            m_new = m_blk if m is None else jnp.maximum(m, m_blk)
            pv = _dot(vt_ref[0, 0, :VT_ROWS, j * tk:(j + 1) * tk], jnp.exp2(s_t - m_new).astype(BF16))
            acc = pv if acc is None else jnp.exp2(m - m_new) * acc + pv
            m = m_new
        finish(acc, acc[HEAD_DIM:HEAD_DIM + 1])


def _attn_call(score_bound, qa_t, ka, va_t):
    B, _, S, _ = ka.shape
    tq = ATTN_TQ
    grid = (B, N_KV_HEADS, S // tq)
    return pl.pallas_call(
        _attn_kernel, grid=grid,
        in_specs=[
            pl.BlockSpec(memory_space=pltpu.SMEM),
            pl.BlockSpec((1, Q_PER_KV, HEAD_DIM, tq), lambda b, g, qi: (b, g, 0, qi)),
            pl.BlockSpec((1, 1, S, HEAD_DIM), lambda b, g, qi: (b, g, 0, 0)),
            pl.BlockSpec((1, 1, LANES, S), lambda b, g, qi: (b, g, 0, 0)),
        ],
        out_specs=pl.BlockSpec((1, tq, Q_PER_KV * HEAD_DIM), lambda b, g, qi: (b, qi, g)),
        out_shape=jax.ShapeDtypeStruct((B, S, ATTN_WIDTH), BF16),
        compiler_params=_cparams(("arbitrary", "arbitrary", "arbitrary")),
    )(score_bound, qa_t, ka, va_t)


def _block_row(a, blk, row):
    n, w = a.shape
    a3 = a.reshape(n // blk, blk, w)
    return jnp.broadcast_to(a3[:, row:row + 1, :], a3.shape).reshape(n, w)


def _split3_bf16(a):
    hi = a.astype(BF16)
    r1 = a - hi.astype(F32)
    mid = r1.astype(BF16)
    lo = (r1 - mid.astype(F32)).astype(BF16)
    return hi, mid, lo


def _gla_direction(bi, q_ref, k_ref, v_ref, la_ref, o_ref, state_ref, forward):
    C = q_ref.shape[1]
    row = lax.broadcasted_iota(jnp.int32, (C, C), 0)
    col = lax.broadcasted_iota(jnp.int32, (C, C), 1)
    tri = ((col <= row) if forward else (col >= row)).astype(BF16)
    la = la_ref[bi]
    cum_all = sum(_dot(tri, part) for part in _split3_bf16(la))
    excl_all = cum_all - la
    xr = row ^ col
    valid = (col <= row) if forward else (col > row)
    lane = lax.broadcasted_iota(jnp.int32, (C, LANES), 1)
    lane_lo = lane < GLA_DK
    edge = C - 1 if forward else 0
    srow = lax.broadcasted_iota(jnp.int32, (LANES, 2 * GLA_DV), 0)
    scol = lax.broadcasted_iota(jnp.int32, (LANES, 2 * GLA_DV), 1)
    on_diag = (srow < GLA_DK) == (scol < GLA_DV)
    pairs = range(GLA_HEADS // 2)
    yield

    factors = []
    for pair in pairs:
        sl = slice(pair * LANES, (pair + 1) * LANES)
        q, k = q_ref[bi, :, sl].astype(F32), k_ref[bi, :, sl].astype(F32)
        cum, excl = cum_all[:, sl], excl_all[:, sl]
        total = cum[edge:edge + 1, :]

        base_row = 0 if forward else GLA_REF_CHUNK - 1
        loc = cum - _block_row(excl, GLA_REF_CHUNK, base_row)
        q_lv = [q * jnp.exp2(loc)]
        k_lv = [(k * jnp.exp2(-loc)).astype(BF16)]
        blk = 2 * GLA_REF_CHUNK
        while blk <= C:
            mid = _block_row(excl if forward else cum, blk, blk // 2)
            e = jnp.exp2(-jnp.abs(cum - mid))
            q_lv.append(q * e)
            k_lv.append((k * e).astype(BF16))
            blk *= 2
        q_in = (q * jnp.exp2(cum)).astype(BF16)
        k_out = k * jnp.exp2(total - cum)
        k_out_t = k_out.T.astype(BF16)
        decay_t = jnp.exp2(total).T
        factors.append((q_lv, k_lv, q_in, k_out_t, decay_t))
    yield

    def stack_heads(a):
        return jnp.concatenate([jnp.where(lane_lo, a, 0.0), jnp.where(lane_lo, 0.0, a)], axis=0).astype(BF16)

    xr2 = jnp.concatenate([xr, xr], axis=0)
    valid2 = jnp.concatenate([valid, valid], axis=0)
    scores = []
    for pair in pairs:
        q_lv, k_lv = factors[pair][:2]
        scores2 = _dot_nt(stack_heads(q_lv[-1]), k_lv[-1])
        bound = C // 2
        for lv in range(len(q_lv) - 2, -1, -1):
            scores2 = jnp.where(xr2 < bound, _dot_nt(stack_heads(q_lv[lv]), k_lv[lv]), scores2)
            bound //= 2
        scores.append(jnp.where(valid2, scores2, 0.0).astype(BF16))
    yield

    for pair in pairs:
        q_in, k_out_t, decay_t = factors[pair][2:]
        v2 = v_ref[bi, :, 2 * pair * GLA_DV:2 * (pair + 1) * GLA_DV]
        state = state_ref[bi, pair]
        inter = _dot(q_in, state.astype(BF16))
        for hh in range(2):
            h = 2 * pair + hh
            cols = slice(hh * GLA_DV, (hh + 1) * GLA_DV)
            o = _dot(scores[pair][hh * C:(hh + 1) * C], v2[:, cols]) + inter[:, cols]
            o_ref[bi, :, h * GLA_DV:(h + 1) * GLA_DV] = o.astype(o_ref.dtype)
        state_ref[bi, pair] = jnp.where(on_diag, decay_t * state + _dot(k_out_t, v2), 0.0)
    yield


def _gla_kernel(qf_ref, kf_ref, vf_ref, laf_ref, qb_ref, kb_ref, vb_ref, lab_ref,
                of_ref, ob_ref, sf_ref, sb_ref):
    @pl.when(pl.program_id(1) == 0)
    def _():
        sf_ref[...] = jnp.zeros(sf_ref.shape, F32)
        sb_ref[...] = jnp.zeros(sb_ref.shape, F32)

    chains = []
    for bi in range(GLA_BATCH):
        chains.append(_gla_direction(bi, qf_ref, kf_ref, vf_ref, laf_ref, of_ref, sf_ref, True))
        chains.append(_gla_direction(bi, qb_ref, kb_ref, vb_ref, lab_ref, ob_ref, sb_ref, False))
    for _ in range(GLA_PHASES):
        for chain in chains:
            next(chain)


def _gla_call(gq, gk, gv, laf, lab):
    B, S, _ = gq.shape
    C, nb = GLA_CHUNK, GLA_BATCH
    n = S // C
    fwd = lambda b, c: (b, c, 0)
    bwd = lambda b, c: (b, n - 1 - c, 0)
    qk = lambda im: pl.BlockSpec((nb, C, GLA_QK_WIDTH), im)
    vv = lambda im: pl.BlockSpec((nb, C, GLA_WIDTH), im)
    state = pltpu.VMEM((nb, GLA_HEADS // 2, LANES, 2 * GLA_DV), F32)
    return pl.pallas_call(
        _gla_kernel, grid=(B // nb, n),
        in_specs=[qk(fwd), qk(fwd), vv(fwd), qk(fwd), qk(bwd), qk(bwd), vv(bwd), qk(bwd)],
        out_specs=[vv(fwd), vv(bwd)],
        out_shape=[jax.ShapeDtypeStruct((B, S, GLA_WIDTH), BF16)] * 2,
        scratch_shapes=[state, state],
        compiler_params=_cparams(("arbitrary", "arbitrary")),
    )(gq, gk, gv, laf, gq, gk, gv, lab)


def _layer_norm(y, g, b):
    mu = jnp.mean(y, axis=-1, keepdims=True)
    d = y - mu
    var = jnp.mean(d * d, axis=-1, keepdims=True)
    return d * lax.rsqrt(var + LN_EPS) * g + b


def _silu(g):
    return g * (1.0 / (1.0 + jnp.exp(-g)))


def _mix_ffn_kernel(attn_ref, of_ref, ob_ref, go_ref, x_ref, wo_ref, gng_ref, ln1g_ref, ln1b_ref,
                    wg_ref, wu_ref, wd_ref, ln2g_ref, ln2b_ref, out_ref, x1_sc, x1b_sc):
    t = pl.program_id(0)
    cur, prev = t % 2, (t + 1) % 2

    @pl.when(t == 0)
    def _():
        x1_sc[1] = jnp.zeros(x1_sc.shape[1:], F32)
        x1b_sc[1] = jnp.zeros(x1b_sc.shape[1:], BF16)

    hidden = []

    def ffn_chunk(c):
        sl = slice(c * FFN_CHUNK, (c + 1) * FFN_CHUNK)
        xb = x1b_sc[prev]
        hidden.append((_silu(_dot(xb, wg_ref[:, sl])) * _dot(xb, wu_ref[:, sl])).astype(BF16))

    def gate_head(h):
        sl = slice(h * GLA_DV, (h + 1) * GLA_DV)
        o = of_ref[:, sl].astype(F32) + ob_ref[:, sl].astype(F32)
        inv = lax.rsqrt(jnp.mean(o * o, axis=-1, keepdims=True) + RMS_EPS)
        return ((o * inv * gng_ref[...]) * _silu(go_ref[:, sl].astype(F32))).astype(BF16)

    n_chunks = D_FF // FFN_CHUNK
    gated = []
    for c in range(GLA_HEADS):
        ffn_chunk(c)
        gated.append(gate_head(c))
    ffn_chunk(GLA_HEADS)
    merged = jnp.concatenate([attn_ref[...]] + gated, axis=-1)
    mixed = _dot(merged, wo_ref[...])
    ffn_chunk(GLA_HEADS + 1)
    ffn_chunk(GLA_HEADS + 2)
    x1 = _layer_norm(DEEPNORM_ALPHA * x_ref[...] + mixed, ln1g_ref[...], ln1b_ref[...])
    x1_sc[cur] = x1
    x1b_sc[cur] = x1.astype(BF16)
    for c in range(GLA_HEADS + 3, n_chunks):
        ffn_chunk(c)
    ffn = _dot(jnp.concatenate(hidden, axis=-1), wd_ref[...])
    out_ref[...] = _layer_norm(DEEPNORM_ALPHA * x1_sc[prev] + ffn, ln2g_ref[...], ln2b_ref[...])


def _mix_ffn_call(attn, o_f, o_b, go, x, w_out, gng, ln1_g, ln1_b, w_gate, w_up, w_down, ln2_g, ln2_b):
    B, S, D = x.shape
    tm = MIX_FFN_TM
    tokens = B * S
    n = tokens // tm
    flat = lambda a: a.reshape(tokens, a.shape[-1])
    merge_tile = lambda t: (jnp.minimum(t, n - 1), 0)
    ffn_tile = lambda t: (jnp.maximum(t - 1, 0), 0)
    const = lambda t: (0, 0)
    resident = lambda shape: pl.BlockSpec(shape, const, pipeline_mode=pl.Buffered(1))
    out = pl.pallas_call(
        _mix_ffn_kernel, grid=(n + 1,),
        in_specs=[
            pl.BlockSpec((tm, ATTN_WIDTH), merge_tile),
            pl.BlockSpec((tm, GLA_WIDTH), merge_tile),
            pl.BlockSpec((tm, GLA_WIDTH), merge_tile),
            pl.BlockSpec((tm, GLA_WIDTH), merge_tile),
            pl.BlockSpec((tm, D), merge_tile),
            resident((ATTN_WIDTH + GLA_WIDTH, D)),
            pl.BlockSpec((1, GLA_DV), const),
            pl.BlockSpec((1, D), const),
            pl.BlockSpec((1, D), const),
            resident((D, D_FF)), resident((D, D_FF)), resident((D_FF, D)),
            pl.BlockSpec((1, D), const),
            pl.BlockSpec((1, D), const),
        ],
        out_specs=pl.BlockSpec((tm, D), ffn_tile),
        out_shape=jax.ShapeDtypeStruct((tokens, D), F32),
        scratch_shapes=[pltpu.VMEM((2, tm, D), F32), pltpu.VMEM((2, tm, D), BF16)],
        compiler_params=_cparams(("arbitrary",)),
    )(flat(attn), flat(o_f), flat(o_b), flat(go), flat(x), w_out, gng, ln1_g, ln1_b,
      w_gate, w_up, w_down, ln2_g, ln2_b)
    return out.reshape(B, S, D)


def _rope_tables(seq_len):
    t = jnp.arange(seq_len, dtype=jnp.int32)
    row_id = (t // GRID_W).astype(F32)
    col_id = (t % GRID_W).astype(F32)
    inv_freq = ROPE_THETA ** (-jnp.arange(0, AXIAL_DIM, 2, dtype=F32) / AXIAL_DIM)
    lane = jnp.arange(LANES, dtype=jnp.int32)
    d = lane % HEAD_DIM
    is_col = (d // AXIAL_DIM) == 1
    upper = ((d % AXIAL_DIM) // ROPE_HALF) == 1
    ang = jnp.where(is_col, col_id[:, None], row_id[:, None]) * jnp.tile(inv_freq, LANES // ROPE_HALF)
    sin = jnp.sin(ang)
    return jnp.cos(ang), jnp.where(upper, sin, 0.0), jnp.where(upper, 0.0, -sin)


def kernel(x, w_in, q_norm_g, k_norm_g, gate_up_fwd, gate_bias_fwd, gate_up_bwd, gate_bias_bwd, gla_norm_g,
           w_out, ln1_g, ln1_b, w_ffn_gate, w_ffn_up, w_ffn_down, ln2_g, ln2_b):
    B, S, D = x.shape
    assert D == D_MODEL and S % max(PROJ_TM, ATTN_TQ, ATTN_TK, GLA_CHUNK, MIX_FFN_TM) == 0 and B % GLA_BATCH == 0
    assert w_in.shape[0] == DEPTH
    cos, sin_hi, sin_lo = _rope_tables(S)
    for layer in range(DEPTH):
        w_pad = jnp.pad(w_in[layer], ((0, 0), (0, PROJ_PAD_WIDTH - w_in.shape[-1]))).astype(BF16)
        gup = jnp.zeros((Z_PAD, 2 * GLA_QK_WIDTH), F32)
        gup = gup.at[:GATE_RANK, :GLA_QK_WIDTH].set(gate_up_fwd[layer])
        gup = gup.at[GATE_RANK:2 * GATE_RANK, GLA_QK_WIDTH:].set(gate_up_bwd[layer]).astype(BF16)
        gbias = jnp.concatenate([gate_bias_fwd[layer], gate_bias_bwd[layer]])[None, :]
        qg = jnp.tile(q_norm_g[layer], LANES // HEAD_DIM)[None, :]
        kg = jnp.tile(k_norm_g[layer], LANES // HEAD_DIM)[None, :]

        qa, ka, va, gq, gk, gv, go, laf, lab = _proj_call(x, w_pad, gup, gbias, qg, kg, cos, sin_hi, sin_lo)
        score_bound = (jnp.max(jnp.abs(q_norm_g[layer])) * jnp.max(jnp.abs(k_norm_g[layer]))
                       * SCORE_BOUND_SCALE).reshape(1).astype(F32)
        attn = _attn_call(score_bound, qa, ka, va)
        o_f, o_b = _gla_call(gq, gk, gv, laf, lab)
        x = _mix_ffn_call(attn, o_f, o_b, go, x, w_out[layer].astype(BF16), gla_norm_g[layer][None, :],
                          ln1_g[layer][None, :], ln1_b[layer][None, :],
                          w_ffn_gate[layer].astype(BF16), w_ffn_up[layer].astype(BF16),
                          w_ffn_down[layer].astype(BF16), ln2_g[layer][None, :], ln2_b[layer][None, :])
    return x
```

```python
import jax
import jax.numpy as jnp
from jax import lax
from jax.experimental import pallas as pl
from jax.experimental.pallas import tpu as pltpu

F32 = jnp.float32
BF16 = jnp.bfloat16

D_MODEL = 1024
GRID_W = 64
N_Q_HEADS = 8
N_KV_HEADS = 2
Q_PER_KV = N_Q_HEADS // N_KV_HEADS
HEAD_DIM = 64
AXIAL_DIM = HEAD_DIM // 2
ROPE_HALF = AXIAL_DIM // 2
ROPE_THETA = 10000.0
GLA_HEADS = 4
GLA_DK = 64
GLA_DV = 128
GATE_RANK = 16
GATE_TAU = 16.0
GLA_REF_CHUNK = 16
ATTN_WIDTH = N_Q_HEADS * HEAD_DIM
KV_WIDTH = N_KV_HEADS * HEAD_DIM
GLA_QK_WIDTH = GLA_HEADS * GLA_DK
GLA_WIDTH = GLA_HEADS * GLA_DV
D_FF = 2816
DEPTH = 1
DEEPNORM_ALPHA = (2 * DEPTH) ** 0.25
LN_EPS = 1e-5
RMS_EPS = 1e-6
LOG2_E = 1.4426950408889634

LANES = 128
SUBLANES = 8
VMEM_LIMIT_BYTES = 56 * 1024 * 1024

PROJ_TM = 512
ATTN_TQ = 512
ATTN_TK = 256
ATTN_COL = 512
VT_ROWS = HEAD_DIM + 16
SCORE_BOUND_SCALE = LOG2_E * HEAD_DIM ** 0.5 * 1.01
ATTN_UNSHIFTED_MAX_LOG2 = 60.0
GLA_CHUNK = 128
GLA_BATCH = 4
GLA_PHASES = 4
MIX_FFN_TM = 512
FFN_CHUNK = 256
MERGE_ROW_BLOCKS = 4

OFF_AQ = 0
OFF_AK = OFF_AQ + ATTN_WIDTH
OFF_AV = OFF_AK + KV_WIDTH
OFF_GQ = OFF_AV + KV_WIDTH
OFF_GK = OFF_GQ + GLA_QK_WIDTH
OFF_GV = OFF_GK + GLA_QK_WIDTH
OFF_GO = OFF_GV + GLA_WIDTH
OFF_Z = OFF_GO + GLA_WIDTH
Z_PAD = LANES
PROJ_PAD_WIDTH = OFF_Z + Z_PAD


def _cparams(semantics):
    return pltpu.CompilerParams(dimension_semantics=semantics, vmem_limit_bytes=VMEM_LIMIT_BYTES)


def _dot(a, b):
    return jnp.dot(a, b, preferred_element_type=F32)


def _dot_nt(a, b):
    return lax.dot_general(a, b, (((1,), (1,)), ((), ())), preferred_element_type=F32)


def _rope128(y, cos, sin_hi, sin_lo):
    return y * cos + pltpu.roll(y, ROPE_HALF, 1) * sin_hi + pltpu.roll(y, LANES - ROPE_HALF, 1) * sin_lo


def _head_pair_inv_rms(blk, lane_lo):
    sq = blk * blk
    ss_lo = jnp.sum(jnp.where(lane_lo, sq, 0.0), axis=-1, keepdims=True)
    ss_hi = jnp.sum(jnp.where(lane_lo, 0.0, sq), axis=-1, keepdims=True)
    inv = 1.0 / HEAD_DIM
    return jnp.where(lane_lo, lax.rsqrt(ss_lo * inv + RMS_EPS), lax.rsqrt(ss_hi * inv + RMS_EPS))


def _proj_kernel(x_ref, w_ref, gup_ref, gbias_ref, qg_ref, kg_ref, cos_ref, shi_ref, slo_ref,
                 qa_ref, ka_ref, va_ref, gq_ref, gk_ref, gv_ref, go_ref, laf_ref, lab_ref,
                 aq_sc, akv_sc, z_sc):
    t = pl.program_id(0)
    cur, prev = t % 2, (t + 1) % 2

    @pl.when(t == 0)
    def _():
        aq_sc[1] = jnp.zeros(aq_sc.shape[1:], F32)
        akv_sc[1] = jnp.zeros(akv_sc.shape[1:], F32)
        z_sc[1] = jnp.zeros(z_sc.shape[1:], F32)

    x = x_ref[0].astype(BF16)
    tm = x.shape[0]
    lane = lax.broadcasted_iota(jnp.int32, (tm, LANES), 1)
    lane_lo = lane < HEAD_DIM
    cos, shi, slo = cos_ref[...], shi_ref[...], slo_ref[...]

    def proj(off, width):
        return _dot(x, w_ref[:, off:off + width])

    def q_epilogue(c):
        blk = aq_sc[prev, :, c * LANES:(c + 1) * LANES]
        rot = _rope128(blk * qg_ref[...], cos, shi, slo)
        out_t = (rot * (_head_pair_inv_rms(blk, lane_lo) * (LOG2_E * HEAD_DIM ** -0.5))).T.astype(BF16)
        qa_ref[0, 2 * c] = out_t[:HEAD_DIM]
        qa_ref[0, 2 * c + 1] = out_t[HEAD_DIM:]

    def k_epilogue():
        ak = akv_sc[prev, :, :KV_WIDTH]
        rot = _rope128(ak * kg_ref[...], cos, shi, slo)
        out = (rot * _head_pair_inv_rms(ak, lane_lo)).astype(BF16)
        ka_ref[0, 0] = out[:, :HEAD_DIM]
        ka_ref[0, 1] = out[:, HEAD_DIM:]

    def v_epilogue():
        av = akv_sc[prev, :, KV_WIDTH:]
        ones_col = jnp.where(lane == HEAD_DIM, 1.0, 0.0)
        va_ref[0, 0] = jnp.where(lane_lo, av, ones_col).T.astype(BF16)
        va_ref[0, 1] = jnp.where(lane_lo, pltpu.roll(av, HEAD_DIM, 1), ones_col).T.astype(BF16)

    def gate_epilogue():
        g = _dot(z_sc[prev].astype(BF16), gup_ref[...]) + gbias_ref[...]
        log2_a = (jnp.minimum(g, 0.0) - jnp.log(1.0 + jnp.exp(-jnp.abs(g)))) * (LOG2_E / GATE_TAU)
        laf_ref[0] = log2_a[:, :GLA_QK_WIDTH]
        lab_ref[0] = log2_a[:, GLA_QK_WIDTH:]

    half = GLA_WIDTH // 2
    aq = proj(OFF_AQ, ATTN_WIDTH)
    q_epilogue(0)
    akv = proj(OFF_AK, 2 * KV_WIDTH)
    q_epilogue(1)
    gq_ref[0] = (proj(OFF_GQ, GLA_QK_WIDTH) * (GLA_DK ** -0.5)).astype(BF16)
    q_epilogue(2)
    gk_ref[0] = proj(OFF_GK, GLA_QK_WIDTH).astype(BF16)
    q_epilogue(3)
    gv_ref[0, :, :half] = proj(OFF_GV, half).astype(BF16)
    k_epilogue()
    gv_ref[0, :, half:] = proj(OFF_GV + half, half).astype(BF16)
    v_epilogue()
    go_ref[0, :, :half] = proj(OFF_GO, half).astype(BF16)
    z = proj(OFF_Z, Z_PAD)
    gate_epilogue()
    go_ref[0, :, half:] = proj(OFF_GO + half, half).astype(BF16)
    aq_sc[cur] = aq
    akv_sc[cur] = akv
    z_sc[cur] = z


def _proj_call(x, w_pad, gup, gbias, qg, kg, cos, shi, slo):
    B, S, D = x.shape
    tm = PROJ_TM
    n_s = S // tm
    n = n_s * B
    mm = lambda t: jnp.minimum(t, n - 1)
    ep = lambda t: jnp.maximum(t - 1, 0)
    tok = lambda t: (mm(t) % B, mm(t) // B, 0)
    tok_ep = lambda t: (ep(t) % B, ep(t) // B, 0)
    head = lambda t: (ep(t) % B, 0, ep(t) // B, 0)
    head_t = lambda t: (ep(t) % B, 0, 0, ep(t) // B)
    const = lambda t: (0, 0)
    tab = lambda t: (ep(t) // B, 0)
    in_specs = [
        pl.BlockSpec((1, tm, D), tok),
        pl.BlockSpec((D, PROJ_PAD_WIDTH), const),
        pl.BlockSpec((Z_PAD, 2 * GLA_QK_WIDTH), const),
        pl.BlockSpec((1, 2 * GLA_QK_WIDTH), const),
        pl.BlockSpec((1, LANES), const),
        pl.BlockSpec((1, LANES), const),
        pl.BlockSpec((tm, LANES), tab),
        pl.BlockSpec((tm, LANES), tab),
        pl.BlockSpec((tm, LANES), tab),
    ]
    out_shape = [
        jax.ShapeDtypeStruct((B, N_Q_HEADS, HEAD_DIM, S), BF16),
        jax.ShapeDtypeStruct((B, N_KV_HEADS, S, HEAD_DIM), BF16),
        jax.ShapeDtypeStruct((B, N_KV_HEADS, LANES, S), BF16),
        jax.ShapeDtypeStruct((B, S, GLA_QK_WIDTH), BF16),
        jax.ShapeDtypeStruct((B, S, GLA_QK_WIDTH), BF16),
        jax.ShapeDtypeStruct((B, S, GLA_WIDTH), BF16),
        jax.ShapeDtypeStruct((B, S, GLA_WIDTH), BF16),
        jax.ShapeDtypeStruct((B, S, GLA_QK_WIDTH), F32),
        jax.ShapeDtypeStruct((B, S, GLA_QK_WIDTH), F32),
    ]
    out_specs = [
        pl.BlockSpec((1, N_Q_HEADS, HEAD_DIM, tm), head_t),
        pl.BlockSpec((1, N_KV_HEADS, tm, HEAD_DIM), head),
        pl.BlockSpec((1, N_KV_HEADS, LANES, tm), head_t),
        pl.BlockSpec((1, tm, GLA_QK_WIDTH), tok),
        pl.BlockSpec((1, tm, GLA_QK_WIDTH), tok),
        pl.BlockSpec((1, tm, GLA_WIDTH), tok),
        pl.BlockSpec((1, tm, GLA_WIDTH), tok),
        pl.BlockSpec((1, tm, GLA_QK_WIDTH), tok_ep),
        pl.BlockSpec((1, tm, GLA_QK_WIDTH), tok_ep),
    ]
    scratch = [pltpu.VMEM((2, tm, ATTN_WIDTH), F32), pltpu.VMEM((2, tm, 2 * KV_WIDTH), F32),
               pltpu.VMEM((2, tm, Z_PAD), F32)]
    return pl.pallas_call(
        _proj_kernel, grid=(n + 1,), in_specs=in_specs, out_specs=out_specs, out_shape=out_shape,
        scratch_shapes=scratch, compiler_params=_cparams(("arbitrary",)),
    )(x, w_pad, gup, gbias, qg, kg, cos, shi, slo)


def _attn_kernel(bound_ref, qt_ref, k_ref, vt_ref, o_ref):
    tq = qt_ref.shape[3]
    n_keys = k_ref.shape[2]
    tk = ATTN_TK
    q_t = jnp.concatenate([qt_ref[0, r] for r in range(Q_PER_KV)], axis=1)
    unshifted_is_safe = bound_ref[0] < ATTN_UNSHIFTED_MAX_LOG2

    def finish(acc, den_row):
        o_t = acc[:HEAD_DIM] / den_row
        for r in range(Q_PER_KV):
            o_ref[0, :, r * HEAD_DIM:(r + 1) * HEAD_DIM] = o_t[:, r * tq:(r + 1) * tq].T.astype(o_ref.dtype)

    @pl.when(unshifted_is_safe)
    def _():
        n = n_keys // tk
        n_col = q_t.shape[1] // ATTN_COL
        col = lambda c: slice(c * ATTN_COL, (c + 1) * ATTN_COL)
        scores = lambda j, c: _dot(k_ref[0, 0, j * tk:(j + 1) * tk, :], q_t[:, col(c)])
        acc = [None] * n_col
        den = [None] * n_col
        s_cur = [scores(0, c) for c in range(n_col)]
        for j in range(n):
            s_nxt = []
            for c in range(n_col):
                p = jnp.exp2(s_cur[c])
                pv = _dot(vt_ref[0, 0, :HEAD_DIM, j * tk:(j + 1) * tk], p.astype(BF16))
                acc[c] = pv if acc[c] is None else acc[c] + pv
                part = jnp.sum(p.reshape(tk // SUBLANES, SUBLANES, ATTN_COL), axis=0)
                den[c] = part if den[c] is None else den[c] + part
                if j + 1 < n:
                    s_nxt.append(scores(j + 1, c))
            s_cur = s_nxt
        den_row = jnp.sum(jnp.concatenate(den, axis=1), axis=0, keepdims=True)
        finish(jnp.concatenate(acc, axis=1), den_row)

    @pl.when(jnp.logical_not(unshifted_is_safe))
    def _():
        m = None
        acc = None
        for j in range(n_keys // tk):
            s_t = _dot(k_ref[0, 0, j * tk:(j + 1) * tk, :], q_t)
            m_blk = jnp.max(s_t, axis=0, keepdims=True)
            m_new = m_blk if m is None else jnp.maximum(m, m_blk)
            pv = _dot(vt_ref[0, 0, :VT_ROWS, j * tk:(j + 1) * tk], jnp.exp2(s_t - m_new).astype(BF16))
            acc = pv if acc is None else jnp.exp2(m - m_new) * acc + pv
            m = m_new
        finish(acc, acc[HEAD_DIM:HEAD_DIM + 1])


def _attn_call(score_bound, qa_t, ka, va_t):
    B, _, S, _ = ka.shape
    tq = ATTN_TQ
    grid = (B, N_KV_HEADS, S // tq)
    return pl.pallas_call(
        _attn_kernel, grid=grid,
        in_specs=[
            pl.BlockSpec(memory_space=pltpu.SMEM),
            pl.BlockSpec((1, Q_PER_KV, HEAD_DIM, tq), lambda b, g, qi: (b, g, 0, qi)),
            pl.BlockSpec((1, 1, S, HEAD_DIM), lambda b, g, qi: (b, g, 0, 0)),
            pl.BlockSpec((1, 1, LANES, S), lambda b, g, qi: (b, g, 0, 0)),
        ],
        out_specs=pl.BlockSpec((1, tq, Q_PER_KV * HEAD_DIM), lambda b, g, qi: (b, qi, g)),
        out_shape=jax.ShapeDtypeStruct((B, S, ATTN_WIDTH), BF16),
        compiler_params=_cparams(("arbitrary", "arbitrary", "arbitrary")),
    )(score_bound, qa_t, ka, va_t)


def _block_row(a, blk, row):
    n, w = a.shape
    a3 = a.reshape(n // blk, blk, w)
    return jnp.broadcast_to(a3[:, row:row + 1, :], a3.shape).reshape(n, w)


def _split3_bf16(a):
    hi = a.astype(BF16)
    r1 = a - hi.astype(F32)
    mid = r1.astype(BF16)
    lo = (r1 - mid.astype(F32)).astype(BF16)
    return hi, mid, lo


def _gla_direction(bi, q_ref, k_ref, v_ref, la_ref, o_ref, state_ref, forward):
    C = q_ref.shape[1]
    row = lax.broadcasted_iota(jnp.int32, (C, C), 0)
    col = lax.broadcasted_iota(jnp.int32, (C, C), 1)
    tri = ((col <= row) if forward else (col >= row)).astype(BF16)
    la = la_ref[bi]
    cum_all = sum(_dot(tri, part) for part in _split3_bf16(la))
    excl_all = cum_all - la
    xr = row ^ col
    valid = (col <= row) if forward else (col > row)
    lane = lax.broadcasted_iota(jnp.int32, (C, LANES), 1)
    lane_lo = lane < GLA_DK
    edge = C - 1 if forward else 0
    srow = lax.broadcasted_iota(jnp.int32, (LANES, 2 * GLA_DV), 0)
    scol = lax.broadcasted_iota(jnp.int32, (LANES, 2 * GLA_DV), 1)
    on_diag = (srow < GLA_DK) == (scol < GLA_DV)
    pairs = range(GLA_HEADS // 2)
    yield

    factors = []
    for pair in pairs:
        sl = slice(pair * LANES, (pair + 1) * LANES)
        q, k = q_ref[bi, :, sl].astype(F32), k_ref[bi, :, sl].astype(F32)
        cum, excl = cum_all[:, sl], excl_all[:, sl]
        total = cum[edge:edge + 1, :]

        base_row = 0 if forward else GLA_REF_CHUNK - 1
        loc = cum - _block_row(excl, GLA_REF_CHUNK, base_row)
        q_lv = [q * jnp.exp2(loc)]
        k_lv = [(k * jnp.exp2(-loc)).astype(BF16)]
        blk = 2 * GLA_REF_CHUNK
        while blk <= C:
            mid = _block_row(excl if forward else cum, blk, blk // 2)
            e = jnp.exp2(-jnp.abs(cum - mid))
            q_lv.append(q * e)
            k_lv.append((k * e).astype(BF16))
            blk *= 2
        q_in = (q * jnp.exp2(cum)).astype(BF16)
        k_out = k * jnp.exp2(total - cum)
        k_out_t = k_out.T.astype(BF16)
        decay_t = jnp.exp2(total).T
        factors.append((q_lv, k_lv, q_in, k_out_t, decay_t))
    yield

    def stack_heads(a):
        return jnp.concatenate([jnp.where(lane_lo, a, 0.0), jnp.where(lane_lo, 0.0, a)], axis=0).astype(BF16)

    xr2 = jnp.concatenate([xr, xr], axis=0)
    valid2 = jnp.concatenate([valid, valid], axis=0)
    scores = []
    for pair in pairs:
        q_lv, k_lv = factors[pair][:2]
        scores2 = _dot_nt(stack_heads(q_lv[-1]), k_lv[-1])
        bound = C // 2
        for lv in range(len(q_lv) - 2, -1, -1):
            scores2 = jnp.where(xr2 < bound, _dot_nt(stack_heads(q_lv[lv]), k_lv[lv]), scores2)
            bound //= 2
        scores.append(jnp.where(valid2, scores2, 0.0).astype(BF16))
    yield

    for pair in pairs:
        q_in, k_out_t, decay_t = factors[pair][2:]
        v2 = v_ref[bi, :, 2 * pair * GLA_DV:2 * (pair + 1) * GLA_DV]
        state = state_ref[bi, pair]
        inter = _dot(q_in, state.astype(BF16))
        for hh in range(2):
            h = 2 * pair + hh
            cols = slice(hh * GLA_DV, (hh + 1) * GLA_DV)
            o = _dot(scores[pair][hh * C:(hh + 1) * C], v2[:, cols]) + inter[:, cols]
            o_ref[bi, :, h * GLA_DV:(h + 1) * GLA_DV] = o.astype(o_ref.dtype)
        state_ref[bi, pair] = jnp.where(on_diag, decay_t * state + _dot(k_out_t, v2), 0.0)
    yield


def _gla_kernel(qf_ref, kf_ref, vf_ref, laf_ref, qb_ref, kb_ref, vb_ref, lab_ref,
                of_ref, ob_ref, sf_ref, sb_ref):
    @pl.when(pl.program_id(1) == 0)
    def _():
        sf_ref[...] = jnp.zeros(sf_ref.shape, F32)
        sb_ref[...] = jnp.zeros(sb_ref.shape, F32)

    chains = []
    for bi in range(GLA_BATCH):
        chains.append(_gla_direction(bi, qf_ref, kf_ref, vf_ref, laf_ref, of_ref, sf_ref, True))
        chains.append(_gla_direction(bi, qb_ref, kb_ref, vb_ref, lab_ref, ob_ref, sb_ref, False))
    for _ in range(GLA_PHASES):
        for chain in chains:
            next(chain)


def _gla_call(gq, gk, gv, laf, lab):
    B, S, _ = gq.shape
    C, nb = GLA_CHUNK, GLA_BATCH
    n = S // C
    fwd = lambda b, c: (b, c, 0)
    bwd = lambda b, c: (b, n - 1 - c, 0)
    qk = lambda im: pl.BlockSpec((nb, C, GLA_QK_WIDTH), im)
    vv = lambda im: pl.BlockSpec((nb, C, GLA_WIDTH), im)
    state = pltpu.VMEM((nb, GLA_HEADS // 2, LANES, 2 * GLA_DV), F32)
    return pl.pallas_call(
        _gla_kernel, grid=(B // nb, n),
        in_specs=[qk(fwd), qk(fwd), vv(fwd), qk(fwd), qk(bwd), qk(bwd), vv(bwd), qk(bwd)],
        out_specs=[vv(fwd), vv(bwd)],
        out_shape=[jax.ShapeDtypeStruct((B, S, GLA_WIDTH), BF16)] * 2,
        scratch_shapes=[state, state],
        compiler_params=_cparams(("arbitrary", "arbitrary")),
    )(gq, gk, gv, laf, gq, gk, gv, lab)


def _layer_norm(y, g, b):
    mu = jnp.mean(y, axis=-1, keepdims=True)
    d = y - mu
    var = jnp.mean(d * d, axis=-1, keepdims=True)
    return d * lax.rsqrt(var + LN_EPS) * g + b


def _silu(g):
    return g * (1.0 / (1.0 + jnp.exp(-g)))


def _mix_ffn_kernel(attn_ref, of_ref, ob_ref, go_ref, x_ref, wo_ref, gng_ref, ln1g_ref, ln1b_ref,
                    wg_ref, wu_ref, wd_ref, ln2g_ref, ln2b_ref, out_ref, x1_sc, x1b_sc):
    t = pl.program_id(0)
    cur, prev = t % 2, (t + 1) % 2

    @pl.when(t == 0)
    def _():
        x1_sc[1] = jnp.zeros(x1_sc.shape[1:], F32)
        x1b_sc[1] = jnp.zeros(x1b_sc.shape[1:], BF16)

    hidden = []

    def ffn_chunk(c):
        sl = slice(c * FFN_CHUNK, (c + 1) * FFN_CHUNK)
        xb = x1b_sc[prev]
        hidden.append((_silu(_dot(xb, wg_ref[:, sl])) * _dot(xb, wu_ref[:, sl])).astype(BF16))

    def gate_head(h):
        sl = slice(h * GLA_DV, (h + 1) * GLA_DV)
        o = of_ref[:, sl].astype(F32) + ob_ref[:, sl].astype(F32)
        inv = lax.rsqrt(jnp.mean(o * o, axis=-1, keepdims=True) + RMS_EPS)
        return ((o * inv * gng_ref[...]) * _silu(go_ref[:, sl].astype(F32))).astype(BF16)

    n_chunks = D_FF // FFN_CHUNK
    gated = []
    for c in range(GLA_HEADS):
        ffn_chunk(c)
        gated.append(gate_head(c))
    ffn_chunk(GLA_HEADS)
    merged = jnp.concatenate([attn_ref[...]] + gated, axis=-1)
    tm = merged.shape[0]
    blk = tm // MERGE_ROW_BLOCKS
    for r in range(MERGE_ROW_BLOCKS):
        rows = slice(r * blk, (r + 1) * blk)
        mixed = _dot(merged[rows], wo_ref[...])
        ffn_chunk(GLA_HEADS + 1 + r)
        x1 = _layer_norm(DEEPNORM_ALPHA * x_ref[rows, :] + mixed, ln1g_ref[...], ln1b_ref[...])
        x1_sc[cur, rows, :] = x1
        x1b_sc[cur, rows, :] = x1.astype(BF16)
    for c in range(GLA_HEADS + 1 + MERGE_ROW_BLOCKS, n_chunks):
        ffn_chunk(c)
    hidden_all = jnp.concatenate(hidden, axis=-1)
    half_rows = tm // 2
    for r in range(2):
        rows = slice(r * half_rows, (r + 1) * half_rows)
        ffn = _dot(hidden_all[rows], wd_ref[...])
        out_ref[rows, :] = _layer_norm(DEEPNORM_ALPHA * x1_sc[prev, rows, :] + ffn, ln2g_ref[...], ln2b_ref[...])


def _mix_ffn_call(attn, o_f, o_b, go, x, w_out, gng, ln1_g, ln1_b, w_gate, w_up, w_down, ln2_g, ln2_b):
    B, S, D = x.shape
    tm = MIX_FFN_TM
    tokens = B * S
    n = tokens // tm
    flat = lambda a: a.reshape(tokens, a.shape[-1])
    merge_tile = lambda t: (jnp.minimum(t, n - 1), 0)
    ffn_tile = lambda t: (jnp.maximum(t - 1, 0), 0)
    const = lambda t: (0, 0)
    resident = lambda shape: pl.BlockSpec(shape, const, pipeline_mode=pl.Buffered(1))
    out = pl.pallas_call(
        _mix_ffn_kernel, grid=(n + 1,),
        in_specs=[
            pl.BlockSpec((tm, ATTN_WIDTH), merge_tile),
            pl.BlockSpec((tm, GLA_WIDTH), merge_tile),
            pl.BlockSpec((tm, GLA_WIDTH), merge_tile),
            pl.BlockSpec((tm, GLA_WIDTH), merge_tile),
            pl.BlockSpec((tm, D), merge_tile),
            resident((ATTN_WIDTH + GLA_WIDTH, D)),
            pl.BlockSpec((1, GLA_DV), const),
            pl.BlockSpec((1, D), const),
            pl.BlockSpec((1, D), const),
            resident((D, D_FF)), resident((D, D_FF)), resident((D_FF, D)),
            pl.BlockSpec((1, D), const),
            pl.BlockSpec((1, D), const),
        ],
        out_specs=pl.BlockSpec((tm, D), ffn_tile),
        out_shape=jax.ShapeDtypeStruct((tokens, D), F32),
        scratch_shapes=[pltpu.VMEM((2, tm, D), F32), pltpu.VMEM((2, tm, D), BF16)],
        compiler_params=_cparams(("arbitrary",)),
    )(flat(attn), flat(o_f), flat(o_b), flat(go), flat(x), w_out, gng, ln1_g, ln1_b,
      w_gate, w_up, w_down, ln2_g, ln2_b)
    return out.reshape(B, S, D)


def _rope_tables(seq_len):
    t = jnp.arange(seq_len, dtype=jnp.int32)
    row_id = (t // GRID_W).astype(F32)
    col_id = (t % GRID_W).astype(F32)
    inv_freq = ROPE_THETA ** (-jnp.arange(0, AXIAL_DIM, 2, dtype=F32) / AXIAL_DIM)
    lane = jnp.arange(LANES, dtype=jnp.int32)
    d = lane % HEAD_DIM
    is_col = (d // AXIAL_DIM) == 1
    upper = ((d % AXIAL_DIM) // ROPE_HALF) == 1
    ang = jnp.where(is_col, col_id[:, None], row_id[:, None]) * jnp.tile(inv_freq, LANES // ROPE_HALF)
    sin = jnp.sin(ang)
    return jnp.cos(ang), jnp.where(upper, sin, 0.0), jnp.where(upper, 0.0, -sin)


def kernel(x, w_in, q_norm_g, k_norm_g, gate_up_fwd, gate_bias_fwd, gate_up_bwd, gate_bias_bwd, gla_norm_g,
           w_out, ln1_g, ln1_b, w_ffn_gate, w_ffn_up, w_ffn_down, ln2_g, ln2_b):
    B, S, D = x.shape
    assert D == D_MODEL and S % max(PROJ_TM, ATTN_TQ, ATTN_TK, GLA_CHUNK, MIX_FFN_TM) == 0 and B % GLA_BATCH == 0
    assert w_in.shape[0] == DEPTH
    cos, sin_hi, sin_lo = _rope_tables(S)
    for layer in range(DEPTH):
        w_pad = jnp.pad(w_in[layer], ((0, 0), (0, PROJ_PAD_WIDTH - w_in.shape[-1]))).astype(BF16)
        gup = jnp.zeros((Z_PAD, 2 * GLA_QK_WIDTH), F32)
        gup = gup.at[:GATE_RANK, :GLA_QK_WIDTH].set(gate_up_fwd[layer])
        gup = gup.at[GATE_RANK:2 * GATE_RANK, GLA_QK_WIDTH:].set(gate_up_bwd[layer]).astype(BF16)
        gbias = jnp.concatenate([gate_bias_fwd[layer], gate_bias_bwd[layer]])[None, :]
        qg = jnp.tile(q_norm_g[layer], LANES // HEAD_DIM)[None, :]
        kg = jnp.tile(k_norm_g[layer], LANES // HEAD_DIM)[None, :]

        qa, ka, va, gq, gk, gv, go, laf, lab = _proj_call(x, w_pad, gup, gbias, qg, kg, cos, sin_hi, sin_lo)
        score_bound = (jnp.max(jnp.abs(q_norm_g[layer])) * jnp.max(jnp.abs(k_norm_g[layer]))
                       * SCORE_BOUND_SCALE).reshape(1).astype(F32)
        attn = _attn_call(score_bound, qa, ka, va)
        o_f, o_b = _gla_call(gq, gk, gv, laf, lab)
        x = _mix_ffn_call(attn, o_f, o_b, go, x, w_out[layer].astype(BF16), gla_norm_g[layer][None, :],
                          ln1_g[layer][None, :], ln1_b[layer][None, :],
                          w_ffn_gate[layer].astype(BF16), w_ffn_up[layer].astype(BF16),
                          w_ffn_down[layer].astype(BF16), ln2_g[layer][None, :], ln2_b[layer][None, :])
    return x
```

```python
import jax
import jax.numpy as jnp
from jax import lax
from jax.experimental import pallas as pl
from jax.experimental.pallas import tpu as pltpu

F32 = jnp.float32
BF16 = jnp.bfloat16

D_MODEL = 1024
GRID_W = 64
N_Q_HEADS = 8
N_KV_HEADS = 2
Q_PER_KV = N_Q_HEADS // N_KV_HEADS
HEAD_DIM = 64
AXIAL_DIM = HEAD_DIM // 2
ROPE_HALF = AXIAL_DIM // 2
ROPE_THETA = 10000.0
GLA_HEADS = 4
GLA_DK = 64
GLA_DV = 128
GATE_RANK = 16
GATE_TAU = 16.0
GLA_REF_CHUNK = 16
ATTN_WIDTH = N_Q_HEADS * HEAD_DIM
KV_WIDTH = N_KV_HEADS * HEAD_DIM
GLA_QK_WIDTH = GLA_HEADS * GLA_DK
GLA_WIDTH = GLA_HEADS * GLA_DV
D_FF = 2816
DEPTH = 1
DEEPNORM_ALPHA = (2 * DEPTH) ** 0.25
LN_EPS = 1e-5
RMS_EPS = 1e-6
LOG2_E = 1.4426950408889634

LANES = 128
SUBLANES = 8
VMEM_LIMIT_BYTES = 56 * 1024 * 1024

PROJ_TM = 512
ATTN_TQ = 512
ATTN_TK = 256
ATTN_COL = 512
VT_ROWS = HEAD_DIM + 16
SCORE_BOUND_SCALE = LOG2_E * HEAD_DIM ** 0.5 * 1.01
ATTN_UNSHIFTED_MAX_LOG2 = 60.0
GLA_CHUNK = 128
GLA_BATCH = 4
GLA_PHASES = 4
MIX_FFN_TM = 512
FFN_CHUNK = 256

OFF_AQ = 0
OFF_AK = OFF_AQ + ATTN_WIDTH
OFF_AV = OFF_AK + KV_WIDTH
OFF_GQ = OFF_AV + KV_WIDTH
OFF_GK = OFF_GQ + GLA_QK_WIDTH
OFF_GV = OFF_GK + GLA_QK_WIDTH
OFF_GO = OFF_GV + GLA_WIDTH
OFF_Z = OFF_GO + GLA_WIDTH
Z_PAD = LANES
PROJ_PAD_WIDTH = OFF_Z + Z_PAD


def _cparams(semantics):
    return pltpu.CompilerParams(dimension_semantics=semantics, vmem_limit_bytes=VMEM_LIMIT_BYTES)


def _dot(a, b):
    return jnp.dot(a, b, preferred_element_type=F32)


def _dot_nt(a, b):
    return lax.dot_general(a, b, (((1,), (1,)), ((), ())), preferred_element_type=F32)


def _rope128(y, cos, sin_hi, sin_lo):
    return y * cos + pltpu.roll(y, ROPE_HALF, 1) * sin_hi + pltpu.roll(y, LANES - ROPE_HALF, 1) * sin_lo


def _head_pair_inv_rms(blk, lane_lo):
    sq = blk * blk
    ss_lo = jnp.sum(jnp.where(lane_lo, sq, 0.0), axis=-1, keepdims=True)
    ss_hi = jnp.sum(jnp.where(lane_lo, 0.0, sq), axis=-1, keepdims=True)
    inv = 1.0 / HEAD_DIM
    return jnp.where(lane_lo, lax.rsqrt(ss_lo * inv + RMS_EPS), lax.rsqrt(ss_hi * inv + RMS_EPS))


def _proj_kernel(x_ref, w_ref, gup_ref, gbias_ref, qg_ref, kg_ref, cos_ref, shi_ref, slo_ref,
                 qa_ref, ka_ref, va_ref, gq_ref, gk_ref, gv_ref, go_ref, laf_ref, lab_ref,
                 aq_sc, akv_sc, z_sc):
    t = pl.program_id(0)
    cur, prev = t % 2, (t + 1) % 2

    @pl.when(t == 0)
    def _():
        aq_sc[1] = jnp.zeros(aq_sc.shape[1:], F32)
        akv_sc[1] = jnp.zeros(akv_sc.shape[1:], F32)
        z_sc[1] = jnp.zeros(z_sc.shape[1:], F32)

    x = x_ref[0].astype(BF16)
    tm = x.shape[0]
    lane = lax.broadcasted_iota(jnp.int32, (tm, LANES), 1)
    lane_lo = lane < HEAD_DIM
    cos, shi, slo = cos_ref[...], shi_ref[...], slo_ref[...]

    def proj(off, width):
        return _dot(x, w_ref[:, off:off + width])

    def q_epilogue(c):
        blk = aq_sc[prev, :, c * LANES:(c + 1) * LANES]
        rot = _rope128(blk * qg_ref[...], cos, shi, slo)
        out_t = (rot * (_head_pair_inv_rms(blk, lane_lo) * (LOG2_E * HEAD_DIM ** -0.5))).T.astype(BF16)
        qa_ref[0, 2 * c] = out_t[:HEAD_DIM]
        qa_ref[0, 2 * c + 1] = out_t[HEAD_DIM:]

    def k_epilogue():
        ak = akv_sc[prev, :, :KV_WIDTH]
        rot = _rope128(ak * kg_ref[...], cos, shi, slo)
        out = (rot * _head_pair_inv_rms(ak, lane_lo)).astype(BF16)
        ka_ref[0, 0] = out[:, :HEAD_DIM]
        ka_ref[0, 1] = out[:, HEAD_DIM:]

    def v_epilogue():
        av = akv_sc[prev, :, KV_WIDTH:]
        ones_col = jnp.where(lane == HEAD_DIM, 1.0, 0.0)
        va_ref[0, 0] = jnp.where(lane_lo, av, ones_col).T.astype(BF16)
        va_ref[0, 1] = jnp.where(lane_lo, pltpu.roll(av, HEAD_DIM, 1), ones_col).T.astype(BF16)

    def gate_epilogue():
        g = _dot(z_sc[prev].astype(BF16), gup_ref[...]) + gbias_ref[...]
        log2_a = (jnp.minimum(g, 0.0) - jnp.log(1.0 + jnp.exp(-jnp.abs(g)))) * (LOG2_E / GATE_TAU)
        laf_ref[0] = log2_a[:, :GLA_QK_WIDTH]
        lab_ref[0] = log2_a[:, GLA_QK_WIDTH:]

    half = GLA_WIDTH // 2
    aq = proj(OFF_AQ, ATTN_WIDTH)
    q_epilogue(0)
    akv = proj(OFF_AK, 2 * KV_WIDTH)
    q_epilogue(1)
    gq_ref[0] = (proj(OFF_GQ, GLA_QK_WIDTH) * (GLA_DK ** -0.5)).astype(BF16)
    q_epilogue(2)
    gk_ref[0] = proj(OFF_GK, GLA_QK_WIDTH).astype(BF16)
    q_epilogue(3)
    gv_ref[0, :, :half] = proj(OFF_GV, half).astype(BF16)
    k_epilogue()
    gv_ref[0, :, half:] = proj(OFF_GV + half, half).astype(BF16)
    v_epilogue()
    go_ref[0, :, :half] = proj(OFF_GO, half).astype(BF16)
    z = proj(OFF_Z, Z_PAD)
    gate_epilogue()
    go_ref[0, :, half:] = proj(OFF_GO + half, half).astype(BF16)
    aq_sc[cur] = aq
    akv_sc[cur] = akv
    z_sc[cur] = z


def _proj_call(x, w_pad, gup, gbias, qg, kg, cos, shi, slo):
    B, S, D = x.shape
    tm = PROJ_TM
    n_s = S // tm
    n = n_s * B
    mm = lambda t: jnp.minimum(t, n - 1)
    ep = lambda t: jnp.maximum(t - 1, 0)
    tok = lambda t: (mm(t) % B, mm(t) // B, 0)
    tok_ep = lambda t: (ep(t) % B, ep(t) // B, 0)
    head = lambda t: (ep(t) % B, 0, ep(t) // B, 0)
    head_t = lambda t: (ep(t) % B, 0, 0, ep(t) // B)
    const = lambda t: (0, 0)
    tab = lambda t: (ep(t) // B, 0)
    in_specs = [
        pl.BlockSpec((1, tm, D), tok),
        pl.BlockSpec((D, PROJ_PAD_WIDTH), const),
        pl.BlockSpec((Z_PAD, 2 * GLA_QK_WIDTH), const),
        pl.BlockSpec((1, 2 * GLA_QK_WIDTH), const),
        pl.BlockSpec((1, LANES), const),
        pl.BlockSpec((1, LANES), const),
        pl.BlockSpec((tm, LANES), tab),
        pl.BlockSpec((tm, LANES), tab),
        pl.BlockSpec((tm, LANES), tab),
    ]
    out_shape = [
        jax.ShapeDtypeStruct((B, N_Q_HEADS, HEAD_DIM, S), BF16),
        jax.ShapeDtypeStruct((B, N_KV_HEADS, S, HEAD_DIM), BF16),
        jax.ShapeDtypeStruct((B, N_KV_HEADS, LANES, S), BF16),
        jax.ShapeDtypeStruct((B, S, GLA_QK_WIDTH), BF16),
        jax.ShapeDtypeStruct((B, S, GLA_QK_WIDTH), BF16),
        jax.ShapeDtypeStruct((B, S, GLA_WIDTH), BF16),
        jax.ShapeDtypeStruct((B, S, GLA_WIDTH), BF16),
        jax.ShapeDtypeStruct((B, S, GLA_QK_WIDTH), F32),
        jax.ShapeDtypeStruct((B, S, GLA_QK_WIDTH), F32),
    ]
    out_specs = [
        pl.BlockSpec((1, N_Q_HEADS, HEAD_DIM, tm), head_t),
        pl.BlockSpec((1, N_KV_HEADS, tm, HEAD_DIM), head),
        pl.BlockSpec((1, N_KV_HEADS, LANES, tm), head_t),
        pl.BlockSpec((1, tm, GLA_QK_WIDTH), tok),
        pl.BlockSpec((1, tm, GLA_QK_WIDTH), tok),
        pl.BlockSpec((1, tm, GLA_WIDTH), tok),
        pl.BlockSpec((1, tm, GLA_WIDTH), tok),
        pl.BlockSpec((1, tm, GLA_QK_WIDTH), tok_ep),
        pl.BlockSpec((1, tm, GLA_QK_WIDTH), tok_ep),
    ]
    scratch = [pltpu.VMEM((2, tm, ATTN_WIDTH), F32), pltpu.VMEM((2, tm, 2 * KV_WIDTH), F32),
               pltpu.VMEM((2, tm, Z_PAD), F32)]
    return pl.pallas_call(
        _proj_kernel, grid=(n + 1,), in_specs=in_specs, out_specs=out_specs, out_shape=out_shape,
        scratch_shapes=scratch, compiler_params=_cparams(("arbitrary",)),
    )(x, w_pad, gup, gbias, qg, kg, cos, shi, slo)


def _attn_kernel(bound_ref, qt_ref, k_ref, vt_ref, o_ref, acc_sc, den_sc):
    t = pl.program_id(0)
    cur, prev = t % 2, (t + 1) % 2
    tq = qt_ref.shape[3]
    n_keys = k_ref.shape[2]
    tk = ATTN_TK
    q_t = jnp.concatenate([qt_ref[0, r] for r in range(Q_PER_KV)], axis=1)
    unshifted_is_safe = bound_ref[0] < ATTN_UNSHIFTED_MAX_LOG2

    @pl.when(t == 0)
    def _():
        acc_sc[1] = jnp.zeros(acc_sc.shape[1:], F32)
        den_sc[1] = jnp.ones(den_sc.shape[1:], F32)

    def finish_previous():
        o_t = acc_sc[prev] / den_sc[prev]
        for r in range(Q_PER_KV):
            o_ref[0, :, r * HEAD_DIM:(r + 1) * HEAD_DIM] = o_t[:, r * tq:(r + 1) * tq].T.astype(o_ref.dtype)

    def finish(acc, den_row):
        acc_sc[cur] = acc[:HEAD_DIM]
        den_sc[cur] = den_row

    @pl.when(unshifted_is_safe)
    def _():
        n = n_keys // tk
        n_col = q_t.shape[1] // ATTN_COL
        col = lambda c: slice(c * ATTN_COL, (c + 1) * ATTN_COL)
        scores = lambda j, c: _dot(k_ref[0, 0, j * tk:(j + 1) * tk, :], q_t[:, col(c)])
        acc = [None] * n_col
        den = [None] * n_col
        s_cur = [scores(0, c) for c in range(n_col)]
        for j in range(n):
            s_nxt = []
            for c in range(n_col):
                p = jnp.exp2(s_cur[c])
                pv = _dot(vt_ref[0, 0, :HEAD_DIM, j * tk:(j + 1) * tk], p.astype(BF16))
                acc[c] = pv if acc[c] is None else acc[c] + pv
                part = jnp.sum(p.reshape(tk // SUBLANES, SUBLANES, ATTN_COL), axis=0)
                den[c] = part if den[c] is None else den[c] + part
                if j + 1 < n:
                    s_nxt.append(scores(j + 1, c))
            s_cur = s_nxt
            if j == 0:
                finish_previous()
        den_row = jnp.sum(jnp.concatenate(den, axis=1), axis=0, keepdims=True)
        finish(jnp.concatenate(acc, axis=1), den_row)

    @pl.when(jnp.logical_not(unshifted_is_safe))
    def _():
        finish_previous()
        m = None
        acc = None
        for j in range(n_keys // tk):
            s_t = _dot(k_ref[0, 0, j * tk:(j + 1) * tk, :], q_t)
            m_blk = jnp.max(s_t, axis=0, keepdims=True)
            m_new = m_blk if m is None else jnp.maximum(m, m_blk)
            pv = _dot(vt_ref[0, 0, :VT_ROWS, j * tk:(j + 1) * tk], jnp.exp2(s_t - m_new).astype(BF16))
            acc = pv if acc is None else jnp.exp2(m - m_new) * acc + pv
            m = m_new
        finish(acc, acc[HEAD_DIM:HEAD_DIM + 1])


def _attn_call(score_bound, qa_t, ka, va_t):
    B, _, S, _ = ka.shape
    tq = ATTN_TQ
    nq = S // tq
    n = B * N_KV_HEADS * nq
    acc_blk = lambda t: jnp.minimum(t, n - 1)
    out_blk = lambda t: jnp.maximum(t - 1, 0)
    b_of = lambda u: u // (N_KV_HEADS * nq)
    g_of = lambda u: (u // nq) % N_KV_HEADS
    return pl.pallas_call(
        _attn_kernel, grid=(n + 1,),
        in_specs=[
            pl.BlockSpec(memory_space=pltpu.SMEM),
            pl.BlockSpec((1, Q_PER_KV, HEAD_DIM, tq), lambda t: (b_of(acc_blk(t)), g_of(acc_blk(t)), 0, acc_blk(t) % nq)),
            pl.BlockSpec((1, 1, S, HEAD_DIM), lambda t: (b_of(acc_blk(t)), g_of(acc_blk(t)), 0, 0)),
            pl.BlockSpec((1, 1, LANES, S), lambda t: (b_of(acc_blk(t)), g_of(acc_blk(t)), 0, 0)),
        ],
        out_specs=pl.BlockSpec((1, tq, Q_PER_KV * HEAD_DIM),
                               lambda t: (b_of(out_blk(t)), out_blk(t) % nq, g_of(out_blk(t)))),
        out_shape=jax.ShapeDtypeStruct((B, S, ATTN_WIDTH), BF16),
        scratch_shapes=[pltpu.VMEM((2, HEAD_DIM, Q_PER_KV * tq), F32), pltpu.VMEM((2, 1, Q_PER_KV * tq), F32)],
        compiler_params=_cparams(("arbitrary",)),
    )(score_bound, qa_t, ka, va_t)


def _block_row(a, blk, row):
    n, w = a.shape
    a3 = a.reshape(n // blk, blk, w)
    return jnp.broadcast_to(a3[:, row:row + 1, :], a3.shape).reshape(n, w)


def _split3_bf16(a):
    hi = a.astype(BF16)
    r1 = a - hi.astype(F32)
    mid = r1.astype(BF16)
    lo = (r1 - mid.astype(F32)).astype(BF16)
    return hi, mid, lo


def _gla_direction(bi, q_ref, k_ref, v_ref, la_ref, o_ref, state_ref, forward):
    C = q_ref.shape[1]
    row = lax.broadcasted_iota(jnp.int32, (C, C), 0)
    col = lax.broadcasted_iota(jnp.int32, (C, C), 1)
    tri = ((col <= row) if forward else (col >= row)).astype(BF16)
    la = la_ref[bi]
    cum_all = sum(_dot(tri, part) for part in _split3_bf16(la))
    excl_all = cum_all - la
    xr = row ^ col
    valid = (col <= row) if forward else (col > row)
    lane = lax.broadcasted_iota(jnp.int32, (C, LANES), 1)
    lane_lo = lane < GLA_DK
    edge = C - 1 if forward else 0
    srow = lax.broadcasted_iota(jnp.int32, (LANES, 2 * GLA_DV), 0)
    scol = lax.broadcasted_iota(jnp.int32, (LANES, 2 * GLA_DV), 1)
    on_diag = (srow < GLA_DK) == (scol < GLA_DV)
    pairs = range(GLA_HEADS // 2)
    yield

    factors = []
    for pair in pairs:
        sl = slice(pair * LANES, (pair + 1) * LANES)
        q, k = q_ref[bi, :, sl].astype(F32), k_ref[bi, :, sl].astype(F32)
        cum, excl = cum_all[:, sl], excl_all[:, sl]
        total = cum[edge:edge + 1, :]

        base_row = 0 if forward else GLA_REF_CHUNK - 1
        loc = cum - _block_row(excl, GLA_REF_CHUNK, base_row)
        q_lv = [q * jnp.exp2(loc)]
        k_lv = [(k * jnp.exp2(-loc)).astype(BF16)]
        blk = 2 * GLA_REF_CHUNK
        while blk <= C:
            mid = _block_row(excl if forward else cum, blk, blk // 2)
            e = jnp.exp2(-jnp.abs(cum - mid))
            q_lv.append(q * e)
            k_lv.append((k * e).astype(BF16))
            blk *= 2
        q_in = (q * jnp.exp2(cum)).astype(BF16)
        k_out = k * jnp.exp2(total - cum)
        k_out_t = k_out.T.astype(BF16)
        decay_t = jnp.exp2(total).T
        factors.append((q_lv, k_lv, q_in, k_out_t, decay_t))
    yield

    def stack_heads(a):
        return jnp.concatenate([jnp.where(lane_lo, a, 0.0), jnp.where(lane_lo, 0.0, a)], axis=0).astype(BF16)

    xr2 = jnp.concatenate([xr, xr], axis=0)
    valid2 = jnp.concatenate([valid, valid], axis=0)
    scores = []
    for pair in pairs:
        q_lv, k_lv = factors[pair][:2]
        scores2 = _dot_nt(stack_heads(q_lv[-1]), k_lv[-1])
        bound = C // 2
        for lv in range(len(q_lv) - 2, -1, -1):
            scores2 = jnp.where(xr2 < bound, _dot_nt(stack_heads(q_lv[lv]), k_lv[lv]), scores2)
            bound //= 2
        scores.append(jnp.where(valid2, scores2, 0.0).astype(BF16))
    yield

    for pair in pairs:
        q_in, k_out_t, decay_t = factors[pair][2:]
        v2 = v_ref[bi, :, 2 * pair * GLA_DV:2 * (pair + 1) * GLA_DV]
        state = state_ref[bi, pair]
        inter = _dot(q_in, state.astype(BF16))
        for hh in range(2):
            h = 2 * pair + hh
            cols = slice(hh * GLA_DV, (hh + 1) * GLA_DV)
            o = _dot(scores[pair][hh * C:(hh + 1) * C], v2[:, cols]) + inter[:, cols]
            o_ref[bi, :, h * GLA_DV:(h + 1) * GLA_DV] = o.astype(o_ref.dtype)
        state_ref[bi, pair] = jnp.where(on_diag, decay_t * state + _dot(k_out_t, v2), 0.0)
    yield


def _gla_kernel(qf_ref, kf_ref, vf_ref, laf_ref, qb_ref, kb_ref, vb_ref, lab_ref,
                of_ref, ob_ref, sf_ref, sb_ref):
    @pl.when(pl.program_id(1) == 0)
    def _():
        sf_ref[...] = jnp.zeros(sf_ref.shape, F32)
        sb_ref[...] = jnp.zeros(sb_ref.shape, F32)

    chains = []
    for bi in range(GLA_BATCH):
        chains.append(_gla_direction(bi, qf_ref, kf_ref, vf_ref, laf_ref, of_ref, sf_ref, True))
        chains.append(_gla_direction(bi, qb_ref, kb_ref, vb_ref, lab_ref, ob_ref, sb_ref, False))
    for _ in range(GLA_PHASES):
        for chain in chains:
            next(chain)


def _gla_call(gq, gk, gv, laf, lab):
    B, S, _ = gq.shape
    C, nb = GLA_CHUNK, GLA_BATCH
    n = S // C
    fwd = lambda b, c: (b, c, 0)
    bwd = lambda b, c: (b, n - 1 - c, 0)
    qk = lambda im: pl.BlockSpec((nb, C, GLA_QK_WIDTH), im)
    vv = lambda im: pl.BlockSpec((nb, C, GLA_WIDTH), im)
    state = pltpu.VMEM((nb, GLA_HEADS // 2, LANES, 2 * GLA_DV), F32)
    return pl.pallas_call(
        _gla_kernel, grid=(B // nb, n),
        in_specs=[qk(fwd), qk(fwd), vv(fwd), qk(fwd), qk(bwd), qk(bwd), vv(bwd), qk(bwd)],
        out_specs=[vv(fwd), vv(bwd)],
        out_shape=[jax.ShapeDtypeStruct((B, S, GLA_WIDTH), BF16)] * 2,
        scratch_shapes=[state, state],
        compiler_params=_cparams(("arbitrary", "arbitrary")),
    )(gq, gk, gv, laf, gq, gk, gv, lab)


def _layer_norm(y, g, b):
    mu = jnp.mean(y, axis=-1, keepdims=True)
    d = y - mu
    var = jnp.mean(d * d, axis=-1, keepdims=True)
    return d * lax.rsqrt(var + LN_EPS) * g + b


def _silu(g):
    return g * (1.0 / (1.0 + jnp.exp(-g)))


def _mix_ffn_kernel(attn_ref, of_ref, ob_ref, go_ref, x_ref, wo_ref, gng_ref, ln1g_ref, ln1b_ref,
                    wg_ref, wu_ref, wd_ref, ln2g_ref, ln2b_ref, out_ref, x1_sc, x1b_sc):
    t = pl.program_id(0)
    cur, prev = t % 2, (t + 1) % 2

    @pl.when(t == 0)
    def _():
        x1_sc[1] = jnp.zeros(x1_sc.shape[1:], F32)
        x1b_sc[1] = jnp.zeros(x1b_sc.shape[1:], BF16)

    hidden = []

    def ffn_chunk(c):
        sl = slice(c * FFN_CHUNK, (c + 1) * FFN_CHUNK)
        xb = x1b_sc[prev]
        hidden.append((_silu(_dot(xb, wg_ref[:, sl])) * _dot(xb, wu_ref[:, sl])).astype(BF16))

    def gate_head(h):
        sl = slice(h * GLA_DV, (h + 1) * GLA_DV)
        o = of_ref[:, sl].astype(F32) + ob_ref[:, sl].astype(F32)
        inv = lax.rsqrt(jnp.mean(o * o, axis=-1, keepdims=True) + RMS_EPS)
        return ((o * inv * gng_ref[...]) * _silu(go_ref[:, sl].astype(F32))).astype(BF16)

    n_chunks = D_FF // FFN_CHUNK
    gated = []
    for c in range(GLA_HEADS):
        ffn_chunk(c)
        gated.append(gate_head(c))
    ffn_chunk(GLA_HEADS)
    merged = jnp.concatenate([attn_ref[...]] + gated, axis=-1)
    mixed = _dot(merged, wo_ref[...])
    ffn_chunk(GLA_HEADS + 1)
    ffn_chunk(GLA_HEADS + 2)
    x1 = _layer_norm(DEEPNORM_ALPHA * x_ref[...] + mixed, ln1g_ref[...], ln1b_ref[...])
    x1_sc[cur] = x1
    x1b_sc[cur] = x1.astype(BF16)
    for c in range(GLA_HEADS + 3, n_chunks):
        ffn_chunk(c)
    ffn = _dot(jnp.concatenate(hidden, axis=-1), wd_ref[...])
    out_ref[...] = _layer_norm(DEEPNORM_ALPHA * x1_sc[prev] + ffn, ln2g_ref[...], ln2b_ref[...])


def _mix_ffn_call(attn, o_f, o_b, go, x, w_out, gng, ln1_g, ln1_b, w_gate, w_up, w_down, ln2_g, ln2_b):
    B, S, D = x.shape
    tm = MIX_FFN_TM
    tokens = B * S
    n = tokens // tm
    flat = lambda a: a.reshape(tokens, a.shape[-1])
    merge_tile = lambda t: (jnp.minimum(t, n - 1), 0)
    ffn_tile = lambda t: (jnp.maximum(t - 1, 0), 0)
    const = lambda t: (0, 0)
    resident = lambda shape: pl.BlockSpec(shape, const, pipeline_mode=pl.Buffered(1))
    out = pl.pallas_call(
        _mix_ffn_kernel, grid=(n + 1,),
        in_specs=[
            pl.BlockSpec((tm, ATTN_WIDTH), merge_tile),
            pl.BlockSpec((tm, GLA_WIDTH), merge_tile),
            pl.BlockSpec((tm, GLA_WIDTH), merge_tile),
            pl.BlockSpec((tm, GLA_WIDTH), merge_tile),
            pl.BlockSpec((tm, D), merge_tile),
            resident((ATTN_WIDTH + GLA_WIDTH, D)),
            pl.BlockSpec((1, GLA_DV), const),
            pl.BlockSpec((1, D), const),
            pl.BlockSpec((1, D), const),
            resident((D, D_FF)), resident((D, D_FF)), resident((D_FF, D)),
            pl.BlockSpec((1, D), const),
            pl.BlockSpec((1, D), const),
        ],
        out_specs=pl.BlockSpec((tm, D), ffn_tile),
        out_shape=jax.ShapeDtypeStruct((tokens, D), F32),
        scratch_shapes=[pltpu.VMEM((2, tm, D), F32), pltpu.VMEM((2, tm, D), BF16)],
        compiler_params=_cparams(("arbitrary",)),
    )(flat(attn), flat(o_f), flat(o_b), flat(go), flat(x), w_out, gng, ln1_g, ln1_b,
      w_gate, w_up, w_down, ln2_g, ln2_b)
    return out.reshape(B, S, D)


def _rope_tables(seq_len):
    t = jnp.arange(seq_len, dtype=jnp.int32)
    row_id = (t // GRID_W).astype(F32)
    col_id = (t % GRID_W).astype(F32)
    inv_freq = ROPE_THETA ** (-jnp.arange(0, AXIAL_DIM, 2, dtype=F32) / AXIAL_DIM)
    lane = jnp.arange(LANES, dtype=jnp.int32)
    d = lane % HEAD_DIM
    is_col = (d // AXIAL_DIM) == 1
    upper = ((d % AXIAL_DIM) // ROPE_HALF) == 1
    ang = jnp.where(is_col, col_id[:, None], row_id[:, None]) * jnp.tile(inv_freq, LANES // ROPE_HALF)
    sin = jnp.sin(ang)
    return jnp.cos(ang), jnp.where(upper, sin, 0.0), jnp.where(upper, 0.0, -sin)


def kernel(x, w_in, q_norm_g, k_norm_g, gate_up_fwd, gate_bias_fwd, gate_up_bwd, gate_bias_bwd, gla_norm_g,
           w_out, ln1_g, ln1_b, w_ffn_gate, w_ffn_up, w_ffn_down, ln2_g, ln2_b):
    B, S, D = x.shape
    assert D == D_MODEL and S % max(PROJ_TM, ATTN_TQ, ATTN_TK, GLA_CHUNK, MIX_FFN_TM) == 0 and B % GLA_BATCH == 0
    assert w_in.shape[0] == DEPTH
    cos, sin_hi, sin_lo = _rope_tables(S)
    for layer in range(DEPTH):
        w_pad = jnp.pad(w_in[layer], ((0, 0), (0, PROJ_PAD_WIDTH - w_in.shape[-1]))).astype(BF16)
        gup = jnp.zeros((Z_PAD, 2 * GLA_QK_WIDTH), F32)
        gup = gup.at[:GATE_RANK, :GLA_QK_WIDTH].set(gate_up_fwd[layer])
        gup = gup.at[GATE_RANK:2 * GATE_RANK, GLA_QK_WIDTH:].set(gate_up_bwd[layer]).astype(BF16)
        gbias = jnp.concatenate([gate_bias_fwd[layer], gate_bias_bwd[layer]])[None, :]
        qg = jnp.tile(q_norm_g[layer], LANES // HEAD_DIM)[None, :]
        kg = jnp.tile(k_norm_g[layer], LANES // HEAD_DIM)[None, :]

        qa, ka, va, gq, gk, gv, go, laf, lab = _proj_call(x, w_pad, gup, gbias, qg, kg, cos, sin_hi, sin_lo)
        score_bound = (jnp.max(jnp.abs(q_norm_g[layer])) * jnp.max(jnp.abs(k_norm_g[layer]))
                       * SCORE_BOUND_SCALE).reshape(1).astype(F32)
        attn = _attn_call(score_bound, qa, ka, va)
        o_f, o_b = _gla_call(gq, gk, gv, laf, lab)
        x = _mix_ffn_call(attn, o_f, o_b, go, x, w_out[layer].astype(BF16), gla_norm_g[layer][None, :],
                          ln1_g[layer][None, :], ln1_b[layer][None, :],
                          w_ffn_gate[layer].astype(BF16), w_ffn_up[layer].astype(BF16),
                          w_ffn_down[layer].astype(BF16), ln2_g[layer][None, :], ln2_b[layer][None, :])
    return x
```

```python
import jax
import jax.numpy as jnp
from jax import lax
from jax.experimental import pallas as pl
from jax.experimental.pallas import tpu as pltpu

F32 = jnp.float32
BF16 = jnp.bfloat16

D_MODEL = 1024
GRID_W = 64
N_Q_HEADS = 8
N_KV_HEADS = 2
Q_PER_KV = N_Q_HEADS // N_KV_HEADS
HEAD_DIM = 64
AXIAL_DIM = HEAD_DIM // 2
ROPE_HALF = AXIAL_DIM // 2
ROPE_THETA = 10000.0
GLA_HEADS = 4
GLA_DK = 64
GLA_DV = 128
GATE_RANK = 16
GATE_TAU = 16.0
GLA_REF_CHUNK = 16
ATTN_WIDTH = N_Q_HEADS * HEAD_DIM
KV_WIDTH = N_KV_HEADS * HEAD_DIM
GLA_QK_WIDTH = GLA_HEADS * GLA_DK
GLA_WIDTH = GLA_HEADS * GLA_DV
D_FF = 2816
DEPTH = 1
DEEPNORM_ALPHA = (2 * DEPTH) ** 0.25
LN_EPS = 1e-5
RMS_EPS = 1e-6
LOG2_E = 1.4426950408889634

LANES = 128
SUBLANES = 8
VMEM_LIMIT_BYTES = 56 * 1024 * 1024

PROJ_TM = 512
ATTN_TQ = 512
ATTN_TK = 256
ATTN_COL = 512
VT_ROWS = HEAD_DIM + 16
SCORE_BOUND_SCALE = LOG2_E * HEAD_DIM ** 0.5 * 1.01
ATTN_UNSHIFTED_MAX_LOG2 = 60.0
GLA_CHUNK = 128
GLA_BATCH = 4
GLA_PHASES = 4
MIX_FFN_TM = 512
FFN_CHUNK = 256

OFF_AQ = 0
OFF_AK = OFF_AQ + ATTN_WIDTH
OFF_AV = OFF_AK + KV_WIDTH
OFF_GQ = OFF_AV + KV_WIDTH
OFF_GK = OFF_GQ + GLA_QK_WIDTH
OFF_GV = OFF_GK + GLA_QK_WIDTH
OFF_GO = OFF_GV + GLA_WIDTH
OFF_Z = OFF_GO + GLA_WIDTH
Z_PAD = LANES
PROJ_PAD_WIDTH = OFF_Z + Z_PAD


def _cparams(semantics):
    return pltpu.CompilerParams(dimension_semantics=semantics, vmem_limit_bytes=VMEM_LIMIT_BYTES)


def _dot(a, b):
    return jnp.dot(a, b, preferred_element_type=F32)


def _dot_nt(a, b):
    return lax.dot_general(a, b, (((1,), (1,)), ((), ())), preferred_element_type=F32)


def _rope128(y, cos, sin_hi, sin_lo):
    return y * cos + pltpu.roll(y, ROPE_HALF, 1) * sin_hi + pltpu.roll(y, LANES - ROPE_HALF, 1) * sin_lo


def _head_pair_inv_rms(blk, lane_lo):
    sq = blk * blk
    ss_lo = jnp.sum(jnp.where(lane_lo, sq, 0.0), axis=-1, keepdims=True)
    ss_hi = jnp.sum(jnp.where(lane_lo, 0.0, sq), axis=-1, keepdims=True)
    inv = 1.0 / HEAD_DIM
    return jnp.where(lane_lo, lax.rsqrt(ss_lo * inv + RMS_EPS), lax.rsqrt(ss_hi * inv + RMS_EPS))


def _proj_kernel(x_ref, w_ref, gup_ref, gbias_ref, qg_ref, kg_ref, cos_ref, shi_ref, slo_ref,
                 qa_ref, ka_ref, va_ref, gq_ref, gk_ref, gv_ref, go_ref, laf_ref, lab_ref,
                 aq_sc, akv_sc, z_sc):
    t = pl.program_id(0)
    cur, prev = t % 2, (t + 1) % 2

    @pl.when(t == 0)
    def _():
        aq_sc[1] = jnp.zeros(aq_sc.shape[1:], F32)
        akv_sc[1] = jnp.zeros(akv_sc.shape[1:], F32)
        z_sc[1] = jnp.zeros(z_sc.shape[1:], F32)

    x = x_ref[0].astype(BF16)
    tm = x.shape[0]
    lane = lax.broadcasted_iota(jnp.int32, (tm, LANES), 1)
    lane_lo = lane < HEAD_DIM
    cos, shi, slo = cos_ref[...], shi_ref[...], slo_ref[...]

    def proj(off, width):
        return _dot(x, w_ref[:, off:off + width])

    def q_epilogue(c):
        blk = aq_sc[prev, :, c * LANES:(c + 1) * LANES]
        rot = _rope128(blk * qg_ref[...], cos, shi, slo)
        out_t = (rot * (_head_pair_inv_rms(blk, lane_lo) * (LOG2_E * HEAD_DIM ** -0.5))).T.astype(BF16)
        qa_ref[0, 2 * c] = out_t[:HEAD_DIM]
        qa_ref[0, 2 * c + 1] = out_t[HEAD_DIM:]

    def k_epilogue():
        ak = akv_sc[prev, :, :KV_WIDTH]
        rot = _rope128(ak * kg_ref[...], cos, shi, slo)
        out = (rot * _head_pair_inv_rms(ak, lane_lo)).astype(BF16)
        ka_ref[0, 0] = out[:, :HEAD_DIM]
        ka_ref[0, 1] = out[:, HEAD_DIM:]

    def v_epilogue():
        av = akv_sc[prev, :, KV_WIDTH:]
        ones_col = jnp.where(lane == HEAD_DIM, 1.0, 0.0)
        va_ref[0, 0] = jnp.where(lane_lo, av, ones_col).T.astype(BF16)
        va_ref[0, 1] = jnp.where(lane_lo, pltpu.roll(av, HEAD_DIM, 1), ones_col).T.astype(BF16)

    def gate_epilogue():
        g = _dot(z_sc[prev].astype(BF16), gup_ref[...]) + gbias_ref[...]
        log2_a = (jnp.minimum(g, 0.0) - jnp.log(1.0 + jnp.exp(-jnp.abs(g)))) * (LOG2_E / GATE_TAU)
        laf_ref[0] = log2_a[:, :GLA_QK_WIDTH]
        lab_ref[0] = log2_a[:, GLA_QK_WIDTH:]

    half = GLA_WIDTH // 2
    gq_ref[0] = (proj(OFF_GQ, GLA_QK_WIDTH) * (GLA_DK ** -0.5)).astype(BF16)
    q_epilogue(0)
    gk_ref[0] = proj(OFF_GK, GLA_QK_WIDTH).astype(BF16)
    q_epilogue(1)
    gv_ref[0, :, :half] = proj(OFF_GV, half).astype(BF16)
    q_epilogue(2)
    gv_ref[0, :, half:] = proj(OFF_GV + half, half).astype(BF16)
    q_epilogue(3)
    go_ref[0, :, :half] = proj(OFF_GO, half).astype(BF16)
    k_epilogue()
    go_ref[0, :, half:] = proj(OFF_GO + half, half).astype(BF16)
    v_epilogue()
    gate_epilogue()
    half_q = ATTN_WIDTH // 2
    aq_sc[cur, :, :half_q] = proj(OFF_AQ, half_q)
    aq_sc[cur, :, half_q:] = proj(OFF_AQ + half_q, half_q)
    akv_sc[cur] = proj(OFF_AK, 2 * KV_WIDTH)
    z_sc[cur] = proj(OFF_Z, Z_PAD)


def _proj_call(x, w_pad, gup, gbias, qg, kg, cos, shi, slo):
    B, S, D = x.shape
    tm = PROJ_TM
    n_s = S // tm
    n = n_s * B
    mm = lambda t: jnp.minimum(t, n - 1)
    ep = lambda t: jnp.maximum(t - 1, 0)
    tok = lambda t: (mm(t) % B, mm(t) // B, 0)
    tok_ep = lambda t: (ep(t) % B, ep(t) // B, 0)
    head = lambda t: (ep(t) % B, 0, ep(t) // B, 0)
    head_t = lambda t: (ep(t) % B, 0, 0, ep(t) // B)
    const = lambda t: (0, 0)
    tab = lambda t: (ep(t) // B, 0)
    in_specs = [
        pl.BlockSpec((1, tm, D), tok),
        pl.BlockSpec((D, PROJ_PAD_WIDTH), const),
        pl.BlockSpec((Z_PAD, 2 * GLA_QK_WIDTH), const),
        pl.BlockSpec((1, 2 * GLA_QK_WIDTH), const),
        pl.BlockSpec((1, LANES), const),
        pl.BlockSpec((1, LANES), const),
        pl.BlockSpec((tm, LANES), tab),
        pl.BlockSpec((tm, LANES), tab),
        pl.BlockSpec((tm, LANES), tab),
    ]
    out_shape = [
        jax.ShapeDtypeStruct((B, N_Q_HEADS, HEAD_DIM, S), BF16),
        jax.ShapeDtypeStruct((B, N_KV_HEADS, S, HEAD_DIM), BF16),
        jax.ShapeDtypeStruct((B, N_KV_HEADS, LANES, S), BF16),
        jax.ShapeDtypeStruct((B, S, GLA_QK_WIDTH), BF16),
        jax.ShapeDtypeStruct((B, S, GLA_QK_WIDTH), BF16),
        jax.ShapeDtypeStruct((B, S, GLA_WIDTH), BF16),
        jax.ShapeDtypeStruct((B, S, GLA_WIDTH), BF16),
        jax.ShapeDtypeStruct((B, S, GLA_QK_WIDTH), F32),
        jax.ShapeDtypeStruct((B, S, GLA_QK_WIDTH), F32),
    ]
    out_specs = [
        pl.BlockSpec((1, N_Q_HEADS, HEAD_DIM, tm), head_t),
        pl.BlockSpec((1, N_KV_HEADS, tm, HEAD_DIM), head),
        pl.BlockSpec((1, N_KV_HEADS, LANES, tm), head_t),
        pl.BlockSpec((1, tm, GLA_QK_WIDTH), tok),
        pl.BlockSpec((1, tm, GLA_QK_WIDTH), tok),
        pl.BlockSpec((1, tm, GLA_WIDTH), tok),
        pl.BlockSpec((1, tm, GLA_WIDTH), tok),
        pl.BlockSpec((1, tm, GLA_QK_WIDTH), tok_ep),
        pl.BlockSpec((1, tm, GLA_QK_WIDTH), tok_ep),
    ]
    scratch = [pltpu.VMEM((2, tm, ATTN_WIDTH), F32), pltpu.VMEM((2, tm, 2 * KV_WIDTH), F32),
               pltpu.VMEM((2, tm, Z_PAD), F32)]
    return pl.pallas_call(
        _proj_kernel, grid=(n + 1,), in_specs=in_specs, out_specs=out_specs, out_shape=out_shape,
        scratch_shapes=scratch, compiler_params=_cparams(("arbitrary",)),
    )(x, w_pad, gup, gbias, qg, kg, cos, shi, slo)


def _attn_kernel(bound_ref, qt_ref, k_ref, vt_ref, o_ref, acc_sc, den_sc):
    t = pl.program_id(0)
    cur, prev = t % 2, (t + 1) % 2
    tq = qt_ref.shape[3]
    n_keys = k_ref.shape[2]
    tk = ATTN_TK
    q_t = jnp.concatenate([qt_ref[0, r] for r in range(Q_PER_KV)], axis=1)
    unshifted_is_safe = bound_ref[0] < ATTN_UNSHIFTED_MAX_LOG2

    @pl.when(t == 0)
    def _():
        acc_sc[1] = jnp.zeros(acc_sc.shape[1:], F32)
        den_sc[1] = jnp.ones(den_sc.shape[1:], F32)

    def finish_previous():
        o_t = acc_sc[prev] / den_sc[prev]
        for r in range(Q_PER_KV):
            o_ref[0, :, r * HEAD_DIM:(r + 1) * HEAD_DIM] = o_t[:, r * tq:(r + 1) * tq].T.astype(o_ref.dtype)

    def finish(acc, den_row):
        acc_sc[cur] = acc[:HEAD_DIM]
        den_sc[cur] = den_row

    @pl.when(unshifted_is_safe)
    def _():
        n = n_keys // tk
        n_col = q_t.shape[1] // ATTN_COL
        col = lambda c: slice(c * ATTN_COL, (c + 1) * ATTN_COL)
        scores = lambda j, c: _dot(k_ref[0, 0, j * tk:(j + 1) * tk, :], q_t[:, col(c)])
        acc = [None] * n_col
        den = [None] * n_col
        s_cur = [scores(0, c) for c in range(n_col)]
        for j in range(n):
            s_nxt = []
            for c in range(n_col):
                p = jnp.exp2(s_cur[c])
                pv = _dot(vt_ref[0, 0, :HEAD_DIM, j * tk:(j + 1) * tk], p.astype(BF16))
                acc[c] = pv if acc[c] is None else acc[c] + pv
                part = jnp.sum(p.reshape(tk // SUBLANES, SUBLANES, ATTN_COL), axis=0)
                den[c] = part if den[c] is None else den[c] + part
                if j + 1 < n:
                    s_nxt.append(scores(j + 1, c))
            s_cur = s_nxt
            if j == 0:
                finish_previous()
        den_row = jnp.sum(jnp.concatenate(den, axis=1), axis=0, keepdims=True)
        finish(jnp.concatenate(acc, axis=1), den_row)

    @pl.when(jnp.logical_not(unshifted_is_safe))
    def _():
        finish_previous()
        m = None
        acc = None
        for j in range(n_keys // tk):
            s_t = _dot(k_ref[0, 0, j * tk:(j + 1) * tk, :], q_t)
            m_blk = jnp.max(s_t, axis=0, keepdims=True)
            m_new = m_blk if m is None else jnp.maximum(m, m_blk)
            pv = _dot(vt_ref[0, 0, :VT_ROWS, j * tk:(j + 1) * tk], jnp.exp2(s_t - m_new).astype(BF16))
            acc = pv if acc is None else jnp.exp2(m - m_new) * acc + pv
            m = m_new
        finish(acc, acc[HEAD_DIM:HEAD_DIM + 1])


def _attn_call(score_bound, qa_t, ka, va_t):
    B, _, S, _ = ka.shape
    tq = ATTN_TQ
    nq = S // tq
    n = B * N_KV_HEADS * nq
    acc_blk = lambda t: jnp.minimum(t, n - 1)
    out_blk = lambda t: jnp.maximum(t - 1, 0)
    b_of = lambda u: u // (N_KV_HEADS * nq)
    g_of = lambda u: (u // nq) % N_KV_HEADS
    return pl.pallas_call(
        _attn_kernel, grid=(n + 1,),
        in_specs=[
            pl.BlockSpec(memory_space=pltpu.SMEM),
            pl.BlockSpec((1, Q_PER_KV, HEAD_DIM, tq), lambda t: (b_of(acc_blk(t)), g_of(acc_blk(t)), 0, acc_blk(t) % nq)),
            pl.BlockSpec((1, 1, S, HEAD_DIM), lambda t: (b_of(acc_blk(t)), g_of(acc_blk(t)), 0, 0)),
            pl.BlockSpec((1, 1, LANES, S), lambda t: (b_of(acc_blk(t)), g_of(acc_blk(t)), 0, 0)),
        ],
        out_specs=pl.BlockSpec((1, tq, Q_PER_KV * HEAD_DIM),
                               lambda t: (b_of(out_blk(t)), out_blk(t) % nq, g_of(out_blk(t)))),
        out_shape=jax.ShapeDtypeStruct((B, S, ATTN_WIDTH), BF16),
        scratch_shapes=[pltpu.VMEM((2, HEAD_DIM, Q_PER_KV * tq), F32), pltpu.VMEM((2, 1, Q_PER_KV * tq), F32)],
        compiler_params=_cparams(("arbitrary",)),
    )(score_bound, qa_t, ka, va_t)


def _block_row(a, blk, row):
    n, w = a.shape
    a3 = a.reshape(n // blk, blk, w)
    return jnp.broadcast_to(a3[:, row:row + 1, :], a3.shape).reshape(n, w)


def _split3_bf16(a):
    hi = a.astype(BF16)
    r1 = a - hi.astype(F32)
    mid = r1.astype(BF16)
    lo = (r1 - mid.astype(F32)).astype(BF16)
    return hi, mid, lo


def _gla_direction(bi, q_ref, k_ref, v_ref, la_ref, o_ref, state_ref, forward):
    C = q_ref.shape[1]
    row = lax.broadcasted_iota(jnp.int32, (C, C), 0)
    col = lax.broadcasted_iota(jnp.int32, (C, C), 1)
    tri = ((col <= row) if forward else (col >= row)).astype(BF16)
    la = la_ref[bi]
    cum_all = sum(_dot(tri, part) for part in _split3_bf16(la))
    excl_all = cum_all - la
    xr = row ^ col
    valid = (col <= row) if forward else (col > row)
    lane = lax.broadcasted_iota(jnp.int32, (C, LANES), 1)
    lane_lo = lane < GLA_DK
    edge = C - 1 if forward else 0
    srow = lax.broadcasted_iota(jnp.int32, (LANES, 2 * GLA_DV), 0)
    scol = lax.broadcasted_iota(jnp.int32, (LANES, 2 * GLA_DV), 1)
    on_diag = (srow < GLA_DK) == (scol < GLA_DV)
    pairs = range(GLA_HEADS // 2)
    yield

    factors = []
    for pair in pairs:
        sl = slice(pair * LANES, (pair + 1) * LANES)
        q, k = q_ref[bi, :, sl].astype(F32), k_ref[bi, :, sl].astype(F32)
        cum, excl = cum_all[:, sl], excl_all[:, sl]
        total = cum[edge:edge + 1, :]

        base_row = 0 if forward else GLA_REF_CHUNK - 1
        loc = cum - _block_row(excl, GLA_REF_CHUNK, base_row)
        q_lv = [q * jnp.exp2(loc)]
        k_lv = [(k * jnp.exp2(-loc)).astype(BF16)]
        blk = 2 * GLA_REF_CHUNK
        while blk <= C:
            mid = _block_row(excl if forward else cum, blk, blk // 2)
            e = jnp.exp2(-jnp.abs(cum - mid))
            q_lv.append(q * e)
            k_lv.append((k * e).astype(BF16))
            blk *= 2
        q_in = (q * jnp.exp2(cum)).astype(BF16)
        k_out = k * jnp.exp2(total - cum)
        k_out_t = k_out.T.astype(BF16)
        decay_t = jnp.exp2(total).T
        factors.append((q_lv, k_lv, q_in, k_out_t, decay_t))
    yield

    def stack_heads(a):
        return jnp.concatenate([jnp.where(lane_lo, a, 0.0), jnp.where(lane_lo, 0.0, a)], axis=0).astype(BF16)

    xr2 = jnp.concatenate([xr, xr], axis=0)
    valid2 = jnp.concatenate([valid, valid], axis=0)
    scores = []
    for pair in pairs:
        q_lv, k_lv = factors[pair][:2]
        scores2 = _dot_nt(stack_heads(q_lv[-1]), k_lv[-1])
        bound = C // 2
        for lv in range(len(q_lv) - 2, -1, -1):
            scores2 = jnp.where(xr2 < bound, _dot_nt(stack_heads(q_lv[lv]), k_lv[lv]), scores2)
            bound //= 2
        scores.append(jnp.where(valid2, scores2, 0.0).astype(BF16))
    yield

    for pair in pairs:
        q_in, k_out_t, decay_t = factors[pair][2:]
        v2 = v_ref[bi, :, 2 * pair * GLA_DV:2 * (pair + 1) * GLA_DV]
        state = state_ref[bi, pair]
        inter = _dot(q_in, state.astype(BF16))
        for hh in range(2):
            h = 2 * pair + hh
            cols = slice(hh * GLA_DV, (hh + 1) * GLA_DV)
            o = _dot(scores[pair][hh * C:(hh + 1) * C], v2[:, cols]) + inter[:, cols]
            o_ref[bi, :, h * GLA_DV:(h + 1) * GLA_DV] = o.astype(o_ref.dtype)
        state_ref[bi, pair] = jnp.where(on_diag, decay_t * state + _dot(k_out_t, v2), 0.0)
    yield


def _gla_kernel(qf_ref, kf_ref, vf_ref, laf_ref, qb_ref, kb_ref, vb_ref, lab_ref,
                of_ref, ob_ref, sf_ref, sb_ref):
    @pl.when(pl.program_id(1) == 0)
    def _():
        sf_ref[...] = jnp.zeros(sf_ref.shape, F32)
        sb_ref[...] = jnp.zeros(sb_ref.shape, F32)

    chains = []
    for bi in range(GLA_BATCH):
        chains.append(_gla_direction(bi, qf_ref, kf_ref, vf_ref, laf_ref, of_ref, sf_ref, True))
        chains.append(_gla_direction(bi, qb_ref, kb_ref, vb_ref, lab_ref, ob_ref, sb_ref, False))
    for _ in range(GLA_PHASES):
        for chain in chains:
            next(chain)


def _gla_call(gq, gk, gv, laf, lab):
    B, S, _ = gq.shape
    C, nb = GLA_CHUNK, GLA_BATCH
    n = S // C
    fwd = lambda b, c: (b, c, 0)
    bwd = lambda b, c: (b, n - 1 - c, 0)
    qk = lambda im: pl.BlockSpec((nb, C, GLA_QK_WIDTH), im)
    vv = lambda im: pl.BlockSpec((nb, C, GLA_WIDTH), im)
    state = pltpu.VMEM((nb, GLA_HEADS // 2, LANES, 2 * GLA_DV), F32)
    return pl.pallas_call(
        _gla_kernel, grid=(B // nb, n),
        in_specs=[qk(fwd), qk(fwd), vv(fwd), qk(fwd), qk(bwd), qk(bwd), vv(bwd), qk(bwd)],
        out_specs=[vv(fwd), vv(bwd)],
        out_shape=[jax.ShapeDtypeStruct((B, S, GLA_WIDTH), BF16)] * 2,
        scratch_shapes=[state, state],
        compiler_params=_cparams(("arbitrary", "arbitrary")),
    )(gq, gk, gv, laf, gq, gk, gv, lab)


def _layer_norm(y, g, b):
    mu = jnp.mean(y, axis=-1, keepdims=True)
    d = y - mu
    var = jnp.mean(d * d, axis=-1, keepdims=True)
    return d * lax.rsqrt(var + LN_EPS) * g + b


def _silu(g):
    return g * (1.0 / (1.0 + jnp.exp(-g)))


def _mix_ffn_kernel(attn_ref, of_ref, ob_ref, go_ref, x_ref, wo_ref, gng_ref, ln1g_ref, ln1b_ref,
                    wg_ref, wu_ref, wd_ref, ln2g_ref, ln2b_ref, out_ref, x1_sc, x1b_sc):
    t = pl.program_id(0)
    cur, prev = t % 2, (t + 1) % 2

    @pl.when(t == 0)
    def _():
        x1_sc[1] = jnp.zeros(x1_sc.shape[1:], F32)
        x1b_sc[1] = jnp.zeros(x1b_sc.shape[1:], BF16)

    hidden = []

    def ffn_chunk(c):
        sl = slice(c * FFN_CHUNK, (c + 1) * FFN_CHUNK)
        xb = x1b_sc[prev]
        hidden.append((_silu(_dot(xb, wg_ref[:, sl])) * _dot(xb, wu_ref[:, sl])).astype(BF16))

    def gate_head(h):
        sl = slice(h * GLA_DV, (h + 1) * GLA_DV)
        o = of_ref[:, sl].astype(F32) + ob_ref[:, sl].astype(F32)
        inv = lax.rsqrt(jnp.mean(o * o, axis=-1, keepdims=True) + RMS_EPS)
        return ((o * inv * gng_ref[...]) * _silu(go_ref[:, sl].astype(F32))).astype(BF16)

    n_chunks = D_FF // FFN_CHUNK
    gated = []
    for c in range(GLA_HEADS):
        ffn_chunk(c)
        gated.append(gate_head(c))
    ffn_chunk(GLA_HEADS)
    merged = jnp.concatenate([attn_ref[...]] + gated, axis=-1)
    mixed = _dot(merged, wo_ref[...])
    ffn_chunk(GLA_HEADS + 1)
    ffn_chunk(GLA_HEADS + 2)
    x1 = _layer_norm(DEEPNORM_ALPHA * x_ref[...] + mixed, ln1g_ref[...], ln1b_ref[...])
    x1_sc[cur] = x1
    x1b_sc[cur] = x1.astype(BF16)
    for c in range(GLA_HEADS + 3, n_chunks):
        ffn_chunk(c)
    ffn = _dot(jnp.concatenate(hidden, axis=-1), wd_ref[...])
    out_ref[...] = _layer_norm(DEEPNORM_ALPHA * x1_sc[prev] + ffn, ln2g_ref[...], ln2b_ref[...])


def _mix_ffn_call(attn, o_f, o_b, go, x, w_out, gng, ln1_g, ln1_b, w_gate, w_up, w_down, ln2_g, ln2_b):
    B, S, D = x.shape
    tm = MIX_FFN_TM
    tokens = B * S
    n = tokens // tm
    flat = lambda a: a.reshape(tokens, a.shape[-1])
    merge_tile = lambda t: (jnp.minimum(t, n - 1), 0)
    ffn_tile = lambda t: (jnp.maximum(t - 1, 0), 0)
    const = lambda t: (0, 0)
    resident = lambda shape: pl.BlockSpec(shape, const, pipeline_mode=pl.Buffered(1))
    out = pl.pallas_call(
        _mix_ffn_kernel, grid=(n + 1,),
        in_specs=[
            pl.BlockSpec((tm, ATTN_WIDTH), merge_tile),
            pl.BlockSpec((tm, GLA_WIDTH), merge_tile),
            pl.BlockSpec((tm, GLA_WIDTH), merge_tile),
            pl.BlockSpec((tm, GLA_WIDTH), merge_tile),
            pl.BlockSpec((tm, D), merge_tile),
            resident((ATTN_WIDTH + GLA_WIDTH, D)),
            pl.BlockSpec((1, GLA_DV), const),
            pl.BlockSpec((1, D), const),
            pl.BlockSpec((1, D), const),
            resident((D, D_FF)), resident((D, D_FF)), resident((D_FF, D)),
            pl.BlockSpec((1, D), const),
            pl.BlockSpec((1, D), const),
        ],
        out_specs=pl.BlockSpec((tm, D), ffn_tile),
        out_shape=jax.ShapeDtypeStruct((tokens, D), F32),
        scratch_shapes=[pltpu.VMEM((2, tm, D), F32), pltpu.VMEM((2, tm, D), BF16)],
        compiler_params=_cparams(("arbitrary",)),
    )(flat(attn), flat(o_f), flat(o_b), flat(go), flat(x), w_out, gng, ln1_g, ln1_b,
      w_gate, w_up, w_down, ln2_g, ln2_b)
    return out.reshape(B, S, D)


def _rope_tables(seq_len):
    t = jnp.arange(seq_len, dtype=jnp.int32)
    row_id = (t // GRID_W).astype(F32)
    col_id = (t % GRID_W).astype(F32)
    inv_freq = ROPE_THETA ** (-jnp.arange(0, AXIAL_DIM, 2, dtype=F32) / AXIAL_DIM)
    lane = jnp.arange(LANES, dtype=jnp.int32)
    d = lane % HEAD_DIM
    is_col = (d // AXIAL_DIM) == 1
    upper = ((d % AXIAL_DIM) // ROPE_HALF) == 1
    ang = jnp.where(is_col, col_id[:, None], row_id[:, None]) * jnp.tile(inv_freq, LANES // ROPE_HALF)
    sin = jnp.sin(ang)
    return jnp.cos(ang), jnp.where(upper, sin, 0.0), jnp.where(upper, 0.0, -sin)


def kernel(x, w_in, q_norm_g, k_norm_g, gate_up_fwd, gate_bias_fwd, gate_up_bwd, gate_bias_bwd, gla_norm_g,
           w_out, ln1_g, ln1_b, w_ffn_gate, w_ffn_up, w_ffn_down, ln2_g, ln2_b):
    B, S, D = x.shape
    assert D == D_MODEL and S % max(PROJ_TM, ATTN_TQ, ATTN_TK, GLA_CHUNK, MIX_FFN_TM) == 0 and B % GLA_BATCH == 0
    assert w_in.shape[0] == DEPTH
    cos, sin_hi, sin_lo = _rope_tables(S)
    for layer in range(DEPTH):
        w_pad = jnp.pad(w_in[layer], ((0, 0), (0, PROJ_PAD_WIDTH - w_in.shape[-1]))).astype(BF16)
        gup = jnp.zeros((Z_PAD, 2 * GLA_QK_WIDTH), F32)
        gup = gup.at[:GATE_RANK, :GLA_QK_WIDTH].set(gate_up_fwd[layer])
        gup = gup.at[GATE_RANK:2 * GATE_RANK, GLA_QK_WIDTH:].set(gate_up_bwd[layer]).astype(BF16)
        gbias = jnp.concatenate([gate_bias_fwd[layer], gate_bias_bwd[layer]])[None, :]
        qg = jnp.tile(q_norm_g[layer], LANES // HEAD_DIM)[None, :]
        kg = jnp.tile(k_norm_g[layer], LANES // HEAD_DIM)[None, :]

        qa, ka, va, gq, gk, gv, go, laf, lab = _proj_call(x, w_pad, gup, gbias, qg, kg, cos, sin_hi, sin_lo)
        score_bound = (jnp.max(jnp.abs(q_norm_g[layer])) * jnp.max(jnp.abs(k_norm_g[layer]))
                       * SCORE_BOUND_SCALE).reshape(1).astype(F32)
        attn = _attn_call(score_bound, qa, ka, va)
        o_f, o_b = _gla_call(gq, gk, gv, laf, lab)
        x = _mix_ffn_call(attn, o_f, o_b, go, x, w_out[layer].astype(BF16), gla_norm_g[layer][None, :],
                          ln1_g[layer][None, :], ln1_b[layer][None, :],
                          w_ffn_gate[layer].astype(BF16), w_ffn_up[layer].astype(BF16),
                          w_ffn_down[layer].astype(BF16), ln2_g[layer][None, :], ln2_b[layer][None, :])
    return x
```

```python
import jax
import jax.numpy as jnp
from jax import lax
from jax.experimental import pallas as pl
from jax.experimental.pallas import tpu as pltpu

F32 = jnp.float32
BF16 = jnp.bfloat16

D_MODEL = 1024
GRID_W = 64
N_Q_HEADS = 8
N_KV_HEADS = 2
Q_PER_KV = N_Q_HEADS // N_KV_HEADS
HEAD_DIM = 64
AXIAL_DIM = HEAD_DIM // 2
ROPE_HALF = AXIAL_DIM // 2
ROPE_THETA = 10000.0
GLA_HEADS = 4
GLA_DK = 64
GLA_DV = 128
GATE_RANK = 16
GATE_TAU = 16.0
GLA_REF_CHUNK = 16
ATTN_WIDTH = N_Q_HEADS * HEAD_DIM
KV_WIDTH = N_KV_HEADS * HEAD_DIM
GLA_QK_WIDTH = GLA_HEADS * GLA_DK
GLA_WIDTH = GLA_HEADS * GLA_DV
D_FF = 2816
DEPTH = 1
DEEPNORM_ALPHA = (2 * DEPTH) ** 0.25
LN_EPS = 1e-5
RMS_EPS = 1e-6
LOG2_E = 1.4426950408889634

LANES = 128
SUBLANES = 8
VMEM_LIMIT_BYTES = 56 * 1024 * 1024

PROJ_TM = 512
ATTN_TQ = 1024
ATTN_TK = 256
ATTN_COL = 512
VT_ROWS = HEAD_DIM + 16
SCORE_BOUND_SCALE = LOG2_E * HEAD_DIM ** 0.5 * 1.01
ATTN_UNSHIFTED_MAX_LOG2 = 60.0
GLA_CHUNK = 128
GLA_BATCH = 8
GLA_PHASES = 4
MIX_FFN_TM = 512
FFN_CHUNK = 256

OFF_AQ = 0
OFF_AK = OFF_AQ + ATTN_WIDTH
OFF_AV = OFF_AK + KV_WIDTH
OFF_GQ = OFF_AV + KV_WIDTH
OFF_GK = OFF_GQ + GLA_QK_WIDTH
OFF_GV = OFF_GK + GLA_QK_WIDTH
OFF_GO = OFF_GV + GLA_WIDTH
OFF_Z = OFF_GO + GLA_WIDTH
Z_PAD = LANES
PROJ_PAD_WIDTH = OFF_Z + Z_PAD


def _cparams(semantics):
    return pltpu.CompilerParams(dimension_semantics=semantics, vmem_limit_bytes=VMEM_LIMIT_BYTES)


def _dot(a, b):
    return jnp.dot(a, b, preferred_element_type=F32)


def _dot_nt(a, b):
    return lax.dot_general(a, b, (((1,), (1,)), ((), ())), preferred_element_type=F32)


def _rope128(y, cos, sin_hi, sin_lo):
    return y * cos + pltpu.roll(y, ROPE_HALF, 1) * sin_hi + pltpu.roll(y, LANES - ROPE_HALF, 1) * sin_lo


def _head_pair_inv_rms(blk, lane_lo):
    sq = blk * blk
    ss_lo = jnp.sum(jnp.where(lane_lo, sq, 0.0), axis=-1, keepdims=True)
    ss_hi = jnp.sum(jnp.where(lane_lo, 0.0, sq), axis=-1, keepdims=True)
    inv = 1.0 / HEAD_DIM
    return jnp.where(lane_lo, lax.rsqrt(ss_lo * inv + RMS_EPS), lax.rsqrt(ss_hi * inv + RMS_EPS))


def _proj_kernel(x_ref, w_ref, gup_ref, gbias_ref, qg_ref, kg_ref, cos_ref, shi_ref, slo_ref,
                 qa_ref, ka_ref, va_ref, gq_ref, gk_ref, gv_ref, go_ref, laf_ref, lab_ref,
                 aq_sc, akv_sc, z_sc):
    t = pl.program_id(0)
    cur, prev = t % 2, (t + 1) % 2

    @pl.when(t == 0)
    def _():
        aq_sc[1] = jnp.zeros(aq_sc.shape[1:], F32)
        akv_sc[1] = jnp.zeros(akv_sc.shape[1:], F32)
        z_sc[1] = jnp.zeros(z_sc.shape[1:], F32)

    x = x_ref[0].astype(BF16)
    tm = x.shape[0]
    lane = lax.broadcasted_iota(jnp.int32, (tm, LANES), 1)
    lane_lo = lane < HEAD_DIM
    cos, shi, slo = cos_ref[...], shi_ref[...], slo_ref[...]

    def proj(off, width):
        return _dot(x, w_ref[:, off:off + width])

    def q_epilogue(c):
        blk = aq_sc[prev, :, c * LANES:(c + 1) * LANES]
        rot = _rope128(blk * qg_ref[...], cos, shi, slo)
        out_t = (rot * (_head_pair_inv_rms(blk, lane_lo) * (LOG2_E * HEAD_DIM ** -0.5))).T.astype(BF16)
        qa_ref[0, 2 * c] = out_t[:HEAD_DIM]
        qa_ref[0, 2 * c + 1] = out_t[HEAD_DIM:]

    def k_epilogue():
        ak = akv_sc[prev, :, :KV_WIDTH]
        rot = _rope128(ak * kg_ref[...], cos, shi, slo)
        out = (rot * _head_pair_inv_rms(ak, lane_lo)).astype(BF16)
        ka_ref[0, 0] = out[:, :HEAD_DIM]
        ka_ref[0, 1] = out[:, HEAD_DIM:]

    def v_epilogue():
        av = akv_sc[prev, :, KV_WIDTH:]
        ones_col = jnp.where(lane == HEAD_DIM, 1.0, 0.0)
        va_ref[0, 0] = jnp.where(lane_lo, av, ones_col).T.astype(BF16)
        va_ref[0, 1] = jnp.where(lane_lo, pltpu.roll(av, HEAD_DIM, 1), ones_col).T.astype(BF16)

    def gate_epilogue():
        g = _dot(z_sc[prev].astype(BF16), gup_ref[...]) + gbias_ref[...]
        log2_a = (jnp.minimum(g, 0.0) - jnp.log(1.0 + jnp.exp(-jnp.abs(g)))) * (LOG2_E / GATE_TAU)
        laf_ref[0] = log2_a[:, :GLA_QK_WIDTH]
        lab_ref[0] = log2_a[:, GLA_QK_WIDTH:]

    half = GLA_WIDTH // 2
    gq_ref[0] = (proj(OFF_GQ, GLA_QK_WIDTH) * (GLA_DK ** -0.5)).astype(BF16)
    q_epilogue(0)
    gk_ref[0] = proj(OFF_GK, GLA_QK_WIDTH).astype(BF16)
    q_epilogue(1)
    gv_ref[0, :, :half] = proj(OFF_GV, half).astype(BF16)
    q_epilogue(2)
    gv_ref[0, :, half:] = proj(OFF_GV + half, half).astype(BF16)
    q_epilogue(3)
    go_ref[0, :, :half] = proj(OFF_GO, half).astype(BF16)
    k_epilogue()
    go_ref[0, :, half:] = proj(OFF_GO + half, half).astype(BF16)
    v_epilogue()
    gate_epilogue()
    half_q = ATTN_WIDTH // 2
    aq_sc[cur, :, :half_q] = proj(OFF_AQ, half_q)
    aq_sc[cur, :, half_q:] = proj(OFF_AQ + half_q, half_q)
    akv_sc[cur] = proj(OFF_AK, 2 * KV_WIDTH)
    z_sc[cur] = proj(OFF_Z, Z_PAD)


def _proj_call(x, w_pad, gup, gbias, qg, kg, cos, shi, slo):
    B, S, D = x.shape
    tm = PROJ_TM
    n_s = S // tm
    n = n_s * B
    mm = lambda t: jnp.minimum(t, n - 1)
    ep = lambda t: jnp.maximum(t - 1, 0)
    tok = lambda t: (mm(t) % B, mm(t) // B, 0)
    tok_ep = lambda t: (ep(t) % B, ep(t) // B, 0)
    head = lambda t: (ep(t) % B, 0, ep(t) // B, 0)
    head_t = lambda t: (ep(t) % B, 0, 0, ep(t) // B)
    const = lambda t: (0, 0)
    tab = lambda t: (ep(t) // B, 0)
    in_specs = [
        pl.BlockSpec((1, tm, D), tok),
        pl.BlockSpec((D, PROJ_PAD_WIDTH), const),
        pl.BlockSpec((Z_PAD, 2 * GLA_QK_WIDTH), const),
        pl.BlockSpec((1, 2 * GLA_QK_WIDTH), const),
        pl.BlockSpec((1, LANES), const),
        pl.BlockSpec((1, LANES), const),
        pl.BlockSpec((tm, LANES), tab),
        pl.BlockSpec((tm, LANES), tab),
        pl.BlockSpec((tm, LANES), tab),
    ]
    out_shape = [
        jax.ShapeDtypeStruct((B, N_Q_HEADS, HEAD_DIM, S), BF16),
        jax.ShapeDtypeStruct((B, N_KV_HEADS, S, HEAD_DIM), BF16),
        jax.ShapeDtypeStruct((B, N_KV_HEADS, LANES, S), BF16),
        jax.ShapeDtypeStruct((B, S, GLA_QK_WIDTH), BF16),
        jax.ShapeDtypeStruct((B, S, GLA_QK_WIDTH), BF16),
        jax.ShapeDtypeStruct((B, S, GLA_WIDTH), BF16),
        jax.ShapeDtypeStruct((B, S, GLA_WIDTH), BF16),
        jax.ShapeDtypeStruct((B, S, GLA_QK_WIDTH), F32),
        jax.ShapeDtypeStruct((B, S, GLA_QK_WIDTH), F32),
    ]
    out_specs = [
        pl.BlockSpec((1, N_Q_HEADS, HEAD_DIM, tm), head_t),
        pl.BlockSpec((1, N_KV_HEADS, tm, HEAD_DIM), head),
        pl.BlockSpec((1, N_KV_HEADS, LANES, tm), head_t),
        pl.BlockSpec((1, tm, GLA_QK_WIDTH), tok),
        pl.BlockSpec((1, tm, GLA_QK_WIDTH), tok),
        pl.BlockSpec((1, tm, GLA_WIDTH), tok),
        pl.BlockSpec((1, tm, GLA_WIDTH), tok),
        pl.BlockSpec((1, tm, GLA_QK_WIDTH), tok_ep),
        pl.BlockSpec((1, tm, GLA_QK_WIDTH), tok_ep),
    ]
    scratch = [pltpu.VMEM((2, tm, ATTN_WIDTH), F32), pltpu.VMEM((2, tm, 2 * KV_WIDTH), F32),
               pltpu.VMEM((2, tm, Z_PAD), F32)]
    return pl.pallas_call(
        _proj_kernel, grid=(n + 1,), in_specs=in_specs, out_specs=out_specs, out_shape=out_shape,
        scratch_shapes=scratch, compiler_params=_cparams(("arbitrary",)),
    )(x, w_pad, gup, gbias, qg, kg, cos, shi, slo)


def _attn_kernel(bound_ref, qt_ref, k_ref, vt_ref, o_ref, acc_sc, den_sc):
    t = pl.program_id(0)
    cur, prev = t % 2, (t + 1) % 2
    tq = qt_ref.shape[3]
    n_keys = k_ref.shape[2]
    tk = ATTN_TK
    q_t = jnp.concatenate([qt_ref[0, r] for r in range(Q_PER_KV)], axis=1)
    unshifted_is_safe = bound_ref[0] < ATTN_UNSHIFTED_MAX_LOG2

    @pl.when(t == 0)
    def _():
        acc_sc[1] = jnp.zeros(acc_sc.shape[1:], F32)
        den_sc[1] = jnp.ones(den_sc.shape[1:], F32)

    def finish_previous():
        o_t = acc_sc[prev] / den_sc[prev]
        for r in range(Q_PER_KV):
            o_ref[0, :, r * HEAD_DIM:(r + 1) * HEAD_DIM] = o_t[:, r * tq:(r + 1) * tq].T.astype(o_ref.dtype)

    def finish(acc, den_row):
        acc_sc[cur] = acc[:HEAD_DIM]
        den_sc[cur] = den_row

    @pl.when(unshifted_is_safe)
    def _():
        n = n_keys // tk
        n_col = q_t.shape[1] // ATTN_COL
        col = lambda c: slice(c * ATTN_COL, (c + 1) * ATTN_COL)
        scores = lambda j, c: _dot(k_ref[0, 0, j * tk:(j + 1) * tk, :], q_t[:, col(c)])
        acc = [None] * n_col
        den = [None] * n_col
        s_cur = [scores(0, c) for c in range(n_col)]
        for j in range(n):
            s_nxt = []
            for c in range(n_col):
                p = jnp.exp2(s_cur[c])
                pv = _dot(vt_ref[0, 0, :HEAD_DIM, j * tk:(j + 1) * tk], p.astype(BF16))
                acc[c] = pv if acc[c] is None else acc[c] + pv
                part = jnp.sum(p.reshape(tk // SUBLANES, SUBLANES, ATTN_COL), axis=0)
                den[c] = part if den[c] is None else den[c] + part
                if j + 1 < n:
                    s_nxt.append(scores(j + 1, c))
            s_cur = s_nxt
            if j == 0:
                finish_previous()
        den_row = jnp.sum(jnp.concatenate(den, axis=1), axis=0, keepdims=True)
        finish(jnp.concatenate(acc, axis=1), den_row)

    @pl.when(jnp.logical_not(unshifted_is_safe))
    def _():
        finish_previous()
        m = None
        acc = None
        for j in range(n_keys // tk):
            s_t = _dot(k_ref[0, 0, j * tk:(j + 1) * tk, :], q_t)
            m_blk = jnp.max(s_t, axis=0, keepdims=True)
            m_new = m_blk if m is None else jnp.maximum(m, m_blk)
            pv = _dot(vt_ref[0, 0, :VT_ROWS, j * tk:(j + 1) * tk], jnp.exp2(s_t - m_new).astype(BF16))
            acc = pv if acc is None else jnp.exp2(m - m_new) * acc + pv
            m = m_new
        finish(acc, acc[HEAD_DIM:HEAD_DIM + 1])


def _attn_call(score_bound, qa_t, ka, va_t):
    B, _, S, _ = ka.shape
    tq = ATTN_TQ
    nq = S // tq
    n = B * N_KV_HEADS * nq
    acc_blk = lambda t: jnp.minimum(t, n - 1)
    out_blk = lambda t: jnp.maximum(t - 1, 0)
    b_of = lambda u: u // (N_KV_HEADS * nq)
    g_of = lambda u: (u // nq) % N_KV_HEADS
    return pl.pallas_call(
        _attn_kernel, grid=(n + 1,),
        in_specs=[
            pl.BlockSpec(memory_space=pltpu.SMEM),
            pl.BlockSpec((1, Q_PER_KV, HEAD_DIM, tq), lambda t: (b_of(acc_blk(t)), g_of(acc_blk(t)), 0, acc_blk(t) % nq)),
            pl.BlockSpec((1, 1, S, HEAD_DIM), lambda t: (b_of(acc_blk(t)), g_of(acc_blk(t)), 0, 0)),
            pl.BlockSpec((1, 1, LANES, S), lambda t: (b_of(acc_blk(t)), g_of(acc_blk(t)), 0, 0)),
        ],
        out_specs=pl.BlockSpec((1, tq, Q_PER_KV * HEAD_DIM),
                               lambda t: (b_of(out_blk(t)), out_blk(t) % nq, g_of(out_blk(t)))),
        out_shape=jax.ShapeDtypeStruct((B, S, ATTN_WIDTH), BF16),
        scratch_shapes=[pltpu.VMEM((2, HEAD_DIM, Q_PER_KV * tq), F32), pltpu.VMEM((2, 1, Q_PER_KV * tq), F32)],
        compiler_params=_cparams(("arbitrary",)),
    )(score_bound, qa_t, ka, va_t)


def _block_row(a, blk, row):
    n, w = a.shape
    a3 = a.reshape(n // blk, blk, w)
    return jnp.broadcast_to(a3[:, row:row + 1, :], a3.shape).reshape(n, w)


def _split3_bf16(a):
    hi = a.astype(BF16)
    r1 = a - hi.astype(F32)
    mid = r1.astype(BF16)
    lo = (r1 - mid.astype(F32)).astype(BF16)
    return hi, mid, lo


def _gla_direction(bi, q_ref, k_ref, v_ref, la_ref, o_ref, state_ref, forward):
    C = q_ref.shape[1]
    row = lax.broadcasted_iota(jnp.int32, (C, C), 0)
    col = lax.broadcasted_iota(jnp.int32, (C, C), 1)
    tri = ((col <= row) if forward else (col >= row)).astype(BF16)
    la = la_ref[bi]
    cum_all = sum(_dot(tri, part) for part in _split3_bf16(la))
    excl_all = cum_all - la
    xr = row ^ col
    valid = (col <= row) if forward else (col > row)
    lane = lax.broadcasted_iota(jnp.int32, (C, LANES), 1)
    lane_lo = lane < GLA_DK
    edge = C - 1 if forward else 0
    srow = lax.broadcasted_iota(jnp.int32, (LANES, 2 * GLA_DV), 0)
    scol = lax.broadcasted_iota(jnp.int32, (LANES, 2 * GLA_DV), 1)
    on_diag = (srow < GLA_DK) == (scol < GLA_DV)
    pairs = range(GLA_HEADS // 2)
    yield

    factors = []
    for pair in pairs:
        sl = slice(pair * LANES, (pair + 1) * LANES)
        q, k = q_ref[bi, :, sl].astype(F32), k_ref[bi, :, sl].astype(F32)
        cum, excl = cum_all[:, sl], excl_all[:, sl]
        total = cum[edge:edge + 1, :]

        base_row = 0 if forward else GLA_REF_CHUNK - 1
        loc = cum - _block_row(excl, GLA_REF_CHUNK, base_row)
        q_lv = [q * jnp.exp2(loc)]
        k_lv = [(k * jnp.exp2(-loc)).astype(BF16)]
        blk = 2 * GLA_REF_CHUNK
        while blk <= C:
            mid = _block_row(excl if forward else cum, blk, blk // 2)
            e = jnp.exp2(-jnp.abs(cum - mid))
            q_lv.append(q * e)
            k_lv.append((k * e).astype(BF16))
            blk *= 2
        q_in = (q * jnp.exp2(cum)).astype(BF16)
        k_out = k * jnp.exp2(total - cum)
        k_out_t = k_out.T.astype(BF16)
        decay_t = jnp.exp2(total).T
        factors.append((q_lv, k_lv, q_in, k_out_t, decay_t))
    yield

    def stack_heads(a):
        return jnp.concatenate([jnp.where(lane_lo, a, 0.0), jnp.where(lane_lo, 0.0, a)], axis=0).astype(BF16)

    xr2 = jnp.concatenate([xr, xr], axis=0)
    valid2 = jnp.concatenate([valid, valid], axis=0)
    scores = []
    for pair in pairs:
        q_lv, k_lv = factors[pair][:2]
        scores2 = _dot_nt(stack_heads(q_lv[-1]), k_lv[-1])
        bound = C // 2
        for lv in range(len(q_lv) - 2, -1, -1):
            scores2 = jnp.where(xr2 < bound, _dot_nt(stack_heads(q_lv[lv]), k_lv[lv]), scores2)
            bound //= 2
        scores.append(jnp.where(valid2, scores2, 0.0).astype(BF16))
    yield

    for pair in pairs:
        q_in, k_out_t, decay_t = factors[pair][2:]
        v2 = v_ref[bi, :, 2 * pair * GLA_DV:2 * (pair + 1) * GLA_DV]
        state = state_ref[bi, pair]
        inter = _dot(q_in, state.astype(BF16))
        for hh in range(2):
            h = 2 * pair + hh
            cols = slice(hh * GLA_DV, (hh + 1) * GLA_DV)
            o = _dot(scores[pair][hh * C:(hh + 1) * C], v2[:, cols]) + inter[:, cols]
            o_ref[bi, :, h * GLA_DV:(h + 1) * GLA_DV] = o.astype(o_ref.dtype)
        state_ref[bi, pair] = jnp.where(on_diag, decay_t * state + _dot(k_out_t, v2), 0.0)
    yield


def _gla_kernel(qf_ref, kf_ref, vf_ref, laf_ref, qb_ref, kb_ref, vb_ref, lab_ref,
                of_ref, ob_ref, sf_ref, sb_ref):
    @pl.when(pl.program_id(1) == 0)
    def _():
        sf_ref[...] = jnp.zeros(sf_ref.shape, F32)
        sb_ref[...] = jnp.zeros(sb_ref.shape, F32)

    chains = []
    for bi in range(GLA_BATCH):
        chains.append(_gla_direction(bi, qf_ref, kf_ref, vf_ref, laf_ref, of_ref, sf_ref, True))
        chains.append(_gla_direction(bi, qb_ref, kb_ref, vb_ref, lab_ref, ob_ref, sb_ref, False))
    for _ in range(GLA_PHASES):
        for chain in chains:
            next(chain)


def _gla_call(gq, gk, gv, laf, lab):
    B, S, _ = gq.shape
    C, nb = GLA_CHUNK, GLA_BATCH
    n = S // C
    fwd = lambda b, c: (b, c, 0)
    bwd = lambda b, c: (b, n - 1 - c, 0)
    qk = lambda im: pl.BlockSpec((nb, C, GLA_QK_WIDTH), im)
    vv = lambda im: pl.BlockSpec((nb, C, GLA_WIDTH), im)
    state = pltpu.VMEM((nb, GLA_HEADS // 2, LANES, 2 * GLA_DV), F32)
    return pl.pallas_call(
        _gla_kernel, grid=(B // nb, n),
        in_specs=[qk(fwd), qk(fwd), vv(fwd), qk(fwd), qk(bwd), qk(bwd), vv(bwd), qk(bwd)],
        out_specs=[vv(fwd), vv(bwd)],
        out_shape=[jax.ShapeDtypeStruct((B, S, GLA_WIDTH), BF16)] * 2,
        scratch_shapes=[state, state],
        compiler_params=_cparams(("arbitrary", "arbitrary")),
    )(gq, gk, gv, laf, gq, gk, gv, lab)


def _layer_norm(y, g, b):
    mu = jnp.mean(y, axis=-1, keepdims=True)
    d = y - mu
    var = jnp.mean(d * d, axis=-1, keepdims=True)
    return d * lax.rsqrt(var + LN_EPS) * g + b


def _silu(g):
    return g * (1.0 / (1.0 + jnp.exp(-g)))


def _mix_ffn_kernel(attn_ref, of_ref, ob_ref, go_ref, x_ref, wo_ref, gng_ref, ln1g_ref, ln1b_ref,
                    wg_ref, wu_ref, wd_ref, ln2g_ref, ln2b_ref, out_ref, x1_sc, x1b_sc):
    t = pl.program_id(0)
    cur, prev = t % 2, (t + 1) % 2

    @pl.when(t == 0)
    def _():
        x1_sc[1] = jnp.zeros(x1_sc.shape[1:], F32)
        x1b_sc[1] = jnp.zeros(x1b_sc.shape[1:], BF16)

    hidden = []

    def ffn_chunk(c):
        sl = slice(c * FFN_CHUNK, (c + 1) * FFN_CHUNK)
        xb = x1b_sc[prev]
        hidden.append((_silu(_dot(xb, wg_ref[:, sl])) * _dot(xb, wu_ref[:, sl])).astype(BF16))

    def gate_head(h):
        sl = slice(h * GLA_DV, (h + 1) * GLA_DV)
        o = of_ref[:, sl].astype(F32) + ob_ref[:, sl].astype(F32)
        inv = lax.rsqrt(jnp.mean(o * o, axis=-1, keepdims=True) + RMS_EPS)
        return ((o * inv * gng_ref[...]) * _silu(go_ref[:, sl].astype(F32))).astype(BF16)

    n_chunks = D_FF // FFN_CHUNK
    gated = []
    for c in range(GLA_HEADS):
        ffn_chunk(c)
        gated.append(gate_head(c))
    ffn_chunk(GLA_HEADS)
    merged = jnp.concatenate([attn_ref[...]] + gated, axis=-1)
    mixed = _dot(merged, wo_ref[...])
    ffn_chunk(GLA_HEADS + 1)
    ffn_chunk(GLA_HEADS + 2)
    x1 = _layer_norm(DEEPNORM_ALPHA * x_ref[...] + mixed, ln1g_ref[...], ln1b_ref[...])
    x1_sc[cur] = x1
    x1b_sc[cur] = x1.astype(BF16)
    for c in range(GLA_HEADS + 3, n_chunks):
        ffn_chunk(c)
    ffn = _dot(jnp.concatenate(hidden, axis=-1), wd_ref[...])
    out_ref[...] = _layer_norm(DEEPNORM_ALPHA * x1_sc[prev] + ffn, ln2g_ref[...], ln2b_ref[...])


def _mix_ffn_call(attn, o_f, o_b, go, x, w_out, gng, ln1_g, ln1_b, w_gate, w_up, w_down, ln2_g, ln2_b):
    B, S, D = x.shape
    tm = MIX_FFN_TM
    tokens = B * S
    n = tokens // tm
    flat = lambda a: a.reshape(tokens, a.shape[-1])
    merge_tile = lambda t: (jnp.minimum(t, n - 1), 0)
    ffn_tile = lambda t: (jnp.maximum(t - 1, 0), 0)
    const = lambda t: (0, 0)
    resident = lambda shape: pl.BlockSpec(shape, const, pipeline_mode=pl.Buffered(1))
    out = pl.pallas_call(
        _mix_ffn_kernel, grid=(n + 1,),
        in_specs=[
            pl.BlockSpec((tm, ATTN_WIDTH), merge_tile),
            pl.BlockSpec((tm, GLA_WIDTH), merge_tile),
            pl.BlockSpec((tm, GLA_WIDTH), merge_tile),
            pl.BlockSpec((tm, GLA_WIDTH), merge_tile),
            pl.BlockSpec((tm, D), merge_tile),
            resident((ATTN_WIDTH + GLA_WIDTH, D)),
            pl.BlockSpec((1, GLA_DV), const),
            pl.BlockSpec((1, D), const),
            pl.BlockSpec((1, D), const),
            resident((D, D_FF)), resident((D, D_FF)), resident((D_FF, D)),
            pl.BlockSpec((1, D), const),
            pl.BlockSpec((1, D), const),
        ],
        out_specs=pl.BlockSpec((tm, D), ffn_tile),
        out_shape=jax.ShapeDtypeStruct((tokens, D), F32),
        scratch_shapes=[pltpu.VMEM((2, tm, D), F32), pltpu.VMEM((2, tm, D), BF16)],
        compiler_params=_cparams(("arbitrary",)),
    )(flat(attn), flat(o_f), flat(o_b), flat(go), flat(x), w_out, gng, ln1_g, ln1_b,
      w_gate, w_up, w_down, ln2_g, ln2_b)
    return out.reshape(B, S, D)


def _rope_tables(seq_len):
    t = jnp.arange(seq_len, dtype=jnp.int32)
    row_id = (t // GRID_W).astype(F32)
    col_id = (t % GRID_W).astype(F32)
    inv_freq = ROPE_THETA ** (-jnp.arange(0, AXIAL_DIM, 2, dtype=F32) / AXIAL_DIM)
    lane = jnp.arange(LANES, dtype=jnp.int32)
    d = lane % HEAD_DIM
    is_col = (d // AXIAL_DIM) == 1
    upper = ((d % AXIAL_DIM) // ROPE_HALF) == 1
    ang = jnp.where(is_col, col_id[:, None], row_id[:, None]) * jnp.tile(inv_freq, LANES // ROPE_HALF)
    sin = jnp.sin(ang)
    return jnp.cos(ang), jnp.where(upper, sin, 0.0), jnp.where(upper, 0.0, -sin)


def kernel(x, w_in, q_norm_g, k_norm_g, gate_up_fwd, gate_bias_fwd, gate_up_bwd, gate_bias_bwd, gla_norm_g,
           w_out, ln1_g, ln1_b, w_ffn_gate, w_ffn_up, w_ffn_down, ln2_g, ln2_b):
    B, S, D = x.shape
    assert D == D_MODEL and S % max(PROJ_TM, ATTN_TQ, ATTN_TK, GLA_CHUNK, MIX_FFN_TM) == 0 and B % GLA_BATCH == 0
    assert w_in.shape[0] == DEPTH
    cos, sin_hi, sin_lo = _rope_tables(S)
    for layer in range(DEPTH):
        w_pad = jnp.pad(w_in[layer], ((0, 0), (0, PROJ_PAD_WIDTH - w_in.shape[-1]))).astype(BF16)
        gup = jnp.zeros((Z_PAD, 2 * GLA_QK_WIDTH), F32)
        gup = gup.at[:GATE_RANK, :GLA_QK_WIDTH].set(gate_up_fwd[layer])
        gup = gup.at[GATE_RANK:2 * GATE_RANK, GLA_QK_WIDTH:].set(gate_up_bwd[layer]).astype(BF16)
        gbias = jnp.concatenate([gate_bias_fwd[layer], gate_bias_bwd[layer]])[None, :]
        qg = jnp.tile(q_norm_g[layer], LANES // HEAD_DIM)[None, :]
        kg = jnp.tile(k_norm_g[layer], LANES // HEAD_DIM)[None, :]

        qa, ka, va, gq, gk, gv, go, laf, lab = _proj_call(x, w_pad, gup, gbias, qg, kg, cos, sin_hi, sin_lo)
        score_bound = (jnp.max(jnp.abs(q_norm_g[layer])) * jnp.max(jnp.abs(k_norm_g[layer]))
                       * SCORE_BOUND_SCALE).reshape(1).astype(F32)
        attn = _attn_call(score_bound, qa, ka, va)
        o_f, o_b = _gla_call(gq, gk, gv, laf, lab)
        x = _mix_ffn_call(attn, o_f, o_b, go, x, w_out[layer].astype(BF16), gla_norm_g[layer][None, :],
                          ln1_g[layer][None, :], ln1_b[layer][None, :],
                          w_ffn_gate[layer].astype(BF16), w_ffn_up[layer].astype(BF16),
                          w_ffn_down[layer].astype(BF16), ln2_g[layer][None, :], ln2_b[layer][None, :])
    return x
```

```python
import jax
import jax.numpy as jnp
from jax import lax
from jax.experimental import pallas as pl
from jax.experimental.pallas import tpu as pltpu

F32 = jnp.float32
BF16 = jnp.bfloat16

D_MODEL = 1024
GRID_W = 64
N_Q_HEADS = 8
N_KV_HEADS = 2
Q_PER_KV = N_Q_HEADS // N_KV_HEADS
HEAD_DIM = 64
AXIAL_DIM = HEAD_DIM // 2
ROPE_HALF = AXIAL_DIM // 2
ROPE_THETA = 10000.0
GLA_HEADS = 4
GLA_DK = 64
GLA_DV = 128
GATE_RANK = 16
GATE_TAU = 16.0
GLA_REF_CHUNK = 16
ATTN_WIDTH = N_Q_HEADS * HEAD_DIM
KV_WIDTH = N_KV_HEADS * HEAD_DIM
GLA_QK_WIDTH = GLA_HEADS * GLA_DK
GLA_WIDTH = GLA_HEADS * GLA_DV
D_FF = 2816
DEPTH = 1
DEEPNORM_ALPHA = (2 * DEPTH) ** 0.25
LN_EPS = 1e-5
RMS_EPS = 1e-6
LOG2_E = 1.4426950408889634

LANES = 128
SUBLANES = 8
VMEM_LIMIT_BYTES = 56 * 1024 * 1024

PROJ_TM = 512
ATTN_TQ = 512
ATTN_TK = 256
ATTN_COL = 512
VT_ROWS = HEAD_DIM + 16
SCORE_BOUND_SCALE = LOG2_E * HEAD_DIM ** 0.5 * 1.01
ATTN_UNSHIFTED_MAX_LOG2 = 60.0
GLA_CHUNK = 128
GLA_BATCH = 8
GLA_PHASES = 4
MIX_FFN_TM = 512
FFN_CHUNK = 256

OFF_AQ = 0
OFF_AK = OFF_AQ + ATTN_WIDTH
OFF_AV = OFF_AK + KV_WIDTH
OFF_GQ = OFF_AV + KV_WIDTH
OFF_GK = OFF_GQ + GLA_QK_WIDTH
OFF_GV = OFF_GK + GLA_QK_WIDTH
OFF_GO = OFF_GV + GLA_WIDTH
OFF_Z = OFF_GO + GLA_WIDTH
Z_PAD = LANES
PROJ_PAD_WIDTH = OFF_Z + Z_PAD


def _cparams(semantics):
    return pltpu.CompilerParams(dimension_semantics=semantics, vmem_limit_bytes=VMEM_LIMIT_BYTES)


def _dot(a, b):
    return jnp.dot(a, b, preferred_element_type=F32)


def _dot_nt(a, b):
    return lax.dot_general(a, b, (((1,), (1,)), ((), ())), preferred_element_type=F32)


def _rope128(y, cos, sin_hi, sin_lo):
    return y * cos + pltpu.roll(y, ROPE_HALF, 1) * sin_hi + pltpu.roll(y, LANES - ROPE_HALF, 1) * sin_lo


def _head_pair_inv_rms(blk, lane_lo):
    sq = blk * blk
    ss_lo = jnp.sum(jnp.where(lane_lo, sq, 0.0), axis=-1, keepdims=True)
    ss_hi = jnp.sum(jnp.where(lane_lo, 0.0, sq), axis=-1, keepdims=True)
    inv = 1.0 / HEAD_DIM
    return jnp.where(lane_lo, lax.rsqrt(ss_lo * inv + RMS_EPS), lax.rsqrt(ss_hi * inv + RMS_EPS))


def _proj_kernel(x_ref, w_ref, gup_ref, gbias_ref, qg_ref, kg_ref, cos_ref, shi_ref, slo_ref,
                 qa_ref, ka_ref, va_ref, gq_ref, gk_ref, gv_ref, go_ref, laf_ref, lab_ref,
                 aq_sc, akv_sc, z_sc):
    t = pl.program_id(0)
    cur, prev = t % 2, (t + 1) % 2

    @pl.when(t == 0)
    def _():
        aq_sc[1] = jnp.zeros(aq_sc.shape[1:], F32)
        akv_sc[1] = jnp.zeros(akv_sc.shape[1:], F32)
        z_sc[1] = jnp.zeros(z_sc.shape[1:], F32)

    x = x_ref[0].astype(BF16)
    tm = x.shape[0]
    lane = lax.broadcasted_iota(jnp.int32, (tm, LANES), 1)
    lane_lo = lane < HEAD_DIM
    cos, shi, slo = cos_ref[...], shi_ref[...], slo_ref[...]

    def proj(off, width):
        return _dot(x, w_ref[:, off:off + width])

    def q_epilogue(c):
        blk = aq_sc[prev, :, c * LANES:(c + 1) * LANES]
        rot = _rope128(blk * qg_ref[...], cos, shi, slo)
        out_t = (rot * (_head_pair_inv_rms(blk, lane_lo) * (LOG2_E * HEAD_DIM ** -0.5))).T.astype(BF16)
        qa_ref[0, 2 * c] = out_t[:HEAD_DIM]
        qa_ref[0, 2 * c + 1] = out_t[HEAD_DIM:]

    def k_epilogue():
        ak = akv_sc[prev, :, :KV_WIDTH]
        rot = _rope128(ak * kg_ref[...], cos, shi, slo)
        out_t = (rot * _head_pair_inv_rms(ak, lane_lo)).T.astype(BF16)
        ka_ref[0, 0] = out_t[:HEAD_DIM]
        ka_ref[0, 1] = out_t[HEAD_DIM:]

    def v_epilogue():
        av = akv_sc[prev, :, KV_WIDTH:]
        ones_col = jnp.where(lane == HEAD_DIM, 1.0, 0.0)
        va_ref[0, 0] = jnp.where(lane_lo, av, ones_col).T.astype(BF16)
        va_ref[0, 1] = jnp.where(lane_lo, pltpu.roll(av, HEAD_DIM, 1), ones_col).T.astype(BF16)

    def gate_epilogue():
        g = _dot(z_sc[prev].astype(BF16), gup_ref[...]) + gbias_ref[...]
        log2_a = (jnp.minimum(g, 0.0) - jnp.log(1.0 + jnp.exp(-jnp.abs(g)))) * (LOG2_E / GATE_TAU)
        laf_ref[0] = log2_a[:, :GLA_QK_WIDTH]
        lab_ref[0] = log2_a[:, GLA_QK_WIDTH:]

    half = GLA_WIDTH // 2
    gq_ref[0] = (proj(OFF_GQ, GLA_QK_WIDTH) * (GLA_DK ** -0.5)).astype(BF16)
    q_epilogue(0)
    gk_ref[0] = proj(OFF_GK, GLA_QK_WIDTH).astype(BF16)
    q_epilogue(1)
    gv_ref[0, :, :half] = proj(OFF_GV, half).astype(BF16)
    q_epilogue(2)
    gv_ref[0, :, half:] = proj(OFF_GV + half, half).astype(BF16)
    q_epilogue(3)
    go_ref[0, :, :half] = proj(OFF_GO, half).astype(BF16)
    k_epilogue()
    go_ref[0, :, half:] = proj(OFF_GO + half, half).astype(BF16)
    v_epilogue()
    gate_epilogue()
    half_q = ATTN_WIDTH // 2
    aq_sc[cur, :, :half_q] = proj(OFF_AQ, half_q)
    aq_sc[cur, :, half_q:] = proj(OFF_AQ + half_q, half_q)
    akv_sc[cur] = proj(OFF_AK, 2 * KV_WIDTH)
    z_sc[cur] = proj(OFF_Z, Z_PAD)


def _proj_call(x, w_pad, gup, gbias, qg, kg, cos, shi, slo):
    B, S, D = x.shape
    tm = PROJ_TM
    n_s = S // tm
    n = n_s * B
    mm = lambda t: jnp.minimum(t, n - 1)
    ep = lambda t: jnp.maximum(t - 1, 0)
    tok = lambda t: (mm(t) % B, mm(t) // B, 0)
    tok_ep = lambda t: (ep(t) % B, ep(t) // B, 0)
    head_t = lambda t: (ep(t) % B, 0, 0, ep(t) // B)
    const = lambda t: (0, 0)
    tab = lambda t: (ep(t) // B, 0)
    in_specs = [
        pl.BlockSpec((1, tm, D), tok),
        pl.BlockSpec((D, PROJ_PAD_WIDTH), const),
        pl.BlockSpec((Z_PAD, 2 * GLA_QK_WIDTH), const),
        pl.BlockSpec((1, 2 * GLA_QK_WIDTH), const),
        pl.BlockSpec((1, LANES), const),
        pl.BlockSpec((1, LANES), const),
        pl.BlockSpec((tm, LANES), tab),
        pl.BlockSpec((tm, LANES), tab),
        pl.BlockSpec((tm, LANES), tab),
    ]
    out_shape = [
        jax.ShapeDtypeStruct((B, N_Q_HEADS, HEAD_DIM, S), BF16),
        jax.ShapeDtypeStruct((B, N_KV_HEADS, HEAD_DIM, S), BF16),
        jax.ShapeDtypeStruct((B, N_KV_HEADS, LANES, S), BF16),
        jax.ShapeDtypeStruct((B, S, GLA_QK_WIDTH), BF16),
        jax.ShapeDtypeStruct((B, S, GLA_QK_WIDTH), BF16),
        jax.ShapeDtypeStruct((B, S, GLA_WIDTH), BF16),
        jax.ShapeDtypeStruct((B, S, GLA_WIDTH), BF16),
        jax.ShapeDtypeStruct((B, S, GLA_QK_WIDTH), F32),
        jax.ShapeDtypeStruct((B, S, GLA_QK_WIDTH), F32),
    ]
    out_specs = [
        pl.BlockSpec((1, N_Q_HEADS, HEAD_DIM, tm), head_t),
        pl.BlockSpec((1, N_KV_HEADS, HEAD_DIM, tm), head_t),
        pl.BlockSpec((1, N_KV_HEADS, LANES, tm), head_t),
        pl.BlockSpec((1, tm, GLA_QK_WIDTH), tok),
        pl.BlockSpec((1, tm, GLA_QK_WIDTH), tok),
        pl.BlockSpec((1, tm, GLA_WIDTH), tok),
        pl.BlockSpec((1, tm, GLA_WIDTH), tok),
        pl.BlockSpec((1, tm, GLA_QK_WIDTH), tok_ep),
        pl.BlockSpec((1, tm, GLA_QK_WIDTH), tok_ep),
    ]
    scratch = [pltpu.VMEM((2, tm, ATTN_WIDTH), F32), pltpu.VMEM((2, tm, 2 * KV_WIDTH), F32),
               pltpu.VMEM((2, tm, Z_PAD), F32)]
    return pl.pallas_call(
        _proj_kernel, grid=(n + 1,), in_specs=in_specs, out_specs=out_specs, out_shape=out_shape,
        scratch_shapes=scratch, compiler_params=_cparams(("arbitrary",)),
    )(x, w_pad, gup, gbias, qg, kg, cos, shi, slo)


def _attn_kernel(bound_ref, qt_ref, kt_ref, vt_ref, o_ref, acc_sc, den_sc):
    t = pl.program_id(0)
    cur, prev = t % 2, (t + 1) % 2
    tq = qt_ref.shape[3]
    n_keys = kt_ref.shape[3]
    tk = ATTN_TK
    key_chunk = lambda j: kt_ref[0, 0, :, j * tk:(j + 1) * tk].T
    q_t = jnp.concatenate([qt_ref[0, r] for r in range(Q_PER_KV)], axis=1)
    unshifted_is_safe = bound_ref[0] < ATTN_UNSHIFTED_MAX_LOG2

    @pl.when(t == 0)
    def _():
        acc_sc[1] = jnp.zeros(acc_sc.shape[1:], F32)
        den_sc[1] = jnp.ones(den_sc.shape[1:], F32)

    def finish_previous():
        o_t = acc_sc[prev] / den_sc[prev]
        for r in range(Q_PER_KV):
            o_ref[0, :, r * HEAD_DIM:(r + 1) * HEAD_DIM] = o_t[:, r * tq:(r + 1) * tq].T.astype(o_ref.dtype)

    def finish(acc, den_row):
        acc_sc[cur] = acc[:HEAD_DIM]
        den_sc[cur] = den_row

    @pl.when(unshifted_is_safe)
    def _():
        n = n_keys // tk
        n_col = q_t.shape[1] // ATTN_COL
        col = lambda c: slice(c * ATTN_COL, (c + 1) * ATTN_COL)
        keys = [key_chunk(j) for j in range(n)]
        scores = lambda j, c: _dot(keys[j], q_t[:, col(c)])
        acc = [None] * n_col
        den = [None] * n_col
        s_cur = [scores(0, c) for c in range(n_col)]
        for j in range(n):
            s_nxt = []
            for c in range(n_col):
                p = jnp.exp2(s_cur[c])
                pv = _dot(vt_ref[0, 0, :HEAD_DIM, j * tk:(j + 1) * tk], p.astype(BF16))
                acc[c] = pv if acc[c] is None else acc[c] + pv
                part = jnp.sum(p.reshape(tk // SUBLANES, SUBLANES, ATTN_COL), axis=0)
                den[c] = part if den[c] is None else den[c] + part
                if j + 1 < n:
                    s_nxt.append(scores(j + 1, c))
            s_cur = s_nxt
            if j == 0:
                finish_previous()
        den_row = jnp.sum(jnp.concatenate(den, axis=1), axis=0, keepdims=True)
        finish(jnp.concatenate(acc, axis=1), den_row)

    @pl.when(jnp.logical_not(unshifted_is_safe))
    def _():
        finish_previous()
        m = None
        acc = None
        for j in range(n_keys // tk):
            s_t = _dot(key_chunk(j), q_t)
            m_blk = jnp.max(s_t, axis=0, keepdims=True)
            m_new = m_blk if m is None else jnp.maximum(m, m_blk)
            pv = _dot(vt_ref[0, 0, :VT_ROWS, j * tk:(j + 1) * tk], jnp.exp2(s_t - m_new).astype(BF16))
            acc = pv if acc is None else jnp.exp2(m - m_new) * acc + pv
            m = m_new
        finish(acc, acc[HEAD_DIM:HEAD_DIM + 1])


def _attn_call(score_bound, qa_t, ka_t, va_t):
    B, _, _, S = ka_t.shape
    tq = ATTN_TQ
    nq = S // tq
    n = B * N_KV_HEADS * nq
    acc_blk = lambda t: jnp.minimum(t, n - 1)
    out_blk = lambda t: jnp.maximum(t - 1, 0)
    b_of = lambda u: u // (N_KV_HEADS * nq)
    g_of = lambda u: (u // nq) % N_KV_HEADS
    return pl.pallas_call(
        _attn_kernel, grid=(n + 1,),
        in_specs=[
            pl.BlockSpec(memory_space=pltpu.SMEM),
            pl.BlockSpec((1, Q_PER_KV, HEAD_DIM, tq), lambda t: (b_of(acc_blk(t)), g_of(acc_blk(t)), 0, acc_blk(t) % nq)),
            pl.BlockSpec((1, 1, HEAD_DIM, S), lambda t: (b_of(acc_blk(t)), g_of(acc_blk(t)), 0, 0)),
            pl.BlockSpec((1, 1, LANES, S), lambda t: (b_of(acc_blk(t)), g_of(acc_blk(t)), 0, 0)),
        ],
        out_specs=pl.BlockSpec((1, tq, Q_PER_KV * HEAD_DIM),
                               lambda t: (b_of(out_blk(t)), out_blk(t) % nq, g_of(out_blk(t)))),
        out_shape=jax.ShapeDtypeStruct((B, S, ATTN_WIDTH), BF16),
        scratch_shapes=[pltpu.VMEM((2, HEAD_DIM, Q_PER_KV * tq), F32), pltpu.VMEM((2, 1, Q_PER_KV * tq), F32)],
        compiler_params=_cparams(("arbitrary",)),
    )(score_bound, qa_t, ka_t, va_t)


def _block_row(a, blk, row):
    n, w = a.shape
    a3 = a.reshape(n // blk, blk, w)
    return jnp.broadcast_to(a3[:, row:row + 1, :], a3.shape).reshape(n, w)


def _split3_bf16(a):
    hi = a.astype(BF16)
    r1 = a - hi.astype(F32)
    mid = r1.astype(BF16)
    lo = (r1 - mid.astype(F32)).astype(BF16)
    return hi, mid, lo


def _gla_direction(bi, q_ref, k_ref, v_ref, la_ref, o_ref, state_ref, forward):
    C = q_ref.shape[1]
    row = lax.broadcasted_iota(jnp.int32, (C, C), 0)
    col = lax.broadcasted_iota(jnp.int32, (C, C), 1)
    tri = ((col <= row) if forward else (col >= row)).astype(BF16)
    la = la_ref[bi]
    cum_all = sum(_dot(tri, part) for part in _split3_bf16(la))
    excl_all = cum_all - la
    xr = row ^ col
    valid = (col <= row) if forward else (col > row)
    lane = lax.broadcasted_iota(jnp.int32, (C, LANES), 1)
    lane_lo = lane < GLA_DK
    edge = C - 1 if forward else 0
    srow = lax.broadcasted_iota(jnp.int32, (LANES, 2 * GLA_DV), 0)
    scol = lax.broadcasted_iota(jnp.int32, (LANES, 2 * GLA_DV), 1)
    on_diag = (srow < GLA_DK) == (scol < GLA_DV)
    pairs = range(GLA_HEADS // 2)
    yield

    factors = []
    for pair in pairs:
        sl = slice(pair * LANES, (pair + 1) * LANES)
        q, k = q_ref[bi, :, sl].astype(F32), k_ref[bi, :, sl].astype(F32)
        cum, excl = cum_all[:, sl], excl_all[:, sl]
        total = cum[edge:edge + 1, :]

        base_row = 0 if forward else GLA_REF_CHUNK - 1
        loc = cum - _block_row(excl, GLA_REF_CHUNK, base_row)
        q_lv = [q * jnp.exp2(loc)]
        k_lv = [(k * jnp.exp2(-loc)).astype(BF16)]
        blk = 2 * GLA_REF_CHUNK
        while blk <= C:
            mid = _block_row(excl if forward else cum, blk, blk // 2)
            e = jnp.exp2(-jnp.abs(cum - mid))
            q_lv.append(q * e)
            k_lv.append((k * e).astype(BF16))
            blk *= 2
        q_in = (q * jnp.exp2(cum)).astype(BF16)
        k_out = k * jnp.exp2(total - cum)
        k_out_t = k_out.T.astype(BF16)
        decay_t = jnp.exp2(total).T
        factors.append((q_lv, k_lv, q_in, k_out_t, decay_t))
    yield

    def stack_heads(a):
        return jnp.concatenate([jnp.where(lane_lo, a, 0.0), jnp.where(lane_lo, 0.0, a)], axis=0).astype(BF16)

    xr2 = jnp.concatenate([xr, xr], axis=0)
    valid2 = jnp.concatenate([valid, valid], axis=0)
    scores = []
    for pair in pairs:
        q_lv, k_lv = factors[pair][:2]
        scores2 = _dot_nt(stack_heads(q_lv[-1]), k_lv[-1])
        bound = C // 2
        for lv in range(len(q_lv) - 2, -1, -1):
            scores2 = jnp.where(xr2 < bound, _dot_nt(stack_heads(q_lv[lv]), k_lv[lv]), scores2)
            bound //= 2
        scores.append(jnp.where(valid2, scores2, 0.0).astype(BF16))
    yield

    for pair in pairs:
        q_in, k_out_t, decay_t = factors[pair][2:]
        v2 = v_ref[bi, :, 2 * pair * GLA_DV:2 * (pair + 1) * GLA_DV]
        state = state_ref[bi, pair]
        inter = _dot(q_in, state.astype(BF16))
        for hh in range(2):
            h = 2 * pair + hh
            cols = slice(hh * GLA_DV, (hh + 1) * GLA_DV)
            o = _dot(scores[pair][hh * C:(hh + 1) * C], v2[:, cols]) + inter[:, cols]
            o_ref[bi, :, h * GLA_DV:(h + 1) * GLA_DV] = o.astype(o_ref.dtype)
        state_ref[bi, pair] = jnp.where(on_diag, decay_t * state + _dot(k_out_t, v2), 0.0)
    yield


def _gla_kernel(qf_ref, kf_ref, vf_ref, laf_ref, qb_ref, kb_ref, vb_ref, lab_ref,
                of_ref, ob_ref, sf_ref, sb_ref):
    @pl.when(pl.program_id(1) == 0)
    def _():
        sf_ref[...] = jnp.zeros(sf_ref.shape, F32)
        sb_ref[...] = jnp.zeros(sb_ref.shape, F32)

    chains = []
    for bi in range(GLA_BATCH):
        chains.append(_gla_direction(bi, qf_ref, kf_ref, vf_ref, laf_ref, of_ref, sf_ref, True))
        chains.append(_gla_direction(bi, qb_ref, kb_ref, vb_ref, lab_ref, ob_ref, sb_ref, False))
    for _ in range(GLA_PHASES):
        for chain in chains:
            next(chain)


def _gla_call(gq, gk, gv, laf, lab):
    B, S, _ = gq.shape
    C, nb = GLA_CHUNK, GLA_BATCH
    n = S // C
    fwd = lambda b, c: (b, c, 0)
    bwd = lambda b, c: (b, n - 1 - c, 0)
    qk = lambda im: pl.BlockSpec((nb, C, GLA_QK_WIDTH), im)
    vv = lambda im: pl.BlockSpec((nb, C, GLA_WIDTH), im)
    state = pltpu.VMEM((nb, GLA_HEADS // 2, LANES, 2 * GLA_DV), F32)
    return pl.pallas_call(
        _gla_kernel, grid=(B // nb, n),
        in_specs=[qk(fwd), qk(fwd), vv(fwd), qk(fwd), qk(bwd), qk(bwd), vv(bwd), qk(bwd)],
        out_specs=[vv(fwd), vv(bwd)],
        out_shape=[jax.ShapeDtypeStruct((B, S, GLA_WIDTH), BF16)] * 2,
        scratch_shapes=[state, state],
        compiler_params=_cparams(("arbitrary", "arbitrary")),
    )(gq, gk, gv, laf, gq, gk, gv, lab)


def _layer_norm(y, g, b):
    mu = jnp.mean(y, axis=-1, keepdims=True)
    d = y - mu
    var = jnp.mean(d * d, axis=-1, keepdims=True)
    return d * lax.rsqrt(var + LN_EPS) * g + b


def _silu(g):
    return g * (1.0 / (1.0 + jnp.exp(-g)))


def _mix_ffn_kernel(attn_ref, of_ref, ob_ref, go_ref, x_ref, wo_ref, gng_ref, ln1g_ref, ln1b_ref,
                    wg_ref, wu_ref, wd_ref, ln2g_ref, ln2b_ref, out_ref, x1_sc, x1b_sc):
    t = pl.program_id(0)
    cur, prev = t % 2, (t + 1) % 2

    @pl.when(t == 0)
    def _():
        x1_sc[1] = jnp.zeros(x1_sc.shape[1:], F32)
        x1b_sc[1] = jnp.zeros(x1b_sc.shape[1:], BF16)

    hidden = []

    def ffn_chunk(c):
        sl = slice(c * FFN_CHUNK, (c + 1) * FFN_CHUNK)
        xb = x1b_sc[prev]
        hidden.append((_silu(_dot(xb, wg_ref[:, sl])) * _dot(xb, wu_ref[:, sl])).astype(BF16))

    def gate_head(h):
        sl = slice(h * GLA_DV, (h + 1) * GLA_DV)
        o = of_ref[:, sl].astype(F32) + ob_ref[:, sl].astype(F32)
        inv = lax.rsqrt(jnp.mean(o * o, axis=-1, keepdims=True) + RMS_EPS)
        return ((o * inv * gng_ref[...]) * _silu(go_ref[:, sl].astype(F32))).astype(BF16)

    n_chunks = D_FF // FFN_CHUNK
    gated = []
    for c in range(GLA_HEADS):
        ffn_chunk(c)
        gated.append(gate_head(c))
    ffn_chunk(GLA_HEADS)
    merged = jnp.concatenate([attn_ref[...]] + gated, axis=-1)
    mixed = _dot(merged, wo_ref[...])
    ffn_chunk(GLA_HEADS + 1)
    ffn_chunk(GLA_HEADS + 2)
    x1 = _layer_norm(DEEPNORM_ALPHA * x_ref[...] + mixed, ln1g_ref[...], ln1b_ref[...])
    x1_sc[cur] = x1
    x1b_sc[cur] = x1.astype(BF16)
    for c in range(GLA_HEADS + 3, n_chunks):
        ffn_chunk(c)
    ffn = _dot(jnp.concatenate(hidden, axis=-1), wd_ref[...])
    out_ref[...] = _layer_norm(DEEPNORM_ALPHA * x1_sc[prev] + ffn, ln2g_ref[...], ln2b_ref[...])


def _mix_ffn_call(attn, o_f, o_b, go, x, w_out, gng, ln1_g, ln1_b, w_gate, w_up, w_down, ln2_g, ln2_b):
    B, S, D = x.shape
    tm = MIX_FFN_TM
    tokens = B * S
    n = tokens // tm
    flat = lambda a: a.reshape(tokens, a.shape[-1])
    merge_tile = lambda t: (jnp.minimum(t, n - 1), 0)
    ffn_tile = lambda t: (jnp.maximum(t - 1, 0), 0)
    const = lambda t: (0, 0)
    resident = lambda shape: pl.BlockSpec(shape, const, pipeline_mode=pl.Buffered(1))
    out = pl.pallas_call(
        _mix_ffn_kernel, grid=(n + 1,),
        in_specs=[
            pl.BlockSpec((tm, ATTN_WIDTH), merge_tile),
            pl.BlockSpec((tm, GLA_WIDTH), merge_tile),
            pl.BlockSpec((tm, GLA_WIDTH), merge_tile),
            pl.BlockSpec((tm, GLA_WIDTH), merge_tile),
            pl.BlockSpec((tm, D), merge_tile),
            resident((ATTN_WIDTH + GLA_WIDTH, D)),
            pl.BlockSpec((1, GLA_DV), const),
            pl.BlockSpec((1, D), const),
            pl.BlockSpec((1, D), const),
            resident((D, D_FF)), resident((D, D_FF)), resident((D_FF, D)),
            pl.BlockSpec((1, D), const),
            pl.BlockSpec((1, D), const),
        ],
        out_specs=pl.BlockSpec((tm, D), ffn_tile),
        out_shape=jax.ShapeDtypeStruct((tokens, D), F32),
        scratch_shapes=[pltpu.VMEM((2, tm, D), F32), pltpu.VMEM((2, tm, D), BF16)],
        compiler_params=_cparams(("arbitrary",)),
    )(flat(attn), flat(o_f), flat(o_b), flat(go), flat(x), w_out, gng, ln1_g, ln1_b,
      w_gate, w_up, w_down, ln2_g, ln2_b)
    return out.reshape(B, S, D)


def _rope_tables(seq_len):
    t = jnp.arange(seq_len, dtype=jnp.int32)
    row_id = (t // GRID_W).astype(F32)
    col_id = (t % GRID_W).astype(F32)
    inv_freq = ROPE_THETA ** (-jnp.arange(0, AXIAL_DIM, 2, dtype=F32) / AXIAL_DIM)
    lane = jnp.arange(LANES, dtype=jnp.int32)
    d = lane % HEAD_DIM
    is_col = (d // AXIAL_DIM) == 1
    upper = ((d % AXIAL_DIM) // ROPE_HALF) == 1
    ang = jnp.where(is_col, col_id[:, None], row_id[:, None]) * jnp.tile(inv_freq, LANES // ROPE_HALF)
    sin = jnp.sin(ang)
    return jnp.cos(ang), jnp.where(upper, sin, 0.0), jnp.where(upper, 0.0, -sin)


def kernel(x, w_in, q_norm_g, k_norm_g, gate_up_fwd, gate_bias_fwd, gate_up_bwd, gate_bias_bwd, gla_norm_g,
           w_out, ln1_g, ln1_b, w_ffn_gate, w_ffn_up, w_ffn_down, ln2_g, ln2_b):
    B, S, D = x.shape
    assert D == D_MODEL and S % max(PROJ_TM, ATTN_TQ, ATTN_TK, GLA_CHUNK, MIX_FFN_TM) == 0 and B % GLA_BATCH == 0
    assert w_in.shape[0] == DEPTH
    cos, sin_hi, sin_lo = _rope_tables(S)
    for layer in range(DEPTH):
        w_pad = jnp.pad(w_in[layer], ((0, 0), (0, PROJ_PAD_WIDTH - w_in.shape[-1]))).astype(BF16)
        gup = jnp.zeros((Z_PAD, 2 * GLA_QK_WIDTH), F32)
        gup = gup.at[:GATE_RANK, :GLA_QK_WIDTH].set(gate_up_fwd[layer])
        gup = gup.at[GATE_RANK:2 * GATE_RANK, GLA_QK_WIDTH:].set(gate_up_bwd[layer]).astype(BF16)
        gbias = jnp.concatenate([gate_bias_fwd[layer], gate_bias_bwd[layer]])[None, :]
        qg = jnp.tile(q_norm_g[layer], LANES // HEAD_DIM)[None, :]
        kg = jnp.tile(k_norm_g[layer], LANES // HEAD_DIM)[None, :]

        qa, ka, va, gq, gk, gv, go, laf, lab = _proj_call(x, w_pad, gup, gbias, qg, kg, cos, sin_hi, sin_lo)
        score_bound = (jnp.max(jnp.abs(q_norm_g[layer])) * jnp.max(jnp.abs(k_norm_g[layer]))
                       * SCORE_BOUND_SCALE).reshape(1).astype(F32)
        attn = _attn_call(score_bound, qa, ka, va)
        o_f, o_b = _gla_call(gq, gk, gv, laf, lab)
        x = _mix_ffn_call(attn, o_f, o_b, go, x, w_out[layer].astype(BF16), gla_norm_g[layer][None, :],
                          ln1_g[layer][None, :], ln1_b[layer][None, :],
                          w_ffn_gate[layer].astype(BF16), w_ffn_up[layer].astype(BF16),
                          w_ffn_down[layer].astype(BF16), ln2_g[layer][None, :], ln2_b[layer][None, :])
    return x
```

```python
import jax
import jax.numpy as jnp
from jax import lax
from jax.experimental import pallas as pl
from jax.experimental.pallas import tpu as pltpu

F32 = jnp.float32
BF16 = jnp.bfloat16

D_MODEL = 1024
GRID_W = 64
N_Q_HEADS = 8
N_KV_HEADS = 2
Q_PER_KV = N_Q_HEADS // N_KV_HEADS
HEAD_DIM = 64
AXIAL_DIM = HEAD_DIM // 2
ROPE_HALF = AXIAL_DIM // 2
ROPE_THETA = 10000.0
GLA_HEADS = 4
GLA_DK = 64
GLA_DV = 128
GATE_RANK = 16
GATE_TAU = 16.0
GLA_REF_CHUNK = 16
ATTN_WIDTH = N_Q_HEADS * HEAD_DIM
KV_WIDTH = N_KV_HEADS * HEAD_DIM
GLA_QK_WIDTH = GLA_HEADS * GLA_DK
GLA_WIDTH = GLA_HEADS * GLA_DV
D_FF = 2816
DEPTH = 1
DEEPNORM_ALPHA = (2 * DEPTH) ** 0.25
LN_EPS = 1e-5
RMS_EPS = 1e-6
LOG2_E = 1.4426950408889634

LANES = 128
SUBLANES = 8
VMEM_LIMIT_BYTES = 56 * 1024 * 1024

PROJ_TM = 512
ATTN_TQ = 512
ATTN_TK = 256
ATTN_COL = 512
VT_ROWS = HEAD_DIM + 16
SCORE_BOUND_SCALE = LOG2_E * HEAD_DIM ** 0.5 * 1.01
ATTN_UNSHIFTED_MAX_LOG2 = 60.0
GLA_CHUNK = 128
GLA_BATCH = 8
GLA_PHASES = 4
MIX_FFN_TM = 512
FFN_CHUNK = 256

OFF_AQ = 0
OFF_AK = OFF_AQ + ATTN_WIDTH
OFF_AV = OFF_AK + KV_WIDTH
OFF_GQ = OFF_AV + KV_WIDTH
OFF_GK = OFF_GQ + GLA_QK_WIDTH
OFF_GV = OFF_GK + GLA_QK_WIDTH
OFF_GO = OFF_GV + GLA_WIDTH
OFF_Z = OFF_GO + GLA_WIDTH
Z_PAD = LANES
PROJ_PAD_WIDTH = OFF_Z + Z_PAD


def _cparams(semantics):
    return pltpu.CompilerParams(dimension_semantics=semantics, vmem_limit_bytes=VMEM_LIMIT_BYTES)


def _dot(a, b):
    return jnp.dot(a, b, preferred_element_type=F32)


def _dot_nt(a, b):
    return lax.dot_general(a, b, (((1,), (1,)), ((), ())), preferred_element_type=F32)


def _rope128(y, cos, sin_hi, sin_lo):
    return y * cos + pltpu.roll(y, ROPE_HALF, 1) * sin_hi + pltpu.roll(y, LANES - ROPE_HALF, 1) * sin_lo


def _head_pair_inv_rms(blk, lane_lo):
    sq = blk * blk
    ss_lo = jnp.sum(jnp.where(lane_lo, sq, 0.0), axis=-1, keepdims=True)
    ss_hi = jnp.sum(jnp.where(lane_lo, 0.0, sq), axis=-1, keepdims=True)
    inv = 1.0 / HEAD_DIM
    return jnp.where(lane_lo, lax.rsqrt(ss_lo * inv + RMS_EPS), lax.rsqrt(ss_hi * inv + RMS_EPS))


def _proj_kernel(x_ref, w_ref, gup_ref, gbias_ref, qg_ref, kg_ref, cos_ref, shi_ref, slo_ref,
                 qa_ref, ka_ref, va_ref, gq_ref, gk_ref, gv_ref, go_ref, laf_ref, lab_ref,
                 aq_sc, akv_sc, z_sc):
    t = pl.program_id(0)
    cur, prev = t % 2, (t + 1) % 2

    @pl.when(t == 0)
    def _():
        aq_sc[1] = jnp.zeros(aq_sc.shape[1:], F32)
        akv_sc[1] = jnp.zeros(akv_sc.shape[1:], F32)
        z_sc[1] = jnp.zeros(z_sc.shape[1:], F32)

    x = x_ref[0].astype(BF16)
    tm = x.shape[0]
    lane = lax.broadcasted_iota(jnp.int32, (tm, LANES), 1)
    lane_lo = lane < HEAD_DIM
    cos, shi, slo = cos_ref[...], shi_ref[...], slo_ref[...]

    def proj(off, width):
        return _dot(x, w_ref[:, off:off + width])

    def q_epilogue(c):
        blk = aq_sc[prev, :, c * LANES:(c + 1) * LANES]
        rot = _rope128(blk * qg_ref[...], cos, shi, slo)
        out_t = (rot * (_head_pair_inv_rms(blk, lane_lo) * (LOG2_E * HEAD_DIM ** -0.5))).T.astype(BF16)
        qa_ref[0, 2 * c] = out_t[:HEAD_DIM]
        qa_ref[0, 2 * c + 1] = out_t[HEAD_DIM:]

    def k_epilogue():
        ak = akv_sc[prev, :, :KV_WIDTH]
        rot = _rope128(ak * kg_ref[...], cos, shi, slo)
        out_t = (rot * _head_pair_inv_rms(ak, lane_lo)).T.astype(BF16)
        ka_ref[0, 0] = out_t[:HEAD_DIM]
        ka_ref[0, 1] = out_t[HEAD_DIM:]

    def v_epilogue():
        av = akv_sc[prev, :, KV_WIDTH:]
        ones_col = jnp.where(lane == HEAD_DIM, 1.0, 0.0)
        va_ref[0, 0] = jnp.where(lane_lo, av, ones_col).T.astype(BF16)
        va_ref[0, 1] = jnp.where(lane_lo, pltpu.roll(av, HEAD_DIM, 1), ones_col).T.astype(BF16)

    def gate_epilogue():
        g = _dot(z_sc[prev].astype(BF16), gup_ref[...]) + gbias_ref[...]
        log2_a = (jnp.minimum(g, 0.0) - jnp.log(1.0 + jnp.exp(-jnp.abs(g)))) * (LOG2_E / GATE_TAU)
        laf_ref[0] = log2_a[:, :GLA_QK_WIDTH].astype(BF16)
        lab_ref[0] = log2_a[:, GLA_QK_WIDTH:].astype(BF16)

    half = GLA_WIDTH // 2
    gq_ref[0] = (proj(OFF_GQ, GLA_QK_WIDTH) * (GLA_DK ** -0.5)).astype(BF16)
    q_epilogue(0)
    gk_ref[0] = proj(OFF_GK, GLA_QK_WIDTH).astype(BF16)
    q_epilogue(1)
    gv_ref[0, :, :half] = proj(OFF_GV, half).astype(BF16)
    q_epilogue(2)
    gv_ref[0, :, half:] = proj(OFF_GV + half, half).astype(BF16)
    q_epilogue(3)
    go_ref[0, :, :half] = proj(OFF_GO, half).astype(BF16)
    k_epilogue()
    go_ref[0, :, half:] = proj(OFF_GO + half, half).astype(BF16)
    v_epilogue()
    gate_epilogue()
    half_q = ATTN_WIDTH // 2
    aq_sc[cur, :, :half_q] = proj(OFF_AQ, half_q)
    aq_sc[cur, :, half_q:] = proj(OFF_AQ + half_q, half_q)
    akv_sc[cur] = proj(OFF_AK, 2 * KV_WIDTH)
    z_sc[cur] = proj(OFF_Z, Z_PAD)


def _proj_call(x, w_pad, gup, gbias, qg, kg, cos, shi, slo):
    B, S, D = x.shape
    tm = PROJ_TM
    n_s = S // tm
    n = n_s * B
    mm = lambda t: jnp.minimum(t, n - 1)
    ep = lambda t: jnp.maximum(t - 1, 0)
    tok = lambda t: (mm(t) % B, mm(t) // B, 0)
    tok_ep = lambda t: (ep(t) % B, ep(t) // B, 0)
    head_t = lambda t: (ep(t) % B, 0, 0, ep(t) // B)
    const = lambda t: (0, 0)
    tab = lambda t: (ep(t) // B, 0)
    in_specs = [
        pl.BlockSpec((1, tm, D), tok),
        pl.BlockSpec((D, PROJ_PAD_WIDTH), const),
        pl.BlockSpec((Z_PAD, 2 * GLA_QK_WIDTH), const),
        pl.BlockSpec((1, 2 * GLA_QK_WIDTH), const),
        pl.BlockSpec((1, LANES), const),
        pl.BlockSpec((1, LANES), const),
        pl.BlockSpec((tm, LANES), tab),
        pl.BlockSpec((tm, LANES), tab),
        pl.BlockSpec((tm, LANES), tab),
    ]
    out_shape = [
        jax.ShapeDtypeStruct((B, N_Q_HEADS, HEAD_DIM, S), BF16),
        jax.ShapeDtypeStruct((B, N_KV_HEADS, HEAD_DIM, S), BF16),
        jax.ShapeDtypeStruct((B, N_KV_HEADS, LANES, S), BF16),
        jax.ShapeDtypeStruct((B, S, GLA_QK_WIDTH), BF16),
        jax.ShapeDtypeStruct((B, S, GLA_QK_WIDTH), BF16),
        jax.ShapeDtypeStruct((B, S, GLA_WIDTH), BF16),
        jax.ShapeDtypeStruct((B, S, GLA_WIDTH), BF16),
        jax.ShapeDtypeStruct((B, S, GLA_QK_WIDTH), BF16),
        jax.ShapeDtypeStruct((B, S, GLA_QK_WIDTH), BF16),
    ]
    out_specs = [
        pl.BlockSpec((1, N_Q_HEADS, HEAD_DIM, tm), head_t),
        pl.BlockSpec((1, N_KV_HEADS, HEAD_DIM, tm), head_t),
        pl.BlockSpec((1, N_KV_HEADS, LANES, tm), head_t),
        pl.BlockSpec((1, tm, GLA_QK_WIDTH), tok),
        pl.BlockSpec((1, tm, GLA_QK_WIDTH), tok),
        pl.BlockSpec((1, tm, GLA_WIDTH), tok),
        pl.BlockSpec((1, tm, GLA_WIDTH), tok),
        pl.BlockSpec((1, tm, GLA_QK_WIDTH), tok_ep),
        pl.BlockSpec((1, tm, GLA_QK_WIDTH), tok_ep),
    ]
    scratch = [pltpu.VMEM((2, tm, ATTN_WIDTH), F32), pltpu.VMEM((2, tm, 2 * KV_WIDTH), F32),
               pltpu.VMEM((2, tm, Z_PAD), F32)]
    return pl.pallas_call(
        _proj_kernel, grid=(n + 1,), in_specs=in_specs, out_specs=out_specs, out_shape=out_shape,
        scratch_shapes=scratch, compiler_params=_cparams(("arbitrary",)),
    )(x, w_pad, gup, gbias, qg, kg, cos, shi, slo)


def _attn_kernel(bound_ref, qt_ref, kt_ref, vt_ref, o_ref, acc_sc, den_sc):
    t = pl.program_id(0)
    cur, prev = t % 2, (t + 1) % 2
    tq = qt_ref.shape[3]
    n_keys = kt_ref.shape[3]
    tk = ATTN_TK
    key_chunk = lambda j: kt_ref[0, 0, :, j * tk:(j + 1) * tk].T
    q_t = jnp.concatenate([qt_ref[0, r] for r in range(Q_PER_KV)], axis=1)
    unshifted_is_safe = bound_ref[0] < ATTN_UNSHIFTED_MAX_LOG2

    @pl.when(t == 0)
    def _():
        acc_sc[1] = jnp.zeros(acc_sc.shape[1:], F32)
        den_sc[1] = jnp.ones(den_sc.shape[1:], F32)

    def finish_previous():
        o_t = acc_sc[prev] / den_sc[prev]
        for r in range(Q_PER_KV):
            o_ref[0, :, r * HEAD_DIM:(r + 1) * HEAD_DIM] = o_t[:, r * tq:(r + 1) * tq].T.astype(o_ref.dtype)

    def finish(acc, den_row):
        acc_sc[cur] = acc[:HEAD_DIM]
        den_sc[cur] = den_row

    @pl.when(unshifted_is_safe)
    def _():
        n = n_keys // tk
        n_col = q_t.shape[1] // ATTN_COL
        col = lambda c: slice(c * ATTN_COL, (c + 1) * ATTN_COL)
        keys = [key_chunk(j) for j in range(n)]
        scores = lambda j, c: _dot(keys[j], q_t[:, col(c)])
        acc = [None] * n_col
        den = [None] * n_col
        s_cur = [scores(0, c) for c in range(n_col)]
        for j in range(n):
            s_nxt = []
            for c in range(n_col):
                p = jnp.exp2(s_cur[c])
                pv = _dot(vt_ref[0, 0, :HEAD_DIM, j * tk:(j + 1) * tk], p.astype(BF16))
                acc[c] = pv if acc[c] is None else acc[c] + pv
                part = jnp.sum(p.reshape(tk // SUBLANES, SUBLANES, ATTN_COL), axis=0)
                den[c] = part if den[c] is None else den[c] + part
                if j + 1 < n:
                    s_nxt.append(scores(j + 1, c))
            s_cur = s_nxt
            if j == 0:
                finish_previous()
        den_row = jnp.sum(jnp.concatenate(den, axis=1), axis=0, keepdims=True)
        finish(jnp.concatenate(acc, axis=1), den_row)

    @pl.when(jnp.logical_not(unshifted_is_safe))
    def _():
        finish_previous()
        m = None
        acc = None
        for j in range(n_keys // tk):
            s_t = _dot(key_chunk(j), q_t)
            m_blk = jnp.max(s_t, axis=0, keepdims=True)
            m_new = m_blk if m is None else jnp.maximum(m, m_blk)
            pv = _dot(vt_ref[0, 0, :VT_ROWS, j * tk:(j + 1) * tk], jnp.exp2(s_t - m_new).astype(BF16))
            acc = pv if acc is None else jnp.exp2(m - m_new) * acc + pv
            m = m_new
        finish(acc, acc[HEAD_DIM:HEAD_DIM + 1])


def _attn_call(score_bound, qa_t, ka_t, va_t):
    B, _, _, S = ka_t.shape
    tq = ATTN_TQ
    nq = S // tq
    n = B * N_KV_HEADS * nq
    acc_blk = lambda t: jnp.minimum(t, n - 1)
    out_blk = lambda t: jnp.maximum(t - 1, 0)
    b_of = lambda u: u // (N_KV_HEADS * nq)
    g_of = lambda u: (u // nq) % N_KV_HEADS
    return pl.pallas_call(
        _attn_kernel, grid=(n + 1,),
        in_specs=[
            pl.BlockSpec(memory_space=pltpu.SMEM),
            pl.BlockSpec((1, Q_PER_KV, HEAD_DIM, tq), lambda t: (b_of(acc_blk(t)), g_of(acc_blk(t)), 0, acc_blk(t) % nq)),
            pl.BlockSpec((1, 1, HEAD_DIM, S), lambda t: (b_of(acc_blk(t)), g_of(acc_blk(t)), 0, 0)),
            pl.BlockSpec((1, 1, LANES, S), lambda t: (b_of(acc_blk(t)), g_of(acc_blk(t)), 0, 0)),
        ],
        out_specs=pl.BlockSpec((1, tq, Q_PER_KV * HEAD_DIM),
                               lambda t: (b_of(out_blk(t)), out_blk(t) % nq, g_of(out_blk(t)))),
        out_shape=jax.ShapeDtypeStruct((B, S, ATTN_WIDTH), BF16),
        scratch_shapes=[pltpu.VMEM((2, HEAD_DIM, Q_PER_KV * tq), F32), pltpu.VMEM((2, 1, Q_PER_KV * tq), F32)],
        compiler_params=_cparams(("arbitrary",)),
    )(score_bound, qa_t, ka_t, va_t)


def _block_row(a, blk, row):
    n, w = a.shape
    a3 = a.reshape(n // blk, blk, w)
    return jnp.broadcast_to(a3[:, row:row + 1, :], a3.shape).reshape(n, w)


def _gla_direction(bi, q_ref, k_ref, v_ref, la_ref, o_ref, state_ref, forward):
    C = q_ref.shape[1]
    row = lax.broadcasted_iota(jnp.int32, (C, C), 0)
    col = lax.broadcasted_iota(jnp.int32, (C, C), 1)
    tri = ((col <= row) if forward else (col >= row)).astype(BF16)
    la = la_ref[bi]
    cum_all = _dot(tri, la)
    excl_all = cum_all - la.astype(F32)
    xr = row ^ col
    valid = (col <= row) if forward else (col > row)
    lane = lax.broadcasted_iota(jnp.int32, (C, LANES), 1)
    lane_lo = lane < GLA_DK
    edge = C - 1 if forward else 0
    srow = lax.broadcasted_iota(jnp.int32, (LANES, 2 * GLA_DV), 0)
    scol = lax.broadcasted_iota(jnp.int32, (LANES, 2 * GLA_DV), 1)
    on_diag = (srow < GLA_DK) == (scol < GLA_DV)
    pairs = range(GLA_HEADS // 2)
    yield

    factors = []
    for pair in pairs:
        sl = slice(pair * LANES, (pair + 1) * LANES)
        q, k = q_ref[bi, :, sl].astype(F32), k_ref[bi, :, sl].astype(F32)
        cum, excl = cum_all[:, sl], excl_all[:, sl]
        total = cum[edge:edge + 1, :]

        base_row = 0 if forward else GLA_REF_CHUNK - 1
        loc = cum - _block_row(excl, GLA_REF_CHUNK, base_row)
        q_lv = [q * jnp.exp2(loc)]
        k_lv = [(k * jnp.exp2(-loc)).astype(BF16)]
        blk = 2 * GLA_REF_CHUNK
        while blk <= C:
            mid = _block_row(excl if forward else cum, blk, blk // 2)
            e = jnp.exp2(-jnp.abs(cum - mid))
            q_lv.append(q * e)
            k_lv.append((k * e).astype(BF16))
            blk *= 2
        q_in = (q * jnp.exp2(cum)).astype(BF16)
        k_out = k * jnp.exp2(total - cum)
        k_out_t = k_out.T.astype(BF16)
        decay_t = jnp.exp2(total).T
        factors.append((q_lv, k_lv, q_in, k_out_t, decay_t))
    yield

    def stack_heads(a):
        return jnp.concatenate([jnp.where(lane_lo, a, 0.0), jnp.where(lane_lo, 0.0, a)], axis=0).astype(BF16)

    xr2 = jnp.concatenate([xr, xr], axis=0)
    valid2 = jnp.concatenate([valid, valid], axis=0)
    scores = []
    for pair in pairs:
        q_lv, k_lv = factors[pair][:2]
        scores2 = _dot_nt(stack_heads(q_lv[-1]), k_lv[-1])
        bound = C // 2
        for lv in range(len(q_lv) - 2, -1, -1):
            scores2 = jnp.where(xr2 < bound, _dot_nt(stack_heads(q_lv[lv]), k_lv[lv]), scores2)
            bound //= 2
        scores.append(jnp.where(valid2, scores2, 0.0).astype(BF16))
    yield

    for pair in pairs:
        q_in, k_out_t, decay_t = factors[pair][2:]
        v2 = v_ref[bi, :, 2 * pair * GLA_DV:2 * (pair + 1) * GLA_DV]
        state = state_ref[bi, pair]
        inter = _dot(q_in, state.astype(BF16))
        for hh in range(2):
            h = 2 * pair + hh
            cols = slice(hh * GLA_DV, (hh + 1) * GLA_DV)
            o = _dot(scores[pair][hh * C:(hh + 1) * C], v2[:, cols]) + inter[:, cols]
            o_ref[bi, :, h * GLA_DV:(h + 1) * GLA_DV] = o.astype(o_ref.dtype)
        state_ref[bi, pair] = jnp.where(on_diag, decay_t * state + _dot(k_out_t, v2), 0.0)
    yield


def _gla_kernel(qf_ref, kf_ref, vf_ref, laf_ref, qb_ref, kb_ref, vb_ref, lab_ref,
                of_ref, ob_ref, sf_ref, sb_ref):
    @pl.when(pl.program_id(1) == 0)
    def _():
        sf_ref[...] = jnp.zeros(sf_ref.shape, F32)
        sb_ref[...] = jnp.zeros(sb_ref.shape, F32)

    chains = []
    for bi in range(GLA_BATCH):
        chains.append(_gla_direction(bi, qf_ref, kf_ref, vf_ref, laf_ref, of_ref, sf_ref, True))
        chains.append(_gla_direction(bi, qb_ref, kb_ref, vb_ref, lab_ref, ob_ref, sb_ref, False))
    for _ in range(GLA_PHASES):
        for chain in chains:
            next(chain)


def _gla_call(gq, gk, gv, laf, lab):
    B, S, _ = gq.shape
    C, nb = GLA_CHUNK, GLA_BATCH
    n = S // C
    fwd = lambda b, c: (b, c, 0)
    bwd = lambda b, c: (b, n - 1 - c, 0)
    qk = lambda im: pl.BlockSpec((nb, C, GLA_QK_WIDTH), im)
    vv = lambda im: pl.BlockSpec((nb, C, GLA_WIDTH), im)
    state = pltpu.VMEM((nb, GLA_HEADS // 2, LANES, 2 * GLA_DV), F32)
    return pl.pallas_call(
        _gla_kernel, grid=(B // nb, n),
        in_specs=[qk(fwd), qk(fwd), vv(fwd), qk(fwd), qk(bwd), qk(bwd), vv(bwd), qk(bwd)],
        out_specs=[vv(fwd), vv(bwd)],
        out_shape=[jax.ShapeDtypeStruct((B, S, GLA_WIDTH), BF16)] * 2,
        scratch_shapes=[state, state],
        compiler_params=_cparams(("arbitrary", "arbitrary")),
    )(gq, gk, gv, laf, gq, gk, gv, lab)


def _layer_norm(y, g, b):
    mu = jnp.mean(y, axis=-1, keepdims=True)
    d = y - mu
    var = jnp.mean(d * d, axis=-1, keepdims=True)
    return d * lax.rsqrt(var + LN_EPS) * g + b


def _silu(g):
    return g * (1.0 / (1.0 + jnp.exp(-g)))


def _mix_ffn_kernel(attn_ref, of_ref, ob_ref, go_ref, x_ref, wo_ref, gng_ref, ln1g_ref, ln1b_ref,
                    wg_ref, wu_ref, wd_ref, ln2g_ref, ln2b_ref, out_ref, x1_sc, x1b_sc):
    t = pl.program_id(0)
    cur, prev = t % 2, (t + 1) % 2

    @pl.when(t == 0)
    def _():
        x1_sc[1] = jnp.zeros(x1_sc.shape[1:], F32)
        x1b_sc[1] = jnp.zeros(x1b_sc.shape[1:], BF16)

    hidden = []

    def ffn_chunk(c):
        sl = slice(c * FFN_CHUNK, (c + 1) * FFN_CHUNK)
        xb = x1b_sc[prev]
        hidden.append((_silu(_dot(xb, wg_ref[:, sl])) * _dot(xb, wu_ref[:, sl])).astype(BF16))

    def gate_head(h):
        sl = slice(h * GLA_DV, (h + 1) * GLA_DV)
        o = of_ref[:, sl].astype(F32) + ob_ref[:, sl].astype(F32)
        inv = lax.rsqrt(jnp.mean(o * o, axis=-1, keepdims=True) + RMS_EPS)
        return ((o * inv * gng_ref[...]) * _silu(go_ref[:, sl].astype(F32))).astype(BF16)

    n_chunks = D_FF // FFN_CHUNK
    gated = []
    for c in range(GLA_HEADS):
        ffn_chunk(c)
        gated.append(gate_head(c))
    ffn_chunk(GLA_HEADS)
    merged = jnp.concatenate([attn_ref[...]] + gated, axis=-1)
    mixed = _dot(merged, wo_ref[...])
    ffn_chunk(GLA_HEADS + 1)
    ffn_chunk(GLA_HEADS + 2)
    x1 = _layer_norm(DEEPNORM_ALPHA * x_ref[...] + mixed, ln1g_ref[...], ln1b_ref[...])
    x1_sc[cur] = x1
    x1b_sc[cur] = x1.astype(BF16)
    for c in range(GLA_HEADS + 3, n_chunks):
        ffn_chunk(c)
    ffn = _dot(jnp.concatenate(hidden, axis=-1), wd_ref[...])
    out_ref[...] = _layer_norm(DEEPNORM_ALPHA * x1_sc[prev] + ffn, ln2g_ref[...], ln2b_ref[...])


def _mix_ffn_call(attn, o_f, o_b, go, x, w_out, gng, ln1_g, ln1_b, w_gate, w_up, w_down, ln2_g, ln2_b):
    B, S, D = x.shape
    tm = MIX_FFN_TM
    tokens = B * S
    n = tokens // tm
    flat = lambda a: a.reshape(tokens, a.shape[-1])
    merge_tile = lambda t: (jnp.minimum(t, n - 1), 0)
    ffn_tile = lambda t: (jnp.maximum(t - 1, 0), 0)
    const = lambda t: (0, 0)
    resident = lambda shape: pl.BlockSpec(shape, const, pipeline_mode=pl.Buffered(1))
    out = pl.pallas_call(
        _mix_ffn_kernel, grid=(n + 1,),
        in_specs=[
            pl.BlockSpec((tm, ATTN_WIDTH), merge_tile),
            pl.BlockSpec((tm, GLA_WIDTH), merge_tile),
            pl.BlockSpec((tm, GLA_WIDTH), merge_tile),
            pl.BlockSpec((tm, GLA_WIDTH), merge_tile),
            pl.BlockSpec((tm, D), merge_tile),
            resident((ATTN_WIDTH + GLA_WIDTH, D)),
            pl.BlockSpec((1, GLA_DV), const),
            pl.BlockSpec((1, D), const),
            pl.BlockSpec((1, D), const),
            resident((D, D_FF)), resident((D, D_FF)), resident((D_FF, D)),
            pl.BlockSpec((1, D), const),
            pl.BlockSpec((1, D), const),
        ],
        out_specs=pl.BlockSpec((tm, D), ffn_tile),
        out_shape=jax.ShapeDtypeStruct((tokens, D), F32),
        scratch_shapes=[pltpu.VMEM((2, tm, D), F32), pltpu.VMEM((2, tm, D), BF16)],
        compiler_params=_cparams(("arbitrary",)),
    )(flat(attn), flat(o_f), flat(o_b), flat(go), flat(x), w_out, gng, ln1_g, ln1_b,
      w_gate, w_up, w_down, ln2_g, ln2_b)
    return out.reshape(B, S, D)


def _rope_tables(seq_len):
    t = jnp.arange(seq_len, dtype=jnp.int32)
    row_id = (t // GRID_W).astype(F32)
    col_id = (t % GRID_W).astype(F32)
    inv_freq = ROPE_THETA ** (-jnp.arange(0, AXIAL_DIM, 2, dtype=F32) / AXIAL_DIM)
    lane = jnp.arange(LANES, dtype=jnp.int32)
    d = lane % HEAD_DIM
    is_col = (d // AXIAL_DIM) == 1
    upper = ((d % AXIAL_DIM) // ROPE_HALF) == 1
    ang = jnp.where(is_col, col_id[:, None], row_id[:, None]) * jnp.tile(inv_freq, LANES // ROPE_HALF)
    sin = jnp.sin(ang)
    return jnp.cos(ang), jnp.where(upper, sin, 0.0), jnp.where(upper, 0.0, -sin)


def kernel(x, w_in, q_norm_g, k_norm_g, gate_up_fwd, gate_bias_fwd, gate_up_bwd, gate_bias_bwd, gla_norm_g,
           w_out, ln1_g, ln1_b, w_ffn_gate, w_ffn_up, w_ffn_down, ln2_g, ln2_b):
    B, S, D = x.shape
    assert D == D_MODEL and S % max(PROJ_TM, ATTN_TQ, ATTN_TK, GLA_CHUNK, MIX_FFN_TM) == 0 and B % GLA_BATCH == 0
    assert w_in.shape[0] == DEPTH
    cos, sin_hi, sin_lo = _rope_tables(S)
    for layer in range(DEPTH):
        w_pad = jnp.pad(w_in[layer], ((0, 0), (0, PROJ_PAD_WIDTH - w_in.shape[-1]))).astype(BF16)
        gup = jnp.zeros((Z_PAD, 2 * GLA_QK_WIDTH), F32)
        gup = gup.at[:GATE_RANK, :GLA_QK_WIDTH].set(gate_up_fwd[layer])
        gup = gup.at[GATE_RANK:2 * GATE_RANK, GLA_QK_WIDTH:].set(gate_up_bwd[layer]).astype(BF16)
        gbias = jnp.concatenate([gate_bias_fwd[layer], gate_bias_bwd[layer]])[None, :]
        qg = jnp.tile(q_norm_g[layer], LANES // HEAD_DIM)[None, :]
        kg = jnp.tile(k_norm_g[layer], LANES // HEAD_DIM)[None, :]

        qa, ka, va, gq, gk, gv, go, laf, lab = _proj_call(x, w_pad, gup, gbias, qg, kg, cos, sin_hi, sin_lo)
        score_bound = (jnp.max(jnp.abs(q_norm_g[layer])) * jnp.max(jnp.abs(k_norm_g[layer]))
                       * SCORE_BOUND_SCALE).reshape(1).astype(F32)
        attn = _attn_call(score_bound, qa, ka, va)
        o_f, o_b = _gla_call(gq, gk, gv, laf, lab)
        x = _mix_ffn_call(attn, o_f, o_b, go, x, w_out[layer].astype(BF16), gla_norm_g[layer][None, :],
                          ln1_g[layer][None, :], ln1_b[layer][None, :],
                          w_ffn_gate[layer].astype(BF16), w_ffn_up[layer].astype(BF16),
                          w_ffn_down[layer].astype(BF16), ln2_g[layer][None, :], ln2_b[layer][None, :])
    return x
```

```python
import jax
import jax.numpy as jnp
from jax import lax
from jax.experimental import pallas as pl
from jax.experimental.pallas import tpu as pltpu

F32 = jnp.float32
BF16 = jnp.bfloat16

D_MODEL = 1024
GRID_W = 64
N_Q_HEADS = 8
N_KV_HEADS = 2
Q_PER_KV = N_Q_HEADS // N_KV_HEADS
HEAD_DIM = 64
AXIAL_DIM = HEAD_DIM // 2
ROPE_HALF = AXIAL_DIM // 2
ROPE_THETA = 10000.0
GLA_HEADS = 4
GLA_DK = 64
GLA_DV = 128
GATE_RANK = 16
GATE_TAU = 16.0
GLA_REF_CHUNK = 16
ATTN_WIDTH = N_Q_HEADS * HEAD_DIM
KV_WIDTH = N_KV_HEADS * HEAD_DIM
GLA_QK_WIDTH = GLA_HEADS * GLA_DK
GLA_WIDTH = GLA_HEADS * GLA_DV
D_FF = 2816
DEPTH = 1
DEEPNORM_ALPHA = (2 * DEPTH) ** 0.25
LN_EPS = 1e-5
RMS_EPS = 1e-6
LOG2_E = 1.4426950408889634

LANES = 128
SUBLANES = 8
VMEM_LIMIT_BYTES = 56 * 1024 * 1024

PROJ_TM = 1024
ATTN_TQ = 512
ATTN_TK = 256
ATTN_COL = 512
VT_ROWS = HEAD_DIM + 16
SCORE_BOUND_SCALE = LOG2_E * HEAD_DIM ** 0.5 * 1.01
ATTN_UNSHIFTED_MAX_LOG2 = 60.0
GLA_CHUNK = 128
GLA_BATCH = 8
GLA_PHASES = 4
MIX_FFN_TM = 512
FFN_CHUNK = 256

OFF_AQ = 0
OFF_AK = OFF_AQ + ATTN_WIDTH
OFF_AV = OFF_AK + KV_WIDTH
OFF_GQ = OFF_AV + KV_WIDTH
OFF_GK = OFF_GQ + GLA_QK_WIDTH
OFF_GV = OFF_GK + GLA_QK_WIDTH
OFF_GO = OFF_GV + GLA_WIDTH
OFF_Z = OFF_GO + GLA_WIDTH
Z_PAD = LANES
PROJ_PAD_WIDTH = OFF_Z + Z_PAD


def _cparams(semantics):
    return pltpu.CompilerParams(dimension_semantics=semantics, vmem_limit_bytes=VMEM_LIMIT_BYTES)


def _dot(a, b):
    return jnp.dot(a, b, preferred_element_type=F32)


def _dot_nt(a, b):
    return lax.dot_general(a, b, (((1,), (1,)), ((), ())), preferred_element_type=F32)


def _rope128(y, cos, sin_hi, sin_lo):
    return y * cos + pltpu.roll(y, ROPE_HALF, 1) * sin_hi + pltpu.roll(y, LANES - ROPE_HALF, 1) * sin_lo


def _head_pair_inv_rms(blk, lane_lo):
    sq = blk * blk
    ss_lo = jnp.sum(jnp.where(lane_lo, sq, 0.0), axis=-1, keepdims=True)
    ss_hi = jnp.sum(jnp.where(lane_lo, 0.0, sq), axis=-1, keepdims=True)
    inv = 1.0 / HEAD_DIM
    return jnp.where(lane_lo, lax.rsqrt(ss_lo * inv + RMS_EPS), lax.rsqrt(ss_hi * inv + RMS_EPS))


def _proj_kernel(x_ref, w_ref, gup_ref, gbias_ref, qg_ref, kg_ref, cos_ref, shi_ref, slo_ref,
                 qa_ref, ka_ref, va_ref, gq_ref, gk_ref, gv_ref, go_ref, laf_ref, lab_ref,
                 aq_sc, akv_sc, z_sc):
    t = pl.program_id(0)
    cur, prev = t % 2, (t + 1) % 2

    @pl.when(t == 0)
    def _():
        aq_sc[1] = jnp.zeros(aq_sc.shape[1:], F32)
        akv_sc[1] = jnp.zeros(akv_sc.shape[1:], F32)
        z_sc[1] = jnp.zeros(z_sc.shape[1:], F32)

    x = x_ref[0].astype(BF16)
    tm = x.shape[0]
    lane = lax.broadcasted_iota(jnp.int32, (tm, LANES), 1)
    lane_lo = lane < HEAD_DIM
    cos, shi, slo = cos_ref[...], shi_ref[...], slo_ref[...]

    def proj(off, width):
        return _dot(x, w_ref[:, off:off + width])

    def q_epilogue(c):
        blk = aq_sc[prev, :, c * LANES:(c + 1) * LANES]
        rot = _rope128(blk * qg_ref[...], cos, shi, slo)
        out_t = (rot * (_head_pair_inv_rms(blk, lane_lo) * (LOG2_E * HEAD_DIM ** -0.5))).T.astype(BF16)
        qa_ref[0, 2 * c] = out_t[:HEAD_DIM]
        qa_ref[0, 2 * c + 1] = out_t[HEAD_DIM:]

    def k_epilogue():
        ak = akv_sc[prev, :, :KV_WIDTH]
        rot = _rope128(ak * kg_ref[...], cos, shi, slo)
        out_t = (rot * _head_pair_inv_rms(ak, lane_lo)).T.astype(BF16)
        ka_ref[0, 0] = out_t[:HEAD_DIM]
        ka_ref[0, 1] = out_t[HEAD_DIM:]

    def v_epilogue():
        av = akv_sc[prev, :, KV_WIDTH:]
        ones_col = jnp.where(lane == HEAD_DIM, 1.0, 0.0)
        va_ref[0, 0] = jnp.where(lane_lo, av, ones_col).T.astype(BF16)
        va_ref[0, 1] = jnp.where(lane_lo, pltpu.roll(av, HEAD_DIM, 1), ones_col).T.astype(BF16)

    def gate_epilogue():
        g = _dot(z_sc[prev].astype(BF16), gup_ref[...]) + gbias_ref[...]
        log2_a = (jnp.minimum(g, 0.0) - jnp.log(1.0 + jnp.exp(-jnp.abs(g)))) * (LOG2_E / GATE_TAU)
        laf_ref[0] = log2_a[:, :GLA_QK_WIDTH]
        lab_ref[0] = log2_a[:, GLA_QK_WIDTH:]

    half = GLA_WIDTH // 2
    gq_ref[0] = (proj(OFF_GQ, GLA_QK_WIDTH) * (GLA_DK ** -0.5)).astype(BF16)
    q_epilogue(0)
    gk_ref[0] = proj(OFF_GK, GLA_QK_WIDTH).astype(BF16)
    q_epilogue(1)
    gv_ref[0, :, :half] = proj(OFF_GV, half).astype(BF16)
    q_epilogue(2)
    gv_ref[0, :, half:] = proj(OFF_GV + half, half).astype(BF16)
    q_epilogue(3)
    go_ref[0, :, :half] = proj(OFF_GO, half).astype(BF16)
    k_epilogue()
    go_ref[0, :, half:] = proj(OFF_GO + half, half).astype(BF16)
    v_epilogue()
    gate_epilogue()
    half_q = ATTN_WIDTH // 2
    aq_sc[cur, :, :half_q] = proj(OFF_AQ, half_q)
    aq_sc[cur, :, half_q:] = proj(OFF_AQ + half_q, half_q)
    akv_sc[cur] = proj(OFF_AK, 2 * KV_WIDTH)
    z_sc[cur] = proj(OFF_Z, Z_PAD)


def _proj_call(x, w_pad, gup, gbias, qg, kg, cos, shi, slo):
    B, S, D = x.shape
    tm = PROJ_TM
    n_s = S // tm
    n = n_s * B
    mm = lambda t: jnp.minimum(t, n - 1)
    ep = lambda t: jnp.maximum(t - 1, 0)
    tok = lambda t: (mm(t) % B, mm(t) // B, 0)
    tok_ep = lambda t: (ep(t) % B, ep(t) // B, 0)
    head_t = lambda t: (ep(t) % B, 0, 0, ep(t) // B)
    const = lambda t: (0, 0)
    tab = lambda t: (ep(t) // B, 0)
    in_specs = [
        pl.BlockSpec((1, tm, D), tok),
        pl.BlockSpec((D, PROJ_PAD_WIDTH), const),
        pl.BlockSpec((Z_PAD, 2 * GLA_QK_WIDTH), const),
        pl.BlockSpec((1, 2 * GLA_QK_WIDTH), const),
        pl.BlockSpec((1, LANES), const),
        pl.BlockSpec((1, LANES), const),
        pl.BlockSpec((tm, LANES), tab),
        pl.BlockSpec((tm, LANES), tab),
        pl.BlockSpec((tm, LANES), tab),
    ]
    out_shape = [
        jax.ShapeDtypeStruct((B, N_Q_HEADS, HEAD_DIM, S), BF16),
        jax.ShapeDtypeStruct((B, N_KV_HEADS, HEAD_DIM, S), BF16),
        jax.ShapeDtypeStruct((B, N_KV_HEADS, LANES, S), BF16),
        jax.ShapeDtypeStruct((B, S, GLA_QK_WIDTH), BF16),
        jax.ShapeDtypeStruct((B, S, GLA_QK_WIDTH), BF16),
        jax.ShapeDtypeStruct((B, S, GLA_WIDTH), BF16),
        jax.ShapeDtypeStruct((B, S, GLA_WIDTH), BF16),
        jax.ShapeDtypeStruct((B, S, GLA_QK_WIDTH), F32),
        jax.ShapeDtypeStruct((B, S, GLA_QK_WIDTH), F32),
    ]
    out_specs = [
        pl.BlockSpec((1, N_Q_HEADS, HEAD_DIM, tm), head_t),
        pl.BlockSpec((1, N_KV_HEADS, HEAD_DIM, tm), head_t),
        pl.BlockSpec((1, N_KV_HEADS, LANES, tm), head_t),
        pl.BlockSpec((1, tm, GLA_QK_WIDTH), tok),
        pl.BlockSpec((1, tm, GLA_QK_WIDTH), tok),
        pl.BlockSpec((1, tm, GLA_WIDTH), tok),
        pl.BlockSpec((1, tm, GLA_WIDTH), tok),
        pl.BlockSpec((1, tm, GLA_QK_WIDTH), tok_ep),
        pl.BlockSpec((1, tm, GLA_QK_WIDTH), tok_ep),
    ]
    scratch = [pltpu.VMEM((2, tm, ATTN_WIDTH), F32), pltpu.VMEM((2, tm, 2 * KV_WIDTH), F32),
               pltpu.VMEM((2, tm, Z_PAD), F32)]
    return pl.pallas_call(
        _proj_kernel, grid=(n + 1,), in_specs=in_specs, out_specs=out_specs, out_shape=out_shape,
        scratch_shapes=scratch, compiler_params=_cparams(("arbitrary",)),
    )(x, w_pad, gup, gbias, qg, kg, cos, shi, slo)


def _attn_kernel(bound_ref, qt_ref, kt_ref, vt_ref, o_ref, acc_sc, den_sc):
    t = pl.program_id(0)
    cur, prev = t % 2, (t + 1) % 2
    tq = qt_ref.shape[3]
    n_keys = kt_ref.shape[3]
    tk = ATTN_TK
    key_chunk = lambda j: kt_ref[0, 0, :, j * tk:(j + 1) * tk].T
    q_t = jnp.concatenate([qt_ref[0, r] for r in range(Q_PER_KV)], axis=1)
    unshifted_is_safe = bound_ref[0] < ATTN_UNSHIFTED_MAX_LOG2

    @pl.when(t == 0)
    def _():
        acc_sc[1] = jnp.zeros(acc_sc.shape[1:], F32)
        den_sc[1] = jnp.ones(den_sc.shape[1:], F32)

    def finish_previous():
        o_t = acc_sc[prev] / den_sc[prev]
        for r in range(Q_PER_KV):
            o_ref[0, :, r * HEAD_DIM:(r + 1) * HEAD_DIM] = o_t[:, r * tq:(r + 1) * tq].T.astype(o_ref.dtype)

    def finish(acc, den_row):
        acc_sc[cur] = acc[:HEAD_DIM]
        den_sc[cur] = den_row

    @pl.when(unshifted_is_safe)
    def _():
        n = n_keys // tk
        n_col = q_t.shape[1] // ATTN_COL
        col = lambda c: slice(c * ATTN_COL, (c + 1) * ATTN_COL)
        keys = [key_chunk(j) for j in range(n)]
        scores = lambda j, c: _dot(keys[j], q_t[:, col(c)])
        acc = [None] * n_col
        den = [None] * n_col
        s_cur = [scores(0, c) for c in range(n_col)]
        for j in range(n):
            s_nxt = []
            for c in range(n_col):
                p = jnp.exp2(s_cur[c])
                pv = _dot(vt_ref[0, 0, :HEAD_DIM, j * tk:(j + 1) * tk], p.astype(BF16))
                acc[c] = pv if acc[c] is None else acc[c] + pv
                part = jnp.sum(p.reshape(tk // SUBLANES, SUBLANES, ATTN_COL), axis=0)
                den[c] = part if den[c] is None else den[c] + part
                if j + 1 < n:
                    s_nxt.append(scores(j + 1, c))
            s_cur = s_nxt
            if j == 0:
                finish_previous()
        den_row = jnp.sum(jnp.concatenate(den, axis=1), axis=0, keepdims=True)
        finish(jnp.concatenate(acc, axis=1), den_row)

    @pl.when(jnp.logical_not(unshifted_is_safe))
    def _():
        finish_previous()
        m = None
        acc = None
        for j in range(n_keys // tk):
            s_t = _dot(key_chunk(j), q_t)
            m_blk = jnp.max(s_t, axis=0, keepdims=True)
            m_new = m_blk if m is None else jnp.maximum(m, m_blk)
            pv = _dot(vt_ref[0, 0, :VT_ROWS, j * tk:(j + 1) * tk], jnp.exp2(s_t - m_new).astype(BF16))
            acc = pv if acc is None else jnp.exp2(m - m_new) * acc + pv
            m = m_new
        finish(acc, acc[HEAD_DIM:HEAD_DIM + 1])


def _attn_call(score_bound, qa_t, ka_t, va_t):
    B, _, _, S = ka_t.shape
    tq = ATTN_TQ
    nq = S // tq
    n = B * N_KV_HEADS * nq
    acc_blk = lambda t: jnp.minimum(t, n - 1)
    out_blk = lambda t: jnp.maximum(t - 1, 0)
    b_of = lambda u: u // (N_KV_HEADS * nq)
    g_of = lambda u: (u // nq) % N_KV_HEADS
    return pl.pallas_call(
        _attn_kernel, grid=(n + 1,),
        in_specs=[
            pl.BlockSpec(memory_space=pltpu.SMEM),
            pl.BlockSpec((1, Q_PER_KV, HEAD_DIM, tq), lambda t: (b_of(acc_blk(t)), g_of(acc_blk(t)), 0, acc_blk(t) % nq)),
            pl.BlockSpec((1, 1, HEAD_DIM, S), lambda t: (b_of(acc_blk(t)), g_of(acc_blk(t)), 0, 0)),
            pl.BlockSpec((1, 1, LANES, S), lambda t: (b_of(acc_blk(t)), g_of(acc_blk(t)), 0, 0)),
        ],
        out_specs=pl.BlockSpec((1, tq, Q_PER_KV * HEAD_DIM),
                               lambda t: (b_of(out_blk(t)), out_blk(t) % nq, g_of(out_blk(t)))),
        out_shape=jax.ShapeDtypeStruct((B, S, ATTN_WIDTH), BF16),
        scratch_shapes=[pltpu.VMEM((2, HEAD_DIM, Q_PER_KV * tq), F32), pltpu.VMEM((2, 1, Q_PER_KV * tq), F32)],
        compiler_params=_cparams(("arbitrary",)),
    )(score_bound, qa_t, ka_t, va_t)


def _block_row(a, blk, row):
    n, w = a.shape
    a3 = a.reshape(n // blk, blk, w)
    return jnp.broadcast_to(a3[:, row:row + 1, :], a3.shape).reshape(n, w)


def _split3_bf16(a):
    hi = a.astype(BF16)
    r1 = a - hi.astype(F32)
    mid = r1.astype(BF16)
    lo = (r1 - mid.astype(F32)).astype(BF16)
    return hi, mid, lo


def _gla_direction(bi, q_ref, k_ref, v_ref, la_ref, o_ref, state_ref, forward):
    C = q_ref.shape[1]
    row = lax.broadcasted_iota(jnp.int32, (C, C), 0)
    col = lax.broadcasted_iota(jnp.int32, (C, C), 1)
    tri = ((col <= row) if forward else (col >= row)).astype(BF16)
    la = la_ref[bi]
    cum_all = sum(_dot(tri, part) for part in _split3_bf16(la))
    excl_all = cum_all - la
    xr = row ^ col
    valid = (col <= row) if forward else (col > row)
    lane = lax.broadcasted_iota(jnp.int32, (C, LANES), 1)
    lane_lo = lane < GLA_DK
    edge = C - 1 if forward else 0
    srow = lax.broadcasted_iota(jnp.int32, (LANES, 2 * GLA_DV), 0)
    scol = lax.broadcasted_iota(jnp.int32, (LANES, 2 * GLA_DV), 1)
    on_diag = (srow < GLA_DK) == (scol < GLA_DV)
    pairs = range(GLA_HEADS // 2)
    yield

    factors = []
    for pair in pairs:
        sl = slice(pair * LANES, (pair + 1) * LANES)
        q, k = q_ref[bi, :, sl].astype(F32), k_ref[bi, :, sl].astype(F32)
        cum, excl = cum_all[:, sl], excl_all[:, sl]
        total = cum[edge:edge + 1, :]

        base_row = 0 if forward else GLA_REF_CHUNK - 1
        loc = cum - _block_row(excl, GLA_REF_CHUNK, base_row)
        q_lv = [q * jnp.exp2(loc)]
        k_lv = [(k * jnp.exp2(-loc)).astype(BF16)]
        blk = 2 * GLA_REF_CHUNK
        while blk <= C:
            mid = _block_row(excl if forward else cum, blk, blk // 2)
            e = jnp.exp2(-jnp.abs(cum - mid))
            q_lv.append(q * e)
            k_lv.append((k * e).astype(BF16))
            blk *= 2
        q_in = (q * jnp.exp2(cum)).astype(BF16)
        k_out = k * jnp.exp2(total - cum)
        k_out_t = k_out.T.astype(BF16)
        decay_t = jnp.exp2(total).T
        factors.append((q_lv, k_lv, q_in, k_out_t, decay_t))
    yield

    def stack_heads(a):
        return jnp.concatenate([jnp.where(lane_lo, a, 0.0), jnp.where(lane_lo, 0.0, a)], axis=0).astype(BF16)

    xr2 = jnp.concatenate([xr, xr], axis=0)
    valid2 = jnp.concatenate([valid, valid], axis=0)
    scores = []
    for pair in pairs:
        q_lv, k_lv = factors[pair][:2]
        scores2 = _dot_nt(stack_heads(q_lv[-1]), k_lv[-1])
        bound = C // 2
        for lv in range(len(q_lv) - 2, -1, -1):
            scores2 = jnp.where(xr2 < bound, _dot_nt(stack_heads(q_lv[lv]), k_lv[lv]), scores2)
            bound //= 2
        scores.append(jnp.where(valid2, scores2, 0.0).astype(BF16))
    yield

    for pair in pairs:
        q_in, k_out_t, decay_t = factors[pair][2:]
        v2 = v_ref[bi, :, 2 * pair * GLA_DV:2 * (pair + 1) * GLA_DV]
        state = state_ref[bi, pair]
        inter = _dot(q_in, state.astype(BF16))
        for hh in range(2):
            h = 2 * pair + hh
            cols = slice(hh * GLA_DV, (hh + 1) * GLA_DV)
            o = _dot(scores[pair][hh * C:(hh + 1) * C], v2[:, cols]) + inter[:, cols]
            o_ref[bi, :, h * GLA_DV:(h + 1) * GLA_DV] = o.astype(o_ref.dtype)
        state_ref[bi, pair] = jnp.where(on_diag, decay_t * state + _dot(k_out_t, v2), 0.0)
    yield


def _gla_kernel(qf_ref, kf_ref, vf_ref, laf_ref, qb_ref, kb_ref, vb_ref, lab_ref,
                of_ref, ob_ref, sf_ref, sb_ref):
    @pl.when(pl.program_id(1) == 0)
    def _():
        sf_ref[...] = jnp.zeros(sf_ref.shape, F32)
        sb_ref[...] = jnp.zeros(sb_ref.shape, F32)

    chains = []
    for bi in range(GLA_BATCH):
        chains.append(_gla_direction(bi, qf_ref, kf_ref, vf_ref, laf_ref, of_ref, sf_ref, True))
        chains.append(_gla_direction(bi, qb_ref, kb_ref, vb_ref, lab_ref, ob_ref, sb_ref, False))
    for _ in range(GLA_PHASES):
        for chain in chains:
            next(chain)


def _gla_call(gq, gk, gv, laf, lab):
    B, S, _ = gq.shape
    C, nb = GLA_CHUNK, GLA_BATCH
    n = S // C
    fwd = lambda b, c: (b, c, 0)
    bwd = lambda b, c: (b, n - 1 - c, 0)
    qk = lambda im: pl.BlockSpec((nb, C, GLA_QK_WIDTH), im)
    vv = lambda im: pl.BlockSpec((nb, C, GLA_WIDTH), im)
    state = pltpu.VMEM((nb, GLA_HEADS // 2, LANES, 2 * GLA_DV), F32)
    return pl.pallas_call(
        _gla_kernel, grid=(B // nb, n),
        in_specs=[qk(fwd), qk(fwd), vv(fwd), qk(fwd), qk(bwd), qk(bwd), vv(bwd), qk(bwd)],
        out_specs=[vv(fwd), vv(bwd)],
        out_shape=[jax.ShapeDtypeStruct((B, S, GLA_WIDTH), BF16)] * 2,
        scratch_shapes=[state, state],
        compiler_params=_cparams(("arbitrary", "arbitrary")),
    )(gq, gk, gv, laf, gq, gk, gv, lab)


def _layer_norm(y, g, b):
    mu = jnp.mean(y, axis=-1, keepdims=True)
    d = y - mu
    var = jnp.mean(d * d, axis=-1, keepdims=True)
    return d * lax.rsqrt(var + LN_EPS) * g + b


def _silu(g):
    return g * (1.0 / (1.0 + jnp.exp(-g)))


def _mix_ffn_kernel(attn_ref, of_ref, ob_ref, go_ref, x_ref, wo_ref, gng_ref, ln1g_ref, ln1b_ref,
                    wg_ref, wu_ref, wd_ref, ln2g_ref, ln2b_ref, out_ref, x1_sc, x1b_sc):
    t = pl.program_id(0)
    cur, prev = t % 2, (t + 1) % 2

    @pl.when(t == 0)
    def _():
        x1_sc[1] = jnp.zeros(x1_sc.shape[1:], F32)
        x1b_sc[1] = jnp.zeros(x1b_sc.shape[1:], BF16)

    hidden = []

    def ffn_chunk(c):
        sl = slice(c * FFN_CHUNK, (c + 1) * FFN_CHUNK)
        xb = x1b_sc[prev]
        hidden.append((_silu(_dot(xb, wg_ref[:, sl])) * _dot(xb, wu_ref[:, sl])).astype(BF16))

    def gate_head(h):
        sl = slice(h * GLA_DV, (h + 1) * GLA_DV)
        o = of_ref[:, sl].astype(F32) + ob_ref[:, sl].astype(F32)
        inv = lax.rsqrt(jnp.mean(o * o, axis=-1, keepdims=True) + RMS_EPS)
        return ((o * inv * gng_ref[...]) * _silu(go_ref[:, sl].astype(F32))).astype(BF16)

    n_chunks = D_FF // FFN_CHUNK
    gated = []
    for c in range(GLA_HEADS):
        ffn_chunk(c)
        gated.append(gate_head(c))
    ffn_chunk(GLA_HEADS)
    merged = jnp.concatenate([attn_ref[...]] + gated, axis=-1)
    mixed = _dot(merged, wo_ref[...])
    ffn_chunk(GLA_HEADS + 1)
    ffn_chunk(GLA_HEADS + 2)
    x1 = _layer_norm(DEEPNORM_ALPHA * x_ref[...] + mixed, ln1g_ref[...], ln1b_ref[...])
    x1_sc[cur] = x1
    x1b_sc[cur] = x1.astype(BF16)
    for c in range(GLA_HEADS + 3, n_chunks):
        ffn_chunk(c)
    ffn = _dot(jnp.concatenate(hidden, axis=-1), wd_ref[...])
    out_ref[...] = _layer_norm(DEEPNORM_ALPHA * x1_sc[prev] + ffn, ln2g_ref[...], ln2b_ref[...])


def _mix_ffn_call(attn, o_f, o_b, go, x, w_out, gng, ln1_g, ln1_b, w_gate, w_up, w_down, ln2_g, ln2_b):
    B, S, D = x.shape
    tm = MIX_FFN_TM
    tokens = B * S
    n = tokens // tm
    flat = lambda a: a.reshape(tokens, a.shape[-1])
    merge_tile = lambda t: (jnp.minimum(t, n - 1), 0)
    ffn_tile = lambda t: (jnp.maximum(t - 1, 0), 0)
    const = lambda t: (0, 0)
    resident = lambda shape: pl.BlockSpec(shape, const, pipeline_mode=pl.Buffered(1))
    out = pl.pallas_call(
        _mix_ffn_kernel, grid=(n + 1,),
        in_specs=[
            pl.BlockSpec((tm, ATTN_WIDTH), merge_tile),
            pl.BlockSpec((tm, GLA_WIDTH), merge_tile),
            pl.BlockSpec((tm, GLA_WIDTH), merge_tile),
            pl.BlockSpec((tm, GLA_WIDTH), merge_tile),
            pl.BlockSpec((tm, D), merge_tile),
            resident((ATTN_WIDTH + GLA_WIDTH, D)),
            pl.BlockSpec((1, GLA_DV), const),
            pl.BlockSpec((1, D), const),
            pl.BlockSpec((1, D), const),
            resident((D, D_FF)), resident((D, D_FF)), resident((D_FF, D)),
            pl.BlockSpec((1, D), const),
            pl.BlockSpec((1, D), const),
        ],
        out_specs=pl.BlockSpec((tm, D), ffn_tile),
        out_shape=jax.ShapeDtypeStruct((tokens, D), F32),
        scratch_shapes=[pltpu.VMEM((2, tm, D), F32), pltpu.VMEM((2, tm, D), BF16)],
        compiler_params=_cparams(("arbitrary",)),
    )(flat(attn), flat(o_f), flat(o_b), flat(go), flat(x), w_out, gng, ln1_g, ln1_b,
      w_gate, w_up, w_down, ln2_g, ln2_b)
    return out.reshape(B, S, D)


def _rope_tables(seq_len):
    t = jnp.arange(seq_len, dtype=jnp.int32)
    row_id = (t // GRID_W).astype(F32)
    col_id = (t % GRID_W).astype(F32)
    inv_freq = ROPE_THETA ** (-jnp.arange(0, AXIAL_DIM, 2, dtype=F32) / AXIAL_DIM)
    lane = jnp.arange(LANES, dtype=jnp.int32)
    d = lane % HEAD_DIM
    is_col = (d // AXIAL_DIM) == 1
    upper = ((d % AXIAL_DIM) // ROPE_HALF) == 1
    ang = jnp.where(is_col, col_id[:, None], row_id[:, None]) * jnp.tile(inv_freq, LANES // ROPE_HALF)
    sin = jnp.sin(ang)
    return jnp.cos(ang), jnp.where(upper, sin, 0.0), jnp.where(upper, 0.0, -sin)


def kernel(x, w_in, q_norm_g, k_norm_g, gate_up_fwd, gate_bias_fwd, gate_up_bwd, gate_bias_bwd, gla_norm_g,
           w_out, ln1_g, ln1_b, w_ffn_gate, w_ffn_up, w_ffn_down, ln2_g, ln2_b):
    B, S, D = x.shape
    assert D == D_MODEL and S % max(PROJ_TM, ATTN_TQ, ATTN_TK, GLA_CHUNK, MIX_FFN_TM) == 0 and B % GLA_BATCH == 0
    assert w_in.shape[0] == DEPTH
    cos, sin_hi, sin_lo = _rope_tables(S)
    for layer in range(DEPTH):
        w_pad = jnp.pad(w_in[layer], ((0, 0), (0, PROJ_PAD_WIDTH - w_in.shape[-1]))).astype(BF16)
        gup = jnp.zeros((Z_PAD, 2 * GLA_QK_WIDTH), F32)
        gup = gup.at[:GATE_RANK, :GLA_QK_WIDTH].set(gate_up_fwd[layer])
        gup = gup.at[GATE_RANK:2 * GATE_RANK, GLA_QK_WIDTH:].set(gate_up_bwd[layer]).astype(BF16)
        gbias = jnp.concatenate([gate_bias_fwd[layer], gate_bias_bwd[layer]])[None, :]
        qg = jnp.tile(q_norm_g[layer], LANES // HEAD_DIM)[None, :]
        kg = jnp.tile(k_norm_g[layer], LANES // HEAD_DIM)[None, :]

        qa, ka, va, gq, gk, gv, go, laf, lab = _proj_call(x, w_pad, gup, gbias, qg, kg, cos, sin_hi, sin_lo)
        score_bound = (jnp.max(jnp.abs(q_norm_g[layer])) * jnp.max(jnp.abs(k_norm_g[layer]))
                       * SCORE_BOUND_SCALE).reshape(1).astype(F32)
        attn = _attn_call(score_bound, qa, ka, va)
        o_f, o_b = _gla_call(gq, gk, gv, laf, lab)
        x = _mix_ffn_call(attn, o_f, o_b, go, x, w_out[layer].astype(BF16), gla_norm_g[layer][None, :],
                          ln1_g[layer][None, :], ln1_b[layer][None, :],
                          w_ffn_gate[layer].astype(BF16), w_ffn_up[layer].astype(BF16),
                          w_ffn_down[layer].astype(BF16), ln2_g[layer][None, :], ln2_b[layer][None, :])
    return x
```

```python
import jax
import jax.numpy as jnp
from jax import lax
from jax.experimental import pallas as pl
from jax.experimental.pallas import tpu as pltpu

F32 = jnp.float32
BF16 = jnp.bfloat16

D_MODEL = 1024
GRID_W = 64
N_Q_HEADS = 8
N_KV_HEADS = 2
Q_PER_KV = N_Q_HEADS // N_KV_HEADS
HEAD_DIM = 64
AXIAL_DIM = HEAD_DIM // 2
ROPE_HALF = AXIAL_DIM // 2
ROPE_THETA = 10000.0
GLA_HEADS = 4
GLA_DK = 64
GLA_DV = 128
GATE_RANK = 16
GATE_TAU = 16.0
GLA_REF_CHUNK = 16
ATTN_WIDTH = N_Q_HEADS * HEAD_DIM
KV_WIDTH = N_KV_HEADS * HEAD_DIM
GLA_QK_WIDTH = GLA_HEADS * GLA_DK
GLA_WIDTH = GLA_HEADS * GLA_DV
D_FF = 2816
DEPTH = 1
DEEPNORM_ALPHA = (2 * DEPTH) ** 0.25
LN_EPS = 1e-5
RMS_EPS = 1e-6
LOG2_E = 1.4426950408889634

LANES = 128
SUBLANES = 8
VMEM_LIMIT_BYTES = 56 * 1024 * 1024

PROJ_TM = 1024
ATTN_TQ = 512
ATTN_TK = 256
ATTN_COL = 512
VT_ROWS = HEAD_DIM + 16
SCORE_BOUND_SCALE = LOG2_E * HEAD_DIM ** 0.5 * 1.01
ATTN_UNSHIFTED_MAX_LOG2 = 60.0
GLA_CHUNK = 128
GLA_BATCH = 8
GLA_PHASES = 4
MIX_FFN_TM = 512
FFN_CHUNK = 256

OFF_AQ = 0
OFF_AK = OFF_AQ + ATTN_WIDTH
OFF_AV = OFF_AK + KV_WIDTH
OFF_GQ = OFF_AV + KV_WIDTH
OFF_GK = OFF_GQ + GLA_QK_WIDTH
OFF_GV = OFF_GK + GLA_QK_WIDTH
OFF_GO = OFF_GV + GLA_WIDTH
OFF_Z = OFF_GO + GLA_WIDTH
Z_PAD = LANES
PROJ_PAD_WIDTH = OFF_Z + Z_PAD


def _cparams(semantics):
    return pltpu.CompilerParams(dimension_semantics=semantics, vmem_limit_bytes=VMEM_LIMIT_BYTES)


def _dot(a, b):
    return jnp.dot(a, b, preferred_element_type=F32)


def _dot_nt(a, b):
    return lax.dot_general(a, b, (((1,), (1,)), ((), ())), preferred_element_type=F32)


def _rope128(y, cos, sin_hi, sin_lo):
    return y * cos + pltpu.roll(y, ROPE_HALF, 1) * sin_hi + pltpu.roll(y, LANES - ROPE_HALF, 1) * sin_lo


def _head_pair_inv_rms(blk, lane_lo):
    sq = blk * blk
    ss_lo = jnp.sum(jnp.where(lane_lo, sq, 0.0), axis=-1, keepdims=True)
    ss_hi = jnp.sum(jnp.where(lane_lo, 0.0, sq), axis=-1, keepdims=True)
    inv = 1.0 / HEAD_DIM
    return jnp.where(lane_lo, lax.rsqrt(ss_lo * inv + RMS_EPS), lax.rsqrt(ss_hi * inv + RMS_EPS))


def _proj_kernel(x_ref, w_ref, gup_ref, gbias_ref, qg_ref, kg_ref, cos_ref, shi_ref, slo_ref,
                 qa_ref, ka_ref, va_ref, gq_ref, gk_ref, gv_ref, go_ref, laf_ref, lab_ref,
                 aq_sc, akv_sc, z_sc):
    t = pl.program_id(0)
    cur, prev = t % 2, (t + 1) % 2

    @pl.when(t == 0)
    def _():
        aq_sc[1] = jnp.zeros(aq_sc.shape[1:], F32)
        akv_sc[1] = jnp.zeros(akv_sc.shape[1:], F32)
        z_sc[1] = jnp.zeros(z_sc.shape[1:], F32)

    x = x_ref[0].astype(BF16)
    tm = x.shape[0]
    lane = lax.broadcasted_iota(jnp.int32, (tm, LANES), 1)
    lane_lo = lane < HEAD_DIM
    cos, shi, slo = cos_ref[...], shi_ref[...], slo_ref[...]

    def proj(off, width):
        return _dot(x, w_ref[:, off:off + width])

    def q_epilogue(c):
        blk = aq_sc[prev, :, c * LANES:(c + 1) * LANES]
        rot = _rope128(blk * qg_ref[...], cos, shi, slo)
        out_t = (rot * (_head_pair_inv_rms(blk, lane_lo) * (LOG2_E * HEAD_DIM ** -0.5))).T.astype(BF16)
        qa_ref[0, 2 * c] = out_t[:HEAD_DIM]
        qa_ref[0, 2 * c + 1] = out_t[HEAD_DIM:]

    def k_epilogue():
        ak = akv_sc[prev, :, :KV_WIDTH]
        rot = _rope128(ak * kg_ref[...], cos, shi, slo)
        out_t = (rot * _head_pair_inv_rms(ak, lane_lo)).T.astype(BF16)
        ka_ref[0, 0] = out_t[:HEAD_DIM]
        ka_ref[0, 1] = out_t[HEAD_DIM:]

    def v_epilogue():
        av = akv_sc[prev, :, KV_WIDTH:]
        ones_col = jnp.where(lane == HEAD_DIM, 1.0, 0.0)
        va_ref[0, 0] = jnp.where(lane_lo, av, ones_col).T.astype(BF16)
        va_ref[0, 1] = jnp.where(lane_lo, pltpu.roll(av, HEAD_DIM, 1), ones_col).T.astype(BF16)

    def gate_epilogue():
        g = _dot(z_sc[prev].astype(BF16), gup_ref[...]) + gbias_ref[...]
        log2_a = (jnp.minimum(g, 0.0) - jnp.log(1.0 + jnp.exp(-jnp.abs(g)))) * (LOG2_E / GATE_TAU)
        laf_ref[0] = log2_a[:, :GLA_QK_WIDTH]
        lab_ref[0] = log2_a[:, GLA_QK_WIDTH:]

    half = GLA_WIDTH // 2
    gq_ref[0] = (proj(OFF_GQ, GLA_QK_WIDTH) * (GLA_DK ** -0.5)).astype(BF16)
    q_epilogue(0)
    gk_ref[0] = proj(OFF_GK, GLA_QK_WIDTH).astype(BF16)
    q_epilogue(1)
    gv_ref[0, :, :half] = proj(OFF_GV, half).astype(BF16)
    q_epilogue(2)
    gv_ref[0, :, half:] = proj(OFF_GV + half, half).astype(BF16)
    q_epilogue(3)
    go_ref[0, :, :half] = proj(OFF_GO, half).astype(BF16)
    k_epilogue()
    go_ref[0, :, half:] = proj(OFF_GO + half, half).astype(BF16)
    v_epilogue()
    gate_epilogue()
    half_q = ATTN_WIDTH // 2
    aq_sc[cur, :, :half_q] = proj(OFF_AQ, half_q)
    aq_sc[cur, :, half_q:] = proj(OFF_AQ + half_q, half_q)
    akv_sc[cur] = proj(OFF_AK, 2 * KV_WIDTH)
    z_sc[cur] = proj(OFF_Z, Z_PAD)


def _proj_call(x, w_pad, gup, gbias, qg, kg, cos, shi, slo):
    B, S, D = x.shape
    tm = PROJ_TM
    n_s = S // tm
    n = n_s * B
    mm = lambda t: jnp.minimum(t, n - 1)
    ep = lambda t: jnp.maximum(t - 1, 0)
    tok = lambda t: (mm(t) % B, mm(t) // B, 0)
    tok_ep = lambda t: (ep(t) % B, ep(t) // B, 0)
    head_t = lambda t: (ep(t) % B, 0, 0, ep(t) // B)
    const = lambda t: (0, 0)
    tab = lambda t: (ep(t) // B, 0)
    in_specs = [
        pl.BlockSpec((1, tm, D), tok),
        pl.BlockSpec((D, PROJ_PAD_WIDTH), const),
        pl.BlockSpec((Z_PAD, 2 * GLA_QK_WIDTH), const),
        pl.BlockSpec((1, 2 * GLA_QK_WIDTH), const),
        pl.BlockSpec((1, LANES), const),
        pl.BlockSpec((1, LANES), const),
        pl.BlockSpec((tm, LANES), tab),
        pl.BlockSpec((tm, LANES), tab),
        pl.BlockSpec((tm, LANES), tab),
    ]
    out_shape = [
        jax.ShapeDtypeStruct((B, N_Q_HEADS, HEAD_DIM, S), BF16),
        jax.ShapeDtypeStruct((B, N_KV_HEADS, HEAD_DIM, S), BF16),
        jax.ShapeDtypeStruct((B, N_KV_HEADS, LANES, S), BF16),
        jax.ShapeDtypeStruct((B, S, GLA_QK_WIDTH), BF16),
        jax.ShapeDtypeStruct((B, S, GLA_QK_WIDTH), BF16),
        jax.ShapeDtypeStruct((B, S, GLA_WIDTH), BF16),
        jax.ShapeDtypeStruct((B, S, GLA_WIDTH), BF16),
        jax.ShapeDtypeStruct((B, S, GLA_QK_WIDTH), F32),
        jax.ShapeDtypeStruct((B, S, GLA_QK_WIDTH), F32),
    ]
    out_specs = [
        pl.BlockSpec((1, N_Q_HEADS, HEAD_DIM, tm), head_t),
        pl.BlockSpec((1, N_KV_HEADS, HEAD_DIM, tm), head_t),
        pl.BlockSpec((1, N_KV_HEADS, LANES, tm), head_t),
        pl.BlockSpec((1, tm, GLA_QK_WIDTH), tok),
        pl.BlockSpec((1, tm, GLA_QK_WIDTH), tok),
        pl.BlockSpec((1, tm, GLA_WIDTH), tok),
        pl.BlockSpec((1, tm, GLA_WIDTH), tok),
        pl.BlockSpec((1, tm, GLA_QK_WIDTH), tok_ep),
        pl.BlockSpec((1, tm, GLA_QK_WIDTH), tok_ep),
    ]
    scratch = [pltpu.VMEM((2, tm, ATTN_WIDTH), F32), pltpu.VMEM((2, tm, 2 * KV_WIDTH), F32),
               pltpu.VMEM((2, tm, Z_PAD), F32)]
    return pl.pallas_call(
        _proj_kernel, grid=(n + 1,), in_specs=in_specs, out_specs=out_specs, out_shape=out_shape,
        scratch_shapes=scratch, compiler_params=_cparams(("arbitrary",)),
    )(x, w_pad, gup, gbias, qg, kg, cos, shi, slo)


def _attn_kernel(bound_ref, qt_ref, kt_ref, vt_ref, o_ref, acc_sc, den_sc):
    t = pl.program_id(0)
    cur, prev = t % 2, (t + 1) % 2
    tq = qt_ref.shape[3]
    n_keys = kt_ref.shape[3]
    tk = ATTN_TK
    key_chunk = lambda j: kt_ref[0, 0, :, j * tk:(j + 1) * tk].T
    q_t = jnp.concatenate([qt_ref[0, r] for r in range(Q_PER_KV)], axis=1)
    unshifted_is_safe = bound_ref[0] < ATTN_UNSHIFTED_MAX_LOG2

    @pl.when(t == 0)
    def _():
        acc_sc[1] = jnp.zeros(acc_sc.shape[1:], F32)
        den_sc[1] = jnp.ones(den_sc.shape[1:], F32)

    def finish_previous():
        o_t = acc_sc[prev] / den_sc[prev]
        for r in range(Q_PER_KV):
            o_ref[0, :, r * HEAD_DIM:(r + 1) * HEAD_DIM] = o_t[:, r * tq:(r + 1) * tq].T.astype(o_ref.dtype)

    def finish(acc, den_row):
        acc_sc[cur] = acc[:HEAD_DIM]
        den_sc[cur] = den_row

    @pl.when(unshifted_is_safe)
    def _():
        n = n_keys // tk
        n_col = q_t.shape[1] // ATTN_COL
        col = lambda c: slice(c * ATTN_COL, (c + 1) * ATTN_COL)
        keys = [key_chunk(j) for j in range(n)]
        scores = lambda j, c: _dot(keys[j], q_t[:, col(c)])
        acc = [None] * n_col
        den = [None] * n_col
        s_cur = [scores(0, c) for c in range(n_col)]
        for j in range(n):
            s_nxt = []
            for c in range(n_col):
                p = jnp.exp2(s_cur[c])
                pv = _dot(vt_ref[0, 0, :HEAD_DIM, j * tk:(j + 1) * tk], p.astype(BF16))
                acc[c] = pv if acc[c] is None else acc[c] + pv
                part = jnp.sum(p.reshape(tk // SUBLANES, SUBLANES, ATTN_COL), axis=0)
                den[c] = part if den[c] is None else den[c] + part
                if j + 1 < n:
                    s_nxt.append(scores(j + 1, c))
            s_cur = s_nxt
            if j == 0:
                finish_previous()
        den_row = jnp.sum(jnp.concatenate(den, axis=1), axis=0, keepdims=True)
        finish(jnp.concatenate(acc, axis=1), den_row)

    @pl.when(jnp.logical_not(unshifted_is_safe))
    def _():
        finish_previous()
        m = None
        acc = None
        for j in range(n_keys // tk):
            s_t = _dot(key_chunk(j), q_t)
            m_blk = jnp.max(s_t, axis=0, keepdims=True)
            m_new = m_blk if m is None else jnp.maximum(m, m_blk)
            pv = _dot(vt_ref[0, 0, :VT_ROWS, j * tk:(j + 1) * tk], jnp.exp2(s_t - m_new).astype(BF16))
            acc = pv if acc is None else jnp.exp2(m - m_new) * acc + pv
            m = m_new
        finish(acc, acc[HEAD_DIM:HEAD_DIM + 1])


def _attn_call(score_bound, qa_t, ka_t, va_t):
    B, _, _, S = ka_t.shape
    tq = ATTN_TQ
    nq = S // tq
    n = B * N_KV_HEADS * nq
    acc_blk = lambda t: jnp.minimum(t, n - 1)
    out_blk = lambda t: jnp.maximum(t - 1, 0)
    b_of = lambda u: u // (N_KV_HEADS * nq)
    g_of = lambda u: (u // nq) % N_KV_HEADS
    return pl.pallas_call(
        _attn_kernel, grid=(n + 1,),
        in_specs=[
            pl.BlockSpec(memory_space=pltpu.SMEM),
            pl.BlockSpec((1, Q_PER_KV, HEAD_DIM, tq), lambda t: (b_of(acc_blk(t)), g_of(acc_blk(t)), 0, acc_blk(t) % nq)),
            pl.BlockSpec((1, 1, HEAD_DIM, S), lambda t: (b_of(acc_blk(t)), g_of(acc_blk(t)), 0, 0)),
            pl.BlockSpec((1, 1, LANES, S), lambda t: (b_of(acc_blk(t)), g_of(acc_blk(t)), 0, 0)),
        ],
        out_specs=pl.BlockSpec((1, tq, Q_PER_KV * HEAD_DIM),
                               lambda t: (b_of(out_blk(t)), out_blk(t) % nq, g_of(out_blk(t)))),
        out_shape=jax.ShapeDtypeStruct((B, S, ATTN_WIDTH), BF16),
        scratch_shapes=[pltpu.VMEM((2, HEAD_DIM, Q_PER_KV * tq), F32), pltpu.VMEM((2, 1, Q_PER_KV * tq), F32)],
        compiler_params=_cparams(("arbitrary",)),
    )(score_bound, qa_t, ka_t, va_t)


def _block_row(a, blk, row):
    n, w = a.shape
    a3 = a.reshape(n // blk, blk, w)
    return jnp.broadcast_to(a3[:, row:row + 1, :], a3.shape).reshape(n, w)


def _split3_bf16(a):
    hi = a.astype(BF16)
    r1 = a - hi.astype(F32)
    mid = r1.astype(BF16)
    lo = (r1 - mid.astype(F32)).astype(BF16)
    return hi, mid, lo


def _gla_direction(bi, q_ref, k_ref, v_ref, la_ref, o_ref, state_ref, forward):
    C = q_ref.shape[1]
    row = lax.broadcasted_iota(jnp.int32, (C, C), 0)
    col = lax.broadcasted_iota(jnp.int32, (C, C), 1)
    tri = ((col <= row) if forward else (col >= row)).astype(BF16)
    la = la_ref[bi]
    cum_all = sum(_dot(tri, part) for part in _split3_bf16(la))
    excl_all = cum_all - la
    xr = row ^ col
    valid = (col <= row) if forward else (col > row)
    lane = lax.broadcasted_iota(jnp.int32, (C, LANES), 1)
    lane_lo = lane < GLA_DK
    edge = C - 1 if forward else 0
    srow = lax.broadcasted_iota(jnp.int32, (LANES, 2 * GLA_DV), 0)
    scol = lax.broadcasted_iota(jnp.int32, (LANES, 2 * GLA_DV), 1)
    on_diag = (srow < GLA_DK) == (scol < GLA_DV)
    pairs = range(GLA_HEADS // 2)
    yield

    factors = []
    for pair in pairs:
        sl = slice(pair * LANES, (pair + 1) * LANES)
        q, k = q_ref[bi, :, sl].astype(F32), k_ref[bi, :, sl].astype(F32)
        cum, excl = cum_all[:, sl], excl_all[:, sl]
        total = cum[edge:edge + 1, :]

        base_row = 0 if forward else GLA_REF_CHUNK - 1
        loc = cum - _block_row(excl, GLA_REF_CHUNK, base_row)
        q_lv = [q * jnp.exp2(loc)]
        k_lv = [(k * jnp.exp2(-loc)).astype(BF16)]
        blk = 2 * GLA_REF_CHUNK
        while blk <= C:
            mid = _block_row(excl if forward else cum, blk, blk // 2)
            e = jnp.exp2(-jnp.abs(cum - mid))
            q_lv.append(q * e)
            k_lv.append((k * e).astype(BF16))
            blk *= 2
        q_in = (q * jnp.exp2(cum)).astype(BF16)
        k_out = k * jnp.exp2(total - cum)
        k_out_t = k_out.T.astype(BF16)
        decay_t = jnp.exp2(total).T
        factors.append((q_lv, k_lv, q_in, k_out_t, decay_t))
    yield

    def stack_heads(a):
        return jnp.concatenate([jnp.where(lane_lo, a, 0.0), jnp.where(lane_lo, 0.0, a)], axis=0).astype(BF16)

    xr2 = jnp.concatenate([xr, xr], axis=0)
    valid2 = jnp.concatenate([valid, valid], axis=0)
    scores = []
    for pair in pairs:
        q_lv, k_lv = factors[pair][:2]
        scores2 = _dot_nt(stack_heads(q_lv[-1]), k_lv[-1])
        bound = C // 2
        for lv in range(len(q_lv) - 2, -1, -1):
            scores2 = jnp.where(xr2 < bound, _dot_nt(stack_heads(q_lv[lv]), k_lv[lv]), scores2)
            bound //= 2
        scores.append(jnp.where(valid2, scores2, 0.0).astype(BF16))
    yield

    for pair in pairs:
        q_in, k_out_t, decay_t = factors[pair][2:]
        v2 = v_ref[bi, :, 2 * pair * GLA_DV:2 * (pair + 1) * GLA_DV]
        state = state_ref[bi, pair]
        inter = _dot(q_in, state.astype(BF16))
        for hh in range(2):
            h = 2 * pair + hh
            cols = slice(hh * GLA_DV, (hh + 1) * GLA_DV)
            o = _dot(scores[pair][hh * C:(hh + 1) * C], v2[:, cols]) + inter[:, cols]
            o_ref[bi, :, h * GLA_DV:(h + 1) * GLA_DV] = o.astype(o_ref.dtype)
        state_ref[bi, pair] = jnp.where(on_diag, decay_t * state + _dot(k_out_t, v2), 0.0)
    yield


def _gla_kernel(qf_ref, kf_ref, vf_ref, laf_ref, qb_ref, kb_ref, vb_ref, lab_ref,
                of_ref, ob_ref, sf_ref, sb_ref):
    @pl.when(pl.program_id(1) == 0)
    def _():
        sf_ref[...] = jnp.zeros(sf_ref.shape, F32)
        sb_ref[...] = jnp.zeros(sb_ref.shape, F32)

    chains = []
    for bi in range(GLA_BATCH):
        chains.append(_gla_direction(bi, qf_ref, kf_ref, vf_ref, laf_ref, of_ref, sf_ref, True))
        chains.append(_gla_direction(bi, qb_ref, kb_ref, vb_ref, lab_ref, ob_ref, sb_ref, False))
    for _ in range(GLA_PHASES):
        for chain in chains:
            next(chain)


def _gla_call(gq, gk, gv, laf, lab):
    B, S, _ = gq.shape
    C, nb = GLA_CHUNK, GLA_BATCH
    n = S // C
    fwd = lambda b, c: (b, c, 0)
    bwd = lambda b, c: (b, n - 1 - c, 0)
    qk = lambda im: pl.BlockSpec((nb, C, GLA_QK_WIDTH), im)
    vv = lambda im: pl.BlockSpec((nb, C, GLA_WIDTH), im)
    state = pltpu.VMEM((nb, GLA_HEADS // 2, LANES, 2 * GLA_DV), F32)
    return pl.pallas_call(
        _gla_kernel, grid=(B // nb, n),
        in_specs=[qk(fwd), qk(fwd), vv(fwd), qk(fwd), qk(bwd), qk(bwd), vv(bwd), qk(bwd)],
        out_specs=[vv(fwd), vv(bwd)],
        out_shape=[jax.ShapeDtypeStruct((B, S, GLA_WIDTH), BF16)] * 2,
        scratch_shapes=[state, state],
        compiler_params=_cparams(("arbitrary", "arbitrary")),
    )(gq, gk, gv, laf, gq, gk, gv, lab)


def _layer_norm(y, g, b):
    mu = jnp.mean(y, axis=-1, keepdims=True)
    d = y - mu
    var = jnp.mean(d * d, axis=-1, keepdims=True)
    return d * lax.rsqrt(var + LN_EPS) * g + b


def _silu(g):
    return g * (1.0 / (1.0 + jnp.exp(-g)))


def _mix_ffn_kernel(attn_ref, of_ref, ob_ref, go_ref, x_ref, wo_ref, gng_ref, ln1g_ref, ln1b_ref,
                    wg_ref, wu_ref, wd_ref, ln2g_ref, ln2b_ref, out_ref, x1_sc, x1b_sc):
    t = pl.program_id(0)
    cur, prev = t % 2, (t + 1) % 2

    @pl.when(t == 0)
    def _():
        x1_sc[1] = jnp.zeros(x1_sc.shape[1:], F32)
        x1b_sc[1] = jnp.zeros(x1b_sc.shape[1:], BF16)

    hidden = []

    def ffn_chunk(c):
        sl = slice(c * FFN_CHUNK, (c + 1) * FFN_CHUNK)
        xb = x1b_sc[prev]
        hidden.append((_silu(_dot(xb, wg_ref[:, sl])) * _dot(xb, wu_ref[:, sl])).astype(BF16))

    def gate_head(h):
        sl = slice(h * GLA_DV, (h + 1) * GLA_DV)
        o = of_ref[:, sl].astype(F32) + ob_ref[:, sl].astype(F32)
        inv = lax.rsqrt(jnp.mean(o * o, axis=-1, keepdims=True) + RMS_EPS)
        return ((o * inv * gng_ref[...]) * _silu(go_ref[:, sl].astype(F32))).astype(BF16)

    n_chunks = D_FF // FFN_CHUNK
    gated = []
    for c in range(GLA_HEADS):
        ffn_chunk(c)
        gated.append(gate_head(c))
    ffn_chunk(GLA_HEADS)
    merged = jnp.concatenate([attn_ref[...]] + gated, axis=-1)
    mixed = _dot(merged, wo_ref[...])
    ffn_chunk(GLA_HEADS + 1)
    ffn_chunk(GLA_HEADS + 2)
    x1 = _layer_norm(DEEPNORM_ALPHA * x_ref[...] + mixed, ln1g_ref[...], ln1b_ref[...])
    x1_sc[cur] = x1
    x1b_sc[cur] = x1.astype(BF16)
    for c in range(GLA_HEADS + 3, n_chunks):
        ffn_chunk(c)
    ffn = _dot(jnp.concatenate(hidden, axis=-1), wd_ref[...])
    out_ref[...] = _layer_norm(DEEPNORM_ALPHA * x1_sc[prev] + ffn, ln2g_ref[...], ln2b_ref[...])


def _mix_ffn_call(attn, o_f, o_b, go, x, w_out, gng, ln1_g, ln1_b, w_gate, w_up, w_down, ln2_g, ln2_b):
    B, S, D = x.shape
    tm = MIX_FFN_TM
    tokens = B * S
    n = tokens // tm
    flat = lambda a: a.reshape(tokens, a.shape[-1])
    merge_tile = lambda t: (jnp.minimum(t, n - 1), 0)
    ffn_tile = lambda t: (jnp.maximum(t - 1, 0), 0)
    const = lambda t: (0, 0)
    resident = lambda shape: pl.BlockSpec(shape, const, pipeline_mode=pl.Buffered(1))
    out = pl.pallas_call(
        _mix_ffn_kernel, grid=(n + 1,),
        in_specs=[
            pl.BlockSpec((tm, ATTN_WIDTH), merge_tile),
            pl.BlockSpec((tm, GLA_WIDTH), merge_tile),
            pl.BlockSpec((tm, GLA_WIDTH), merge_tile),
            pl.BlockSpec((tm, GLA_WIDTH), merge_tile),
            pl.BlockSpec((tm, D), merge_tile),
            resident((ATTN_WIDTH + GLA_WIDTH, D)),
            pl.BlockSpec((1, GLA_DV), const),
            pl.BlockSpec((1, D), const),
            pl.BlockSpec((1, D), const),
            resident((D, D_FF)), resident((D, D_FF)), resident((D_FF, D)),
            pl.BlockSpec((1, D), const),
            pl.BlockSpec((1, D), const),
        ],
        out_specs=pl.BlockSpec((tm, D), ffn_tile),
        out_shape=jax.ShapeDtypeStruct((tokens, D), F32),
        scratch_shapes=[pltpu.VMEM((2, tm, D), F32), pltpu.VMEM((2, tm, D), BF16)],
        compiler_params=_cparams(("arbitrary",)),
    )(flat(attn), flat(o_f), flat(o_b), flat(go), flat(x), w_out, gng, ln1_g, ln1_b,
      w_gate, w_up, w_down, ln2_g, ln2_b)
    return out.reshape(B, S, D)


def _rope_tables(seq_len):
    rows = seq_len // GRID_W
    inv_freq = ROPE_THETA ** (-jnp.arange(0, AXIAL_DIM, 2, dtype=F32) / AXIAL_DIM)
    lane = jnp.arange(LANES, dtype=jnp.int32)
    d = lane % HEAD_DIM
    is_col = (d // AXIAL_DIM) == 1
    upper = ((d % AXIAL_DIM) // ROPE_HALF) == 1
    freq = jnp.tile(inv_freq, LANES // ROPE_HALF)
    ang_row = jnp.arange(rows, dtype=F32)[:, None] * freq
    ang_col = jnp.arange(GRID_W, dtype=F32)[:, None] * freq

    def expand(f):
        per_token = jnp.where(is_col, f(ang_col)[None, :, :], f(ang_row)[:, None, :])
        return per_token.reshape(seq_len, LANES)

    cos, sin = expand(jnp.cos), expand(jnp.sin)
    return cos, jnp.where(upper, sin, 0.0), jnp.where(upper, 0.0, -sin)


def kernel(x, w_in, q_norm_g, k_norm_g, gate_up_fwd, gate_bias_fwd, gate_up_bwd, gate_bias_bwd, gla_norm_g,
           w_out, ln1_g, ln1_b, w_ffn_gate, w_ffn_up, w_ffn_down, ln2_g, ln2_b):
    B, S, D = x.shape
    assert D == D_MODEL and S % max(PROJ_TM, ATTN_TQ, ATTN_TK, GLA_CHUNK, MIX_FFN_TM) == 0 and B % GLA_BATCH == 0
    assert w_in.shape[0] == DEPTH
    cos, sin_hi, sin_lo = _rope_tables(S)
    for layer in range(DEPTH):
        w_pad = jnp.pad(w_in[layer], ((0, 0), (0, PROJ_PAD_WIDTH - w_in.shape[-1]))).astype(BF16)
        gup = jnp.zeros((Z_PAD, 2 * GLA_QK_WIDTH), F32)
        gup = gup.at[:GATE_RANK, :GLA_QK_WIDTH].set(gate_up_fwd[layer])
        gup = gup.at[GATE_RANK:2 * GATE_RANK, GLA_QK_WIDTH:].set(gate_up_bwd[layer]).astype(BF16)
        gbias = jnp.concatenate([gate_bias_fwd[layer], gate_bias_bwd[layer]])[None, :]
        qg = jnp.tile(q_norm_g[layer], LANES // HEAD_DIM)[None, :]
        kg = jnp.tile(k_norm_g[layer], LANES // HEAD_DIM)[None, :]

        qa, ka, va, gq, gk, gv, go, laf, lab = _proj_call(x, w_pad, gup, gbias, qg, kg, cos, sin_hi, sin_lo)
        score_bound = (jnp.max(jnp.abs(q_norm_g[layer])) * jnp.max(jnp.abs(k_norm_g[layer]))
                       * SCORE_BOUND_SCALE).reshape(1).astype(F32)
        attn = _attn_call(score_bound, qa, ka, va)
        o_f, o_b = _gla_call(gq, gk, gv, laf, lab)
        x = _mix_ffn_call(attn, o_f, o_b, go, x, w_out[layer].astype(BF16), gla_norm_g[layer][None, :],
                          ln1_g[layer][None, :], ln1_b[layer][None, :],
                          w_ffn_gate[layer].astype(BF16), w_ffn_up[layer].astype(BF16),
                          w_ffn_down[layer].astype(BF16), ln2_g[layer][None, :], ln2_b[layer][None, :])
    return x
```
